```python
import math
import jax, jax.numpy as jnp
from jax import lax
import numpy as np

D_MODEL = 1024
BATCH = 8
SEQ = 2048
DEPTH = 4

GRID_W = 64
CTX_LEN = 256
D_MIX = D_MODEL
ML_HEADS = 4
ML_HEAD_DIM = 96
ML_WIDTH = ML_HEADS * ML_HEAD_DIM
ML_CHUNK = 128
ML_M_INIT = -1e30
ML_GATE_COLS = 2 * 2 * ML_HEADS
RG_BLOCKS = 6
RG_BLOCK_DIM = 64
RG_WIDTH = RG_BLOCKS * RG_BLOCK_DIM
RG_CONV = 4
RG_C = 8.0
S5_GROUPS = 16
S5_GROUP = 16
S5_WIDTH = S5_GROUPS * S5_GROUP
S5_STATE = 64
A_COLS = 4 * ML_WIDTH + ML_GATE_COLS
B_COLS = 2 * RG_WIDTH
C_COLS = S5_WIDTH
D_IN = A_COLS + B_COLS + C_COLS
N_EXPERTS = 32
TOP_K = 4
D_EXPERT = D_MODEL
SWIGLU_LIMIT = 7.0
SWIGLU_ALPHA = 1.702
MOE_BLOCK = 128
EPS = 1e-6

kernel_name = 'hybrid_mlstm_rglru_s5_moe_dit'


def _maybe_flip(t, d, axis):
    return jnp.flip(t, axis=axis) if d else t


def rmsnorm(x, w):
    xf = x.astype(jnp.float32)
    y = xf * lax.rsqrt(jnp.mean(xf * xf, axis=-1, keepdims=True) + EPS)
    return (y * w.astype(jnp.float32)).astype(x.dtype)


def group_rmsnorm(y, n_groups):
    shp = y.shape
    yg = y.astype(jnp.float32).reshape(*shp[:-1], n_groups, shp[-1] // n_groups)
    yg = yg * lax.rsqrt(jnp.mean(yg * yg, axis=-1, keepdims=True) + EPS)
    return yg.reshape(shp)


def modulate(h, shift, scale):
    return h * (1.0 + scale) + shift


def mlstm_chunkwise(q, k, v, ig, lf, state, with_out):
    Bb, H, S, dh = q.shape
    nc, L = S // ML_CHUNK, ML_CHUNK
    q, k, v = [t.reshape(Bb, H, nc, L, dh) for t in (q, k, v)]
    ig, lf = [t.reshape(Bb, H, nc, L) for t in (ig, lf)]
    b = jnp.cumsum(lf, axis=-1)
    b_last = b[..., -1]
    w = b_last[..., None] - b + ig
    m_loc = jnp.max(w, axis=-1)
    e = jnp.exp(w - m_loc[..., None])
    c_loc = jnp.einsum('bhcs,bhcsd,bhcse->bhcde', e, v, k)
    n_loc = jnp.einsum('bhcs,bhcse->bhce', e, k)

    def step(carry, xs):
        c_st, n_st, m_st = carry
        c_l, n_l, m_l, b_l = xs
        m_new = jnp.maximum(b_l + m_st, m_l)
        a = jnp.exp(b_l + m_st - m_new)
        s = jnp.exp(m_l - m_new)
        c_new = a[..., None, None] * c_st + s[..., None, None] * c_l
        n_new = a[..., None] * n_st + s[..., None] * n_l
        return (c_new, n_new, m_new), (c_st, n_st, m_st)

    xs = tuple(jnp.moveaxis(t, 2, 0) for t in (c_loc, n_loc, m_loc, b_last))
    final, starts = lax.scan(step, state, xs)
    if not with_out:
        return None, final
    c0, n0, m0 = [jnp.moveaxis(t, 0, 2) for t in starts]
    tri = jnp.tril(jnp.ones((L, L), dtype=bool))
    dmat = jnp.where(tri, b[..., :, None] - b[..., None, :] + ig[..., None, :], -jnp.inf)
    g = b + m0[..., None]
    m_t = jnp.maximum(jnp.max(dmat, axis=-1), g)
    s_qk = jnp.einsum('bhctd,bhcsd->bhcts', q, k) * jnp.exp(dmat - m_t[..., None])
    inter = jnp.exp(g - m_t)
    num = jnp.einsum('bhcts,bhcsd->bhctd', s_qk, v) + inter[..., None] * jnp.einsum('bhcde,bhcte->bhctd', c0, q)
    den = jnp.sum(s_qk, axis=-1) + inter * jnp.einsum('bhce,bhcte->bhct', n0, q)
    h = num / jnp.maximum(jnp.abs(den), jnp.exp(-m_t))[..., None]
    return h.reshape(Bb, H, S, dh), final


def mlstm_mixer(p_ctx, p_lat, with_ctx):
    def prep(p):
        Bb, S = p.shape[:2]
        p = p.astype(jnp.float32)
        qkvo = p[..., :4 * ML_WIDTH].reshape(Bb, S, 4, ML_HEADS, ML_HEAD_DIM).transpose(2, 0, 3, 1, 4)
        g = p[..., 4 * ML_WIDTH:].reshape(Bb, S, 2, 2, ML_HEADS).transpose(2, 3, 0, 4, 1)
        qkv = (qkvo[0], qkvo[1] * ML_HEAD_DIM ** -0.5, qkvo[2])
        return qkv, qkvo[3], g[0], jax.nn.log_sigmoid(g[1])

    qkv_c, o_c, ig_c, lf_c = prep(p_ctx)
    qkv_l, o_l, ig_l, lf_l = prep(p_lat)
    Bb = p_lat.shape[0]
    zero = (jnp.zeros((Bb, ML_HEADS, ML_HEAD_DIM, ML_HEAD_DIM), jnp.float32),
            jnp.zeros((Bb, ML_HEADS, ML_HEAD_DIM), jnp.float32),
            jnp.full((Bb, ML_HEADS), ML_M_INIT, jnp.float32))
    h_c, h_l = [], []
    for d in range(2):
        hc, st = mlstm_chunkwise(*[_maybe_flip(t, d, 2) for t in qkv_c], _maybe_flip(ig_c[d], d, 2),
                                 _maybe_flip(lf_c[d], d, 2), zero, with_ctx)
        hl, _ = mlstm_chunkwise(*[_maybe_flip(t, d, 2) for t in qkv_l], _maybe_flip(ig_l[d], d, 2),
                                _maybe_flip(lf_l[d], d, 2), st, True)
        h_l.append(_maybe_flip(hl, d, 2))
        if with_ctx:
            h_c.append(_maybe_flip(hc, d, 2))

    def finish(hs, o):
        h = hs[0] + hs[1]
        h = h * lax.rsqrt(jnp.mean(h * h, axis=-1, keepdims=True) + EPS)
        y = jax.nn.sigmoid(o) * h
        Bb2, H, S, dh = y.shape
        return y.transpose(0, 2, 1, 3).reshape(Bb2, S, H * dh)

    y_c = finish(h_c, o_c) if with_ctx else None
    return y_c, finish(h_l, o_l)


def dwconv_centred(x, w, b):
    S = x.shape[1]
    left = RG_CONV // 2
    xp = jnp.pad(x, ((0, 0), (left, RG_CONV - 1 - left), (0, 0)))
    return sum(xp[:, j:j + S] * w[j] for j in range(RG_CONV)) + b


def linear_scan(a, b, h0):
    b = b.at[:, 0].add(a[:, 0] * h0)

    def comb(l, r):
        return (l[0] * r[0], r[0] * l[1] + r[1])
    _, h = lax.associative_scan(comb, (a, b), axis=1)
    return h


def rglru_gates(x, wa, ba, wx, bx, lam):
    Bb, S, W = x.shape
    xg = x.reshape(Bb, S, RG_BLOCKS, RG_BLOCK_DIM)
    r = jax.nn.sigmoid(jnp.einsum('bsgi,gij->bsgj', xg, wa).reshape(Bb, S, W) + ba)
    i = jax.nn.sigmoid(jnp.einsum('bsgi,gij->bsgj', xg, wx).reshape(Bb, S, W) + bx)
    log_a = -RG_C * r * jax.nn.softplus(-lam)
    a = jnp.exp(log_a)
    b = jnp.sqrt(-jnp.expm1(2.0 * log_a)) * (i * x)
    return a, b


def rglru_mixer(p_ctx, p_lat, conv_w, conv_b, wa, ba, wx, bx, lam, with_ctx):
    def prep(p):
        p = p.astype(jnp.float32)
        return dwconv_centred(p[..., :RG_WIDTH], conv_w, conv_b), jax.nn.gelu(p[..., RG_WIDTH:])

    xc, gc = prep(p_ctx)
    xl, gl = prep(p_lat)
    h_c, h_l = [], []
    for d in range(2):
        ac, bc = rglru_gates(xc, wa[d], ba[d], wx[d], bx[d], lam[d])
        sc = linear_scan(_maybe_flip(ac, d, 1), _maybe_flip(bc, d, 1), jnp.zeros_like(xc[:, 0]))
        al, bl = rglru_gates(xl, wa[d], ba[d], wx[d], bx[d], lam[d])
        sl = linear_scan(_maybe_flip(al, d, 1), _maybe_flip(bl, d, 1), sc[:, -1])
        h_l.append(_maybe_flip(sl, d, 1))
        if with_ctx:
            h_c.append(_maybe_flip(sc, d, 1))
    y_l = gl * group_rmsnorm(h_l[0] + h_l[1], RG_BLOCKS)
    y_c = gc * group_rmsnorm(h_c[0] + h_c[1], RG_BLOCKS) if with_ctx else None
    return y_c, y_l


def to_colmajor(x, rows):
    Bb, S, C = x.shape
    return x.reshape(Bb, rows, GRID_W, C).transpose(0, 2, 1, 3).reshape(Bb, S, C)


def from_colmajor(x, rows):
    Bb, S, C = x.shape
    return x.reshape(Bb, GRID_W, rows, C).transpose(0, 2, 1, 3).reshape(Bb, S, C)


def s5_discretise(lam_re, lam_im, log_dt, b_re, b_im):
    lr, li = lam_re.astype(jnp.float32), lam_im.astype(jnp.float32)
    dt = jnp.exp(log_dt.astype(jnp.float32))[:, None]
    mag = jnp.exp(lr * dt)
    ar, ai = mag * jnp.cos(li * dt), mag * jnp.sin(li * dt)
    den = lr * lr + li * li
    cr = ((ar - 1.0) * lr + ai * li) / den
    ci = (ai * lr - (ar - 1.0) * li) / den
    br, bi = b_re.astype(jnp.float32), b_im.astype(jnp.float32)
    bbr = cr[..., None] * br - ci[..., None] * bi
    bbi = cr[..., None] * bi + ci[..., None] * br
    return ar, ai, bbr, bbi


def complex_scan(ar, ai, ur, ui, h0r, h0i):
    a_r = jnp.broadcast_to(ar, ur.shape)
    a_i = jnp.broadcast_to(ai, ur.shape)
    ur = ur.at[:, 0].add(ar * h0r - ai * h0i)
    ui = ui.at[:, 0].add(ar * h0i + ai * h0r)

    def comb(l, r):
        l_ar, l_ai, l_br, l_bi = l
        r_ar, r_ai, r_br, r_bi = r
        return (r_ar * l_ar - r_ai * l_ai, r_ar * l_ai + r_ai * l_ar,
                r_ar * l_br - r_ai * l_bi + r_br, r_ar * l_bi + r_ai * l_br + r_bi)
    _, _, hr, hi = lax.associative_scan(comb, (a_r, a_i, ur, ui), axis=1)
    return hr, hi


def s5_mixer(p_ctx, p_lat, rows, lam_re, lam_im, log_dt, b_re, b_im, c_re, c_im, d_skip, glu_w, glu_b, with_ctx):
    uc = p_ctx.astype(jnp.float32)
    ul = to_colmajor(p_lat.astype(jnp.float32), rows)
    grp = lambda u: u.reshape(*u.shape[:2], S5_GROUPS, S5_GROUP)
    ucg, ulg = grp(uc), grp(ul)
    cr, cim = c_re.astype(jnp.float32), c_im.astype(jnp.float32)

    def readout(hr, hi):
        y = jnp.einsum('bsgn,gpn->bsgp', hr, cr) - jnp.einsum('bsgn,gpn->bsgp', hi, cim)
        return y.reshape(*y.shape[:2], S5_WIDTH)

    y_c, y_l = [], []
    for d in range(2):
        ar, ai, bbr, bbi = s5_discretise(lam_re[d], lam_im[d], log_dt[d], b_re, b_im)
        ur = _maybe_flip(jnp.einsum('bsgp,gnp->bsgn', ucg, bbr), d, 1)
        ui = _maybe_flip(jnp.einsum('bsgp,gnp->bsgn', ucg, bbi), d, 1)
        zc = jnp.zeros_like(ur[:, 0])
        hr, hi = complex_scan(ar, ai, ur, ui, zc, zc)
        if with_ctx:
            y_c.append(_maybe_flip(readout(hr, hi), d, 1))
        vr = _maybe_flip(jnp.einsum('bsgp,gnp->bsgn', ulg, bbr), d, 1)
        vi = _maybe_flip(jnp.einsum('bsgp,gnp->bsgn', ulg, bbi), d, 1)
        gr, gi = complex_scan(ar, ai, vr, vi, hr[:, -1], hi[:, -1])
        y_l.append(_maybe_flip(readout(gr, gi), d, 1))

    def finish(ys, u):
        y = ys[0] + ys[1] + d_skip * u
        g = jax.nn.gelu(y)
        out = g * jax.nn.sigmoid(g @ glu_w + glu_b)
        return group_rmsnorm(out, S5_GROUPS)

    out_l = from_colmajor(finish(y_l, ul), rows)
    out_c = finish(y_c, uc) if with_ctx else None
    return out_c, out_l


def expert_ffn(xb, e, wgu, bgu, wd, bd):
    gu = xb @ wgu[e] + bgu[e]
    gate = jnp.minimum(gu[:, :D_EXPERT], SWIGLU_LIMIT)
    up = jnp.clip(gu[:, D_EXPERT:], -SWIGLU_LIMIT, SWIGLU_LIMIT)
    glu = gate * jax.nn.sigmoid(SWIGLU_ALPHA * gate)
    return ((up + 1.0) * glu) @ wd[e] + bd[e]


def moe(h, router_w, router_b, wgu, bgu, wd, bd):
    T, Dm = h.shape
    logits = (h @ router_w + router_b).astype(jnp.float32)
    top_val, top_idx = lax.top_k(logits, TOP_K)
    gates = jax.nn.softmax(top_val, axis=-1)
    M = T * TOP_K
    flat_e = top_idx.reshape(-1)
    order = jnp.argsort(flat_e)
    e_sorted = flat_e[order]
    tok_sorted = order // TOP_K
    g_sorted = gates.reshape(-1)[order]
    counts = jnp.bincount(flat_e, length=N_EXPERTS)
    starts = jnp.cumsum(counts) - counts
    padded = (counts + MOE_BLOCK - 1) // MOE_BLOCK * MOE_BLOCK
    pad_ends = jnp.cumsum(padded)
    pad_starts = pad_ends - padded
    dest = pad_starts[e_sorted] + (jnp.arange(M) - starts[e_sorted])
    n_blocks = -(-M // MOE_BLOCK) + N_EXPERTS
    P = n_blocks * MOE_BLOCK
    pad_tok = jnp.full((P,), T, jnp.int32).at[dest].set(tok_sorted.astype(jnp.int32))
    pad_g = jnp.zeros((P,), jnp.float32).at[dest].set(g_sorted)
    block_e = jnp.minimum(jnp.searchsorted(pad_ends, jnp.arange(n_blocks) * MOE_BLOCK, side='right'), N_EXPERTS - 1)
    h_pad = jnp.concatenate([h, jnp.zeros((1, Dm), h.dtype)], axis=0)[pad_tok].reshape(n_blocks, MOE_BLOCK, Dm)
    y = lax.map(lambda args: expert_ffn(args[0], args[1], wgu, bgu, wd, bd), (h_pad, block_e))
    y = y.reshape(P, Dm) * pad_g[:, None]
    out = jax.ops.segment_sum(y, pad_tok, num_segments=T + 1)[:T]
    return out.astype(h.dtype)


def setup_inputs(seed: int = 0) -> dict:
    key = jax.random.key(seed)
    ks = jax.random.split(key, 40)
    L = DEPTH

    def nrm(i, shape, scale):
        return scale * jax.random.normal(ks[i], shape, jnp.float32)

    f0 = 4 * ML_WIDTH + 2 * ML_HEADS
    b_in = nrm(8, (L, D_IN), 0.02).at[:, f0:f0 + 2 * ML_HEADS].add(jnp.tile(jnp.linspace(3.0, 6.0, ML_HEADS), 2))
    u = jax.random.uniform(ks[15], (L, 2, RG_WIDTH), jnp.float32, 0.9, 0.999)
    s = u ** (1.0 / RG_C)
    return {
        'x': nrm(0, (BATCH, SEQ, D_MODEL), 1.0),
        'c': nrm(1, (BATCH, D_MODEL), 1.0),
        'ctx': nrm(2, (BATCH, CTX_LEN, D_MODEL), 1.0),
        'c_ctx': nrm(3, (D_MODEL,), 1.0),
        'ada_w': nrm(4, (L, D_MODEL, 6 * D_MODEL), 0.5 * D_MODEL ** -0.5),
        'ada_b': nrm(5, (L, 6 * D_MODEL), 0.02),
        'norm1_w': 1.0 + nrm(6, (L, D_MODEL), 0.02),
        'w_in': nrm(7, (L, D_MODEL, D_IN), D_MODEL ** -0.5),
        'b_in': b_in,
        'rg_conv_w': nrm(9, (L, RG_CONV, RG_WIDTH), RG_CONV ** -0.5),
        'rg_conv_b': nrm(10, (L, RG_WIDTH), 0.02),
        'rg_wa': nrm(11, (L, 2, RG_BLOCKS, RG_BLOCK_DIM, RG_BLOCK_DIM), RG_BLOCK_DIM ** -0.5),
        'rg_ba': nrm(12, (L, 2, RG_WIDTH), 0.02),
        'rg_wx': nrm(13, (L, 2, RG_BLOCKS, RG_BLOCK_DIM, RG_BLOCK_DIM), RG_BLOCK_DIM ** -0.5),
        'rg_bx': nrm(14, (L, 2, RG_WIDTH), 0.02),
        'rg_lambda': jnp.log(s) - jnp.log1p(-s),
        's5_lambda_re': -0.5 + nrm(16, (L, 2, S5_GROUPS, S5_STATE), 0.01),
        's5_lambda_im': jnp.pi * jnp.arange(S5_STATE, dtype=jnp.float32) + nrm(17, (L, 2, S5_GROUPS, S5_STATE), 0.01),
        's5_log_dt': jax.random.uniform(ks[18], (L, 2, S5_GROUPS), jnp.float32, math.log(1e-3), math.log(1e-1)),
        's5_b_re': nrm(19, (L, S5_GROUPS, S5_STATE, S5_GROUP), (2 * S5_GROUP) ** -0.5),
        's5_b_im': nrm(20, (L, S5_GROUPS, S5_STATE, S5_GROUP), (2 * S5_GROUP) ** -0.5),
        's5_c_re': nrm(21, (L, S5_GROUPS, S5_GROUP, S5_STATE), (2 * S5_STATE) ** -0.5),
        's5_c_im': nrm(22, (L, S5_GROUPS, S5_GROUP, S5_STATE), (2 * S5_STATE) ** -0.5),
        's5_d': nrm(23, (L, S5_WIDTH), 1.0),
        's5_glu_w': nrm(24, (L, S5_WIDTH, S5_WIDTH), S5_WIDTH ** -0.5),
        's5_glu_b': nrm(25, (L, S5_WIDTH), 0.02),
        'mix_norm_w': 1.0 + nrm(26, (L, D_MIX), 0.02),
        'w_out': nrm(27, (L, D_MIX, D_MODEL), D_MIX ** -0.5),
        'norm2_w': 1.0 + nrm(28, (L, D_MODEL), 0.02),
        'router_w': nrm(29, (L, D_MODEL, N_EXPERTS), D_MODEL ** -0.5),
        'router_b': nrm(30, (L, N_EXPERTS), 0.01),
        'moe_w_gate_up': nrm(31, (L, N_EXPERTS, D_MODEL, 2 * D_EXPERT), D_MODEL ** -0.5),
        'moe_b_gate_up': nrm(32, (L, N_EXPERTS, 2 * D_EXPERT), 0.02),
        'moe_w_down': nrm(33, (L, N_EXPERTS, D_EXPERT, D_MODEL), D_EXPERT ** -0.5),
        'moe_b_down': nrm(34, (L, N_EXPERTS, D_MODEL), 0.02),
        'final_norm_w': 1.0 + nrm(35, (D_MODEL,), 0.02),
    }


def reference(x, c, ctx, c_ctx, ada_w, ada_b, norm1_w, w_in, b_in, rg_conv_w, rg_conv_b, rg_wa, rg_ba,
              rg_wx, rg_bx, rg_lambda, s5_lambda_re, s5_lambda_im, s5_log_dt, s5_b_re, s5_b_im, s5_c_re,
              s5_c_im, s5_d, s5_glu_w, s5_glu_b, mix_norm_w, w_out, norm2_w, router_w, router_b,
              moe_w_gate_up, moe_b_gate_up, moe_w_down, moe_b_down, final_norm_w):
    Bb, S, Dm = x.shape
    Sc = ctx.shape[1]
    ROWS = S // GRID_W
    xl, xc = x, ctx
    sil_c = jax.nn.silu(c)
    sil_cc = jax.nn.silu(c_ctx)
    for l in range(DEPTH):
        with_ctx = l < DEPTH - 1
        mod_l = (sil_c @ ada_w[l] + ada_b[l])[:, None, :]
        mod_c = sil_cc @ ada_w[l] + ada_b[l]
        sh1_l, sc1_l, g1_l, sh2_l, sc2_l, g2_l = jnp.split(mod_l, 6, axis=-1)
        sh1_c, sc1_c, g1_c, sh2_c, sc2_c, g2_c = jnp.split(mod_c, 6, axis=-1)

        hl = modulate(rmsnorm(xl, norm1_w[l]), sh1_l, sc1_l)
        hc = modulate(rmsnorm(xc, norm1_w[l]), sh1_c, sc1_c)
        pl = hl @ w_in[l] + b_in[l]
        pc = hc @ w_in[l] + b_in[l]
        pl_a, pl_b, pl_c = jnp.split(pl, [A_COLS, A_COLS + B_COLS], axis=-1)
        pc_a, pc_b, pc_c = jnp.split(pc, [A_COLS, A_COLS + B_COLS], axis=-1)
        ya_c, ya_l = mlstm_mixer(pc_a, pl_a, with_ctx)
        yb_c, yb_l = rglru_mixer(pc_b, pl_b, rg_conv_w[l], rg_conv_b[l], rg_wa[l], rg_ba[l],
                                 rg_wx[l], rg_bx[l], rg_lambda[l], with_ctx)
        yc_c, yc_l = s5_mixer(pc_c, pl_c, ROWS, s5_lambda_re[l], s5_lambda_im[l], s5_log_dt[l], s5_b_re[l],
                              s5_b_im[l], s5_c_re[l], s5_c_im[l], s5_d[l], s5_glu_w[l], s5_glu_b[l], with_ctx)
        merged_l = (jnp.concatenate([ya_l, yb_l, yc_l], axis=-1) * mix_norm_w[l]).astype(x.dtype)
        xl = xl + g1_l * (merged_l @ w_out[l])
        if with_ctx:
            merged_c = (jnp.concatenate([ya_c, yb_c, yc_c], axis=-1) * mix_norm_w[l]).astype(x.dtype)
            xc = xc + g1_c * (merged_c @ w_out[l])

        hl = modulate(rmsnorm(xl, norm2_w[l]), sh2_l, sc2_l)
        if with_ctx:
            hc = modulate(rmsnorm(xc, norm2_w[l]), sh2_c, sc2_c)
            tokens = jnp.concatenate([hc.reshape(-1, Dm), hl.reshape(-1, Dm)], axis=0)
        else:
            tokens = hl.reshape(-1, Dm)
        f_out = moe(tokens, router_w[l], router_b[l], moe_w_gate_up[l], moe_b_gate_up[l],
                    moe_w_down[l], moe_b_down[l])
        if with_ctx:
            xc = xc + g2_c * f_out[:Bb * Sc].reshape(xc.shape)
            f_lat = f_out[Bb * Sc:]
        else:
            f_lat = f_out
        xl = xl + g2_l * f_lat.reshape(xl.shape)
    return rmsnorm(xl, final_norm_w)
```

```python
import functools
import math

import jax
import jax.numpy as jnp
from jax import lax
from jax.experimental import pallas as pl
from jax.experimental.pallas import tpu as pltpu

F32 = jnp.float32
BF16 = jnp.bfloat16
I32 = jnp.int32

D_MODEL = 1024
BATCH = 8
DEPTH = 4
GRID_W = 64
ML_HEADS = 4
ML_HEAD_DIM = 96
ML_CHUNK = 128
ML_M_INIT = -1e30
RG_BLOCKS = 6
RG_BLOCK_DIM = 64
RG_WIDTH = RG_BLOCKS * RG_BLOCK_DIM
RG_CONV = 4
RG_C = 8.0
S5_GROUPS = 16
S5_GROUP = 16
S5_WIDTH = S5_GROUPS * S5_GROUP
S5_STATE = 64
S5_NSTATE = S5_GROUPS * S5_STATE
N_EXPERTS = 32
TOP_K = 4
D_EXPERT = D_MODEL
SWIGLU_LIMIT = 7.0
SWIGLU_ALPHA = 1.702
EPS = 1e-6

LANES = 128
SUBLANES = 8

ML_SLABS = 4 * ML_HEADS + 1
ML_GATE_SLAB = 4 * ML_HEADS
A_PAD = ML_SLABS * LANES
MERGED = ML_HEADS * LANES + RG_WIDTH + S5_WIDTH

ROW_TILE = 512
T_CHUNK = 128
CHUNK_ROWS = T_CHUNK * BATCH
MOE_BM = 256
COMBINE_TILE = 256
VMEM_LIMIT = 56 * 1024 * 1024


def _cparams(*sem):
    return pltpu.CompilerParams(dimension_semantics=sem, vmem_limit_bytes=VMEM_LIMIT)


def _sigmoid(x):
    return 1.0 / (1.0 + jnp.exp(-x))


def _log_sigmoid(x):
    return jnp.minimum(x, 0.0) - jnp.log1p(jnp.exp(-jnp.abs(x)))


def _softplus(x):
    return jnp.maximum(x, 0.0) + jnp.log1p(jnp.exp(-jnp.abs(x)))


def _gelu(x):
    return 0.5 * x * (1.0 + jnp.tanh(0.7978845608028654 * (x + 0.044715 * (x * x * x))))


def _dot(a, b):
    return jnp.dot(a, b, preferred_element_type=F32)


def _dot_nt(a, b):
    return lax.dot_general(a, b, (((1,), (1,)), ((), ())), preferred_element_type=F32)


def _split(a):
    hi = a.astype(BF16)
    lo = (a - hi.astype(F32)).astype(BF16)
    return hi, lo


def _dot_lx(a_exact, b):
    hi, lo = _split(b)
    return _dot(a_exact, hi) + _dot(a_exact, lo)


def _dot_xr(a, b_exact):
    hi, lo = _split(a)
    return _dot(hi, b_exact) + _dot(lo, b_exact)


def _dot3(a, b, nt=False):
    ah, al = _split(a)
    bh, bl = _split(b)
    d = _dot_nt if nt else _dot
    return d(ah, bh) + (d(ah, bl) + d(al, bh))


def _group_ones(n, shift):
    r = lax.broadcasted_iota(I32, (n, n), 0)
    c = lax.broadcasted_iota(I32, (n, n), 1)
    return (lax.shift_right_logical(r, shift) == lax.shift_right_logical(c, shift)).astype(BF16)


def _mod_kernel(c_ref, w_ref, b_ref, o_ref):
    c = c_ref[...]
    s = c * _sigmoid(c)
    o_ref[0] = _dot3(s, w_ref[0]) + b_ref[0]


def _modulation(c_rows, ada_w, ada_b):
    depth, d, n = ada_w.shape
    tn = 1536
    return pl.pallas_call(
        _mod_kernel,
        out_shape=jax.ShapeDtypeStruct((depth, 16, n), F32),
        grid=(depth, n // tn),
        in_specs=[
            pl.BlockSpec((16, d), lambda l, j: (0, 0)),
            pl.BlockSpec((1, d, tn), lambda l, j: (l, 0, j)),
            pl.BlockSpec((1, 1, tn), lambda l, j: (l, 0, j)),
        ],
        out_specs=pl.BlockSpec((1, 16, tn), lambda l, j: (l, 0, j)),
        compiler_params=_cparams("arbitrary", "arbitrary"),
        name="adaln_mod",
    )(c_rows, ada_w, ada_b.reshape(depth, 1, n))


def _inproj_kernel(x_ref, mod_ref, nw_ref, w_ref, b_ref, pa_ref, pb_ref, pcc_ref, pcl_ref, *, n_ctx_tiles):
    i = pl.program_id(0)
    x = x_ref[...]
    tm = x.shape[0]
    ms = jnp.mean(x * x, axis=-1, keepdims=True)
    xn = (x * lax.rsqrt(ms + EPS)) * nw_ref[...]
    xn = xn.reshape(tm // BATCH, BATCH, D_MODEL)
    h = xn * (1.0 + mod_ref[0, 1][None]) + mod_ref[0, 0][None]
    h = h.reshape(tm, D_MODEL).astype(BF16)
    for j in range(0, ML_SLABS - 1, 2):
        p = _dot(h, w_ref[:, j * LANES:(j + 2) * LANES]) + b_ref[:, j * LANES:(j + 2) * LANES]
        pa_ref[j] = p[:, :LANES]
        pa_ref[j + 1] = p[:, LANES:]
    c0 = ML_GATE_SLAB * LANES
    p = _dot(h, w_ref[:, c0:c0 + LANES]) + b_ref[:, c0:c0 + LANES]
    pa_ref[ML_GATE_SLAB] = p
    c0 = A_PAD
    pb_ref[...] = _dot(h, w_ref[:, c0:c0 + 2 * RG_WIDTH]) + b_ref[:, c0:c0 + 2 * RG_WIDTH]
    c0 = A_PAD + 2 * RG_WIDTH
    pc = _dot(h, w_ref[:, c0:c0 + S5_WIDTH]) + b_ref[:, c0:c0 + S5_WIDTH]

    @pl.when(i < n_ctx_tiles)
    def _():
        pcc_ref[...] = pc

    @pl.when(i >= n_ctx_tiles)
    def _():
        pcl_ref[...] = pc.reshape(GRID_W, 1, BATCH, S5_WIDTH)


def _inproj(x, mod, nw, w, b, n_ctx_rows):
    nt = x.shape[0]
    tm = ROW_TILE
    nct = n_ctx_rows // tm
    n_lat_rows = nt - n_ctx_rows
    rows = n_lat_rows // tm
    ncols = w.shape[1]
    return pl.pallas_call(
        functools.partial(_inproj_kernel, n_ctx_tiles=nct),
        out_shape=(
            jax.ShapeDtypeStruct((ML_SLABS, nt, LANES), F32),
            jax.ShapeDtypeStruct((nt, 2 * RG_WIDTH), F32),
            jax.ShapeDtypeStruct((n_ctx_rows, S5_WIDTH), F32),
            jax.ShapeDtypeStruct((GRID_W, rows, BATCH, S5_WIDTH), F32),
        ),
        grid=(nt // tm,),
        in_specs=[
            pl.BlockSpec((tm, D_MODEL), lambda i: (i, 0)),
            pl.BlockSpec((1, 2, BATCH, D_MODEL), lambda i: (jnp.where(i < nct, 0, 1), 0, 0, 0)),
            pl.BlockSpec((1, D_MODEL), lambda i: (0, 0)),
            pl.BlockSpec((D_MODEL, ncols), lambda i: (0, 0)),
            pl.BlockSpec((1, ncols), lambda i: (0, 0)),
        ],
        out_specs=(
            pl.BlockSpec((ML_SLABS, tm, LANES), lambda i: (0, i, 0)),
            pl.BlockSpec((tm, 2 * RG_WIDTH), lambda i: (i, 0)),
            pl.BlockSpec((tm, S5_WIDTH), lambda i: (jnp.minimum(i, nct - 1), 0)),
            pl.BlockSpec((GRID_W, 1, BATCH, S5_WIDTH), lambda i: (0, jnp.maximum(i - nct, 0), 0, 0)),
        ),
        compiler_params=_cparams("arbitrary"),
        name="inproj",
    )(x, mod, nw, w, b)


def _mlstm_kernel(*refs, direction, final, n_chunks):
    if final:
        pa_ref, hb_ref, out_ref, c_sc, m_sc = refs
    else:
        pa_ref, out_ref, c_sc, m_sc = refs
        hb_ref = None
    i = pl.program_id(0)
    L = ML_CHUNK

    @pl.when(i == 0)
    def _():
        c_sc[...] = jnp.zeros(c_sc.shape, F32)
        m_sc[...] = jnp.full(m_sc.shape, ML_M_INIT, F32)

    row = lax.broadcasted_iota(I32, (L, L), 0)
    col = lax.broadcasted_iota(I32, (L, L), 1)
    causal = (row >= col) if direction == 0 else (row <= col)
    tri = causal.astype(BF16)
    tri_t = ((row <= col) if direction == 0 else (row >= col)).astype(BF16)
    ones = jnp.ones((L, L), BF16)
    head_lane = col < ML_HEAD_DIM

    def per_batch(b, carry):
        def ld(ref, j):
            return ref[j, pl.ds(b, L, stride=BATCH), :]

        gates = ld(pa_ref, ML_GATE_SLAB)
        lf = _log_sigmoid(gates)
        bcol = _dot_lx(tri, lf)
        btot = _dot_lx(ones, lf)
        gates_t = gates.T
        brow = _dot_xr(lf.T, tri_t)
        for h in range(ML_HEADS):
            ig_c = ML_HEADS * direction + h
            fg_c = 2 * ML_HEADS + ML_HEADS * direction + h
            q = ld(pa_ref, h).astype(BF16)
            k = ld(pa_ref, ML_HEADS + h).astype(BF16)
            v = ld(pa_ref, 2 * ML_HEADS + h)
            bc = bcol[:, fg_c:fg_c + 1]
            bt = btot[:, fg_c:fg_c + 1]
            igc = gates[:, ig_c:ig_c + 1]
            r = gates_t[ig_c:ig_c + 1, :] - brow[fg_c:fg_c + 1, :]
            dm = jnp.where(causal, bc + r, -jnp.inf)
            idx = b * ML_HEADS + h
            m0 = m_sc[idx][:, 0:1]
            c0 = c_sc[idx]
            g = bc + m0
            mt = jnp.maximum(jnp.max(dm, axis=1, keepdims=True), g)
            pmat = jnp.exp(dm - mt)
            sqk = _dot_nt(q, k) * pmat
            inter = jnp.exp(g - mt)
            nc = _dot_nt(q, c0.astype(BF16))
            num = _dot(sqk.astype(BF16), v.astype(BF16)) + inter * nc
            den = jnp.sum(sqk, axis=1, keepdims=True) + inter * nc[:, ML_HEAD_DIM:ML_HEAD_DIM + 1]
            hh = num / jnp.maximum(jnp.abs(den), jnp.exp(-mt))
            hh = jnp.where(head_lane, hh, 0.0)
            wcol = bt - bc + igc
            mloc = jnp.max(wcol, axis=0, keepdims=True)
            ecol = jnp.exp(wcol - mloc)
            cloc = _dot((v * ecol).T.astype(BF16), k)
            mnew = jnp.maximum(bt + m0, mloc)
            a = jnp.exp(bt + m0 - mnew)
            sc = jnp.exp(mloc - mnew)
            c_sc[idx] = a * c0 + sc * cloc
            m_sc[idx] = jnp.broadcast_to(mnew, (L, L))
            if final:
                ht = hh + ld(hb_ref, h)
                ms = jnp.sum(ht * ht, axis=1, keepdims=True) * (1.0 / ML_HEAD_DIM)
                o = ld(pa_ref, 3 * ML_HEADS + h)
                hh = _sigmoid(o) * (ht * lax.rsqrt(ms + EPS))
            out_ref[h, pl.ds(b, L, stride=BATCH), :] = hh
        return carry

    lax.fori_loop(0, BATCH, per_batch, 0)


def _chunk_order(i, direction, n_ctx, n_all):
    if direction == 0:
        return i
    return jnp.where(i < n_ctx, n_ctx - 1 - i, n_all - 1 + n_ctx - i)


def _mlstm_pass(pa, hb, direction, n_ctx):
    nt = pa.shape[1]
    n_all = nt // CHUNK_ROWS
    final = hb is not None
    cmap = lambda i: (0, _chunk_order(i, direction, n_ctx, n_all), 0)
    in_specs = [pl.BlockSpec((ML_SLABS, CHUNK_ROWS, LANES), cmap)]
    args = [pa]
    if final:
        in_specs.append(pl.BlockSpec((ML_HEADS, CHUNK_ROWS, LANES), cmap))
        args.append(hb)
    return pl.pallas_call(
        functools.partial(_mlstm_kernel, direction=direction, final=final, n_chunks=n_all),
        out_shape=jax.ShapeDtypeStruct((ML_HEADS, nt, LANES), F32),
        grid=(n_all,),
        in_specs=in_specs,
        out_specs=pl.BlockSpec((ML_HEADS, CHUNK_ROWS, LANES), cmap),
        scratch_shapes=[
            pltpu.VMEM((BATCH * ML_HEADS, ML_CHUNK, LANES), F32),
            pltpu.VMEM((BATCH * ML_HEADS, ML_CHUNK, LANES), F32),
        ],
        compiler_params=_cparams("arbitrary"),
        name="mlstm_fwd" if final else "mlstm_bwd",
    )(*args)


def _rglru_kernel(*refs, direction, final, n_ctx, n_all):
    if final:
        cur_ref, prev_ref, next_ref, hb_ref, cw_ref, cb_ref, wg_ref, bg_ref, lam_ref, out_ref, a_sc, b_sc, h_sc = refs
    else:
        cur_ref, prev_ref, next_ref, cw_ref, cb_ref, wg_ref, bg_ref, lam_ref, out_ref, a_sc, b_sc, h_sc = refs
        hb_ref = None
    i = pl.program_id(0)
    c = _chunk_order(i, direction, n_ctx, n_all)
    w = RG_WIDTH

    @pl.when(i == 0)
    def _():
        h_sc[...] = jnp.zeros(h_sc.shape, F32)

    seg_first = jnp.logical_or(c == 0, c == n_ctx)
    seg_last = jnp.logical_or(c == n_ctx - 1, c == n_all - 1)
    cur = cur_ref[:, :w]
    prev = jnp.where(seg_first, 0.0, prev_ref[:, :w])
    nxt = jnp.where(seg_last, 0.0, next_ref[:, :w])
    xc = jnp.concatenate([prev, cur, nxt], axis=0)
    n = CHUNK_ROWS
    xconv = cb_ref[...] + cw_ref[0:1, :] * xc[0:n]
    for j in range(1, RG_CONV):
        xconv = xconv + cw_ref[j:j + 1, :] * xc[j * BATCH:j * BATCH + n]
    z = _dot(xconv.astype(BF16), wg_ref[...]) + bg_ref[...]
    r = _sigmoid(z[:, :w])
    ig = _sigmoid(z[:, w:])
    log_a = (-RG_C * _softplus(-lam_ref[...])) * r
    a = jnp.exp(log_a)
    a_sc[...] = a
    b_sc[...] = jnp.sqrt(-jnp.tanh(log_a) * (a * a + 1.0)) * (ig * xconv)

    def step(t, h):
        tt = t if direction == 0 else T_CHUNK - 1 - t
        rows = pl.ds(pl.multiple_of(tt * BATCH, BATCH), BATCH)
        h = a_sc[rows, :] * h + b_sc[rows, :]
        b_sc[rows, :] = h
        return h

    h_sc[...] = lax.fori_loop(0, T_CHUNK, step, h_sc[...], unroll=8)
    hs = b_sc[...]
    if final:
        ht = hs + hb_ref[...]
        ss = _dot_xr(ht * ht, _group_ones(w, 6)) * (1.0 / RG_BLOCK_DIM)
        hs = _gelu(cur_ref[:, w:]) * (ht * lax.rsqrt(ss + EPS))
    out_ref[...] = hs


def _rglru_pass(pb, hb, cw, cb, wg, bg, lam, direction, n_ctx):
    nt = pb.shape[0]
    n_all = nt // CHUNK_ROWS
    final = hb is not None
    w2 = 2 * RG_WIDTH
    order = lambda i: _chunk_order(i, direction, n_ctx, n_all)
    halo_p = CHUNK_ROWS // (2 * BATCH)
    halo_n = CHUNK_ROWS // BATCH
    in_specs = [
        pl.BlockSpec((CHUNK_ROWS, w2), lambda i: (order(i), 0)),
        pl.BlockSpec((2 * BATCH, w2), lambda i: (jnp.maximum(order(i) * halo_p - 1, 0), 0)),
        pl.BlockSpec((BATCH, w2), lambda i: (jnp.minimum((order(i) + 1) * halo_n, nt // BATCH - 1), 0)),
    ]
    args = [pb, pb, pb]
    if final:
        in_specs.append(pl.BlockSpec((CHUNK_ROWS, RG_WIDTH), lambda i: (order(i), 0)))
        args.append(hb)
    full = lambda s: pl.BlockSpec(s, lambda i: tuple(0 for _ in s))
    in_specs += [full((RG_CONV, RG_WIDTH)), full((1, RG_WIDTH)), full((RG_WIDTH, w2)), full((1, w2)), full((1, RG_WIDTH))]
    args += [cw, cb, wg, bg, lam]
    return pl.pallas_call(
        functools.partial(_rglru_kernel, direction=direction, final=final, n_ctx=n_ctx, n_all=n_all),
        out_shape=jax.ShapeDtypeStruct((nt, RG_WIDTH), F32),
        grid=(n_all,),
        in_specs=in_specs,
        out_specs=pl.BlockSpec((CHUNK_ROWS, RG_WIDTH), lambda i: (order(i), 0)),
        scratch_shapes=[
            pltpu.VMEM((CHUNK_ROWS, RG_WIDTH), F32),
            pltpu.VMEM((CHUNK_ROWS, RG_WIDTH), F32),
            pltpu.VMEM((BATCH, RG_WIDTH), F32),
        ],
        compiler_params=_cparams("arbitrary"),
        name="rglru_fwd" if final else "rglru_bwd",
    )(*args)


def _s5_kernel(*refs, direction, final):
    if final:
        (u_ref, yb_ref, h0_ref, a_ref, bd_ref, cd_ref, dsk_ref, gw_ref, gb_ref,
         out_ref, hN_ref, v_sc, h_sc) = refs
    else:
        u_ref, h0_ref, a_ref, bd_ref, cd_ref, out_ref, hN_ref, v_sc, h_sc = refs
    i = pl.program_id(0)
    n = S5_NSTATE

    @pl.when(i == 0)
    def _():
        h_sc[...] = h0_ref[...]

    u = u_ref[...]
    v_sc[...] = _dot(u.astype(BF16), bd_ref[...])
    ar = jnp.broadcast_to(a_ref[0:1, :], (BATCH, n))
    ai = jnp.broadcast_to(a_ref[1:2, :], (BATCH, n))

    def step(t, carry):
        hr, hi = carry
        tt = t if direction == 0 else T_CHUNK - 1 - t
        rows = pl.ds(pl.multiple_of(tt * BATCH, BATCH), BATCH)
        nr = ar * hr - ai * hi + v_sc[rows, :n]
        ni = ar * hi + ai * hr + v_sc[rows, n:]
        v_sc[rows, :n] = nr
        v_sc[rows, n:] = ni
        return nr, ni

    hr, hi = lax.fori_loop(0, T_CHUNK, step, (h_sc[:, :n], h_sc[:, n:]))
    h_sc[:, :n] = hr
    h_sc[:, n:] = hi
    hN_ref[...] = h_sc[...]
    y = _dot(v_sc[...].astype(BF16), cd_ref[...])
    if final:
        y = y + yb_ref[...] + dsk_ref[...] * u
        g = _gelu(y)
        o = g * _sigmoid(_dot(g.astype(BF16), gw_ref[...]) + gb_ref[...])
        ss = _dot_xr(o * o, _group_ones(S5_WIDTH, 4)) * (1.0 / S5_GROUP)
        y = o * lax.rsqrt(ss + EPS)
    out_ref[...] = y


def _s5_pass(u, yb, h0, a, bd, cd, dsk, gw, gb, direction):
    nt = u.shape[0]
    n_chunks = nt // CHUNK_ROWS
    final = yb is not None
    order = (lambda i: i) if direction == 0 else (lambda i: n_chunks - 1 - i)
    full = lambda s: pl.BlockSpec(s, lambda i: tuple(0 for _ in s))
    chunk = pl.BlockSpec((CHUNK_ROWS, S5_WIDTH), lambda i: (order(i), 0))
    in_specs = [chunk]
    args = [u]
    if final:
        in_specs.append(chunk)
        args.append(yb)
    in_specs += [full((BATCH, 2 * S5_NSTATE)), full((2, S5_NSTATE)), full((S5_WIDTH, 2 * S5_NSTATE)),
                 full((2 * S5_NSTATE, S5_WIDTH))]
    args += [h0, a, bd, cd]
    if final:
        in_specs += [full((1, S5_WIDTH)), full((S5_WIDTH, S5_WIDTH)), full((1, S5_WIDTH))]
        args += [dsk, gw, gb]
    return pl.pallas_call(
        functools.partial(_s5_kernel, direction=direction, final=final),
        out_shape=(jax.ShapeDtypeStruct((nt, S5_WIDTH), F32), jax.ShapeDtypeStruct((BATCH, 2 * S5_NSTATE), F32)),
        grid=(n_chunks,),
        in_specs=in_specs,
        out_specs=(chunk, full((BATCH, 2 * S5_NSTATE))),
        scratch_shapes=[
            pltpu.VMEM((CHUNK_ROWS, 2 * S5_NSTATE), F32),
            pltpu.VMEM((BATCH, 2 * S5_NSTATE), F32),
        ],
        compiler_params=_cparams("arbitrary"),
        name="s5_fwd" if final else "s5_bwd",
    )(*args)


def _outproj_kernel(x_ref, ya_ref, yb_ref, ycc_ref, ycl_ref, mod_ref, mw_ref, wo_ref, n2_ref, rw_ref, rb_ref,
                    xo_ref, h2_ref, idx_ref, gate_ref, rank_ref, cnt_ref, cnt_sc, *, tile0, n_ctx_tiles):
    i = pl.program_id(0)
    tile = i + tile0
    tm = x_ref.shape[0]

    @pl.when(i == 0)
    def _():
        cnt_sc[...] = jnp.zeros(cnt_sc.shape, F32)

    yc = jnp.where(tile < n_ctx_tiles, ycc_ref[...], ycl_ref[...].reshape(tm, S5_WIDTH))
    merged = jnp.concatenate([ya_ref[h] for h in range(ML_HEADS)] + [yb_ref[...], yc], axis=1)
    merged = (merged * mw_ref[...]).astype(BF16)
    proj = _dot(merged, wo_ref[...])
    g1 = mod_ref[0, 0]
    x = x_ref[...].reshape(tm // BATCH, BATCH, D_MODEL) + g1[None] * proj.reshape(tm // BATCH, BATCH, D_MODEL)
    xo_ref[...] = x.reshape(tm, D_MODEL)
    ms = jnp.mean(x * x, axis=-1, keepdims=True)
    xn = (x * lax.rsqrt(ms + EPS)) * n2_ref[...][None]
    h2 = (xn * (1.0 + mod_ref[0, 2][None]) + mod_ref[0, 1][None]).reshape(tm, D_MODEL)
    h2_ref[...] = h2
    logits = _dot3(rw_ref[...], h2, nt=True) + rb_ref[...]
    eidx = lax.broadcasted_iota(I32, (N_EXPERTS, tm), 0)
    vals, ids, hots = [], [], []
    for _ in range(TOP_K):
        m = jnp.max(logits, axis=0, keepdims=True)
        sel = jnp.min(jnp.where(logits == m, eidx, N_EXPERTS), axis=0, keepdims=True)
        hot = eidx == sel
        logits = jnp.where(hot, -jnp.inf, logits)
        vals.append(m)
        ids.append(sel)
        hots.append(hot)
    ex = [jnp.exp(v - vals[0]) for v in vals]
    tot = ex[0] + ex[1] + ex[2] + ex[3]
    idx_ref[...] = jnp.concatenate(ids, axis=0)
    gate_ref[...] = jnp.concatenate([e / tot for e in ex], axis=0)
    selm = hots[0].astype(F32) + hots[1].astype(F32) + hots[2].astype(F32) + hots[3].astype(F32)
    r = lax.broadcasted_iota(I32, (tm, tm), 0)
    c = lax.broadcasted_iota(I32, (tm, tm), 1)
    before = _dot(selm.astype(BF16), (r < c).astype(BF16)) + cnt_sc[:, 0:1]
    ranks = [jnp.sum(jnp.where(hot, before, 0.0), axis=0, keepdims=True) for hot in hots]
    rank_ref[...] = jnp.concatenate(ranks, axis=0).astype(I32)
    cnt_sc[...] = cnt_sc[...] + jnp.sum(selm, axis=1, keepdims=True)
    cnt_ref[...] = cnt_sc[...]


def _outproj(x, ya, yb, ycc, ycl, mod, mw, wo, n2, rw, rb, tile0, n_ctx_rows):
    nt = x.shape[0]
    tm = ROW_TILE
    nct = n_ctx_rows // tm
    n_tiles = nt // tm - tile0
    n_out = n_tiles * tm
    rows = ycl.shape[0] // tm
    ycl4 = ycl.reshape(GRID_W, rows, BATCH, S5_WIDTH)
    full = lambda s: pl.BlockSpec(s, lambda i: tuple(0 for _ in s))
    return pl.pallas_call(
        functools.partial(_outproj_kernel, tile0=tile0, n_ctx_tiles=nct),
        out_shape=(
            jax.ShapeDtypeStruct((n_out, D_MODEL), F32),
            jax.ShapeDtypeStruct((n_out, D_MODEL), F32),
            jax.ShapeDtypeStruct((TOP_K, n_out), I32),
            jax.ShapeDtypeStruct((TOP_K, n_out), F32),
            jax.ShapeDtypeStruct((TOP_K, n_out), I32),
            jax.ShapeDtypeStruct((N_EXPERTS, LANES), F32),
        ),
        grid=(n_tiles,),
        in_specs=[
            pl.BlockSpec((tm, D_MODEL), lambda i: (i + tile0, 0)),
            pl.BlockSpec((ML_HEADS, tm, LANES), lambda i: (0, i + tile0, 0)),
            pl.BlockSpec((tm, RG_WIDTH), lambda i: (i + tile0, 0)),
            pl.BlockSpec((tm, S5_WIDTH), lambda i: (jnp.minimum(i + tile0, nct - 1), 0)),
            pl.BlockSpec((GRID_W, 1, BATCH, S5_WIDTH), lambda i: (0, jnp.maximum(i + tile0 - nct, 0), 0, 0)),
            pl.BlockSpec((1, 3, BATCH, D_MODEL), lambda i: (jnp.where(i + tile0 < nct, 0, 1), 0, 0, 0)),
            full((1, MERGED)),
            full((MERGED, D_MODEL)),
            full((1, D_MODEL)),
            full((N_EXPERTS, D_MODEL)),
            full((N_EXPERTS, 1)),
        ],
        out_specs=(
            pl.BlockSpec((tm, D_MODEL), lambda i: (i, 0)),
            pl.BlockSpec((tm, D_MODEL), lambda i: (i, 0)),
            pl.BlockSpec((TOP_K, tm), lambda i: (0, i)),
            pl.BlockSpec((TOP_K, tm), lambda i: (0, i)),
            pl.BlockSpec((TOP_K, tm), lambda i: (0, i)),
            full((N_EXPERTS, LANES)),
        ),
        scratch_shapes=[pltpu.VMEM((N_EXPERTS, LANES), F32)],
        compiler_params=_cparams("arbitrary"),
        name="outproj_router",
    )(x, ya, yb, ycc, ycl4, mod, mw, wo, n2, rw, rb)


def _gather_kernel(tok_ref, h_hbm, out_ref, sem):
    bm = out_ref.shape[0]

    def issue(r, carry):
        t = tok_ref[0, 0, r]
        pltpu.make_async_copy(h_hbm.at[pl.ds(t, 1), :], out_ref.at[pl.ds(r, 1), :], sem).start()
        return carry

    lax.fori_loop(0, bm, issue, 0)

    def drain(r, carry):
        pltpu.make_async_copy(h_hbm.at[pl.ds(0, 1), :], out_ref.at[pl.ds(r, 1), :], sem).wait()
        return carry

    lax.fori_loop(0, bm, drain, 0)


def _moe_gather(pad_tok, h2, n_blocks):
    bm = MOE_BM
    return pl.pallas_call(
        _gather_kernel,
        out_shape=jax.ShapeDtypeStruct((n_blocks * bm, D_MODEL), F32),
        grid=(n_blocks,),
        in_specs=[
            pl.BlockSpec((1, 1, bm), lambda i: (i, 0, 0), memory_space=pltpu.SMEM),
            pl.BlockSpec(memory_space=pl.ANY),
        ],
        out_specs=pl.BlockSpec((bm, D_MODEL), lambda i: (i, 0)),
        scratch_shapes=[pltpu.SemaphoreType.DMA(())],
        compiler_params=_cparams("arbitrary"),
        name="moe_gather",
    )(pad_tok.reshape(n_blocks, 1, bm), h2)


def _ffn_kernel(be_ref, nu_ref, x_ref, wgu_ref, bgu_ref, wd_ref, bd_ref, y_ref, wgu_sc, wd_sc):
    i = pl.program_id(0)
    changed = jnp.logical_or(i == 0, be_ref[i] != be_ref[jnp.maximum(i - 1, 0)])

    @pl.when(changed)
    def _():
        wgu_sc[...] = wgu_ref[0].astype(BF16)
        wd_sc[...] = wd_ref[0].astype(BF16)

    @pl.when(i < nu_ref[0])
    def _():
        x = x_ref[...].astype(BF16)
        gu = _dot(x, wgu_sc[...]) + bgu_ref[0]
        gate = jnp.minimum(gu[:, :D_EXPERT], SWIGLU_LIMIT)
        up = jnp.clip(gu[:, D_EXPERT:], -SWIGLU_LIMIT, SWIGLU_LIMIT)
        glu = gate * _sigmoid(SWIGLU_ALPHA * gate)
        act = ((up + 1.0) * glu).astype(BF16)
        y_ref[...] = _dot(act, wd_sc[...]) + bd_ref[0]

    @pl.when(i >= nu_ref[0])
    def _():
        y_ref[...] = jnp.zeros(y_ref.shape, F32)


def _moe_ffn(block_e, n_used, xs, wgu, bgu, wd, bd):
    bm = MOE_BM
    n_blocks = xs.shape[0] // bm
    ne = wgu.shape[0]
    grid_spec = pltpu.PrefetchScalarGridSpec(
        num_scalar_prefetch=2,
        grid=(n_blocks,),
        in_specs=[
            pl.BlockSpec((bm, D_MODEL), lambda i, be, nu: (i, 0)),
            pl.BlockSpec((1, D_MODEL, 2 * D_EXPERT), lambda i, be, nu: (be[i], 0, 0)),
            pl.BlockSpec((1, 1, 2 * D_EXPERT), lambda i, be, nu: (be[i], 0, 0)),
            pl.BlockSpec((1, D_EXPERT, D_MODEL), lambda i, be, nu: (be[i], 0, 0)),
            pl.BlockSpec((1, 1, D_MODEL), lambda i, be, nu: (be[i], 0, 0)),
        ],
        out_specs=pl.BlockSpec((bm, D_MODEL), lambda i, be, nu: (i, 0)),
        scratch_shapes=[
            pltpu.VMEM((D_MODEL, 2 * D_EXPERT), BF16),
            pltpu.VMEM((D_EXPERT, D_MODEL), BF16),
        ],
    )
    return pl.pallas_call(
        _ffn_kernel,
        out_shape=jax.ShapeDtypeStruct((n_blocks * bm, D_MODEL), F32),
        grid_spec=grid_spec,
        compiler_params=_cparams("arbitrary"),
        name="moe_ffn",
    )(block_e, n_used, xs, wgu, bgu.reshape(ne, 1, -1), wd, bd.reshape(ne, 1, -1))


def _combine_kernel(pos_ref, x_ref, gate_ref, mod_ref, fw_ref, ys_hbm, out_ref, buf, sem, *, final_norm):
    tm = x_ref.shape[0]
    n = TOP_K * tm

    def issue(r, carry):
        p = pos_ref[0, 0, r]
        pltpu.make_async_copy(ys_hbm.at[pl.ds(p, 1), :], buf.at[pl.ds(r, 1), :], sem).start()
        return carry

    lax.fori_loop(0, n, issue, 0)

    def drain(r, carry):
        pltpu.make_async_copy(ys_hbm.at[pl.ds(0, 1), :], buf.at[pl.ds(r, 1), :], sem).wait()
        return carry

    lax.fori_loop(0, n, drain, 0)
    g = gate_ref[...]
    f = g[:, 0:1] * buf[0:tm, :]
    for k in range(1, TOP_K):
        f = f + g[:, k:k + 1] * buf[k * tm:(k + 1) * tm, :]
    x = x_ref[...].reshape(tm // BATCH, BATCH, D_MODEL) + mod_ref[0][None] * f.reshape(tm // BATCH, BATCH, D_MODEL)
    if final_norm:
        ms = jnp.mean(x * x, axis=-1, keepdims=True)
        x = (x * lax.rsqrt(ms + EPS)) * fw_ref[...][None]
    out_ref[...] = x.reshape(tm, D_MODEL)


def _moe_combine(pos, x, gates_t, mod, fw, ys, n_ctx_rows, final_norm):
    nt = x.shape[0]
    tm = COMBINE_TILE
    n_tiles = nt // tm
    nct = n_ctx_rows // tm
    posr = pos.reshape(TOP_K, n_tiles, tm).transpose(1, 0, 2).reshape(n_tiles, 1, TOP_K * tm)
    return pl.pallas_call(
        functools.partial(_combine_kernel, final_norm=final_norm),
        out_shape=jax.ShapeDtypeStruct((nt, D_MODEL), F32),
        grid=(n_tiles,),
        in_specs=[
            pl.BlockSpec((1, 1, TOP_K * tm), lambda i: (i, 0, 0), memory_space=pltpu.SMEM),
            pl.BlockSpec((tm, D_MODEL), lambda i: (i, 0)),
            pl.BlockSpec((tm, TOP_K), lambda i: (i, 0)),
            pl.BlockSpec((1, BATCH, D_MODEL), lambda i: (jnp.where(i < nct, 0, 1), 0, 0)),
            pl.BlockSpec((1, D_MODEL), lambda i: (0, 0)),
            pl.BlockSpec(memory_space=pl.ANY),
        ],
        out_specs=pl.BlockSpec((tm, D_MODEL), lambda i: (i, 0)),
        scratch_shapes=[pltpu.VMEM((TOP_K * tm, D_MODEL), F32), pltpu.SemaphoreType.DMA(())],
        compiler_params=_cparams("arbitrary"),
        name="moe_combine",
    )(posr, x, gates_t, mod, fw, ys)


def _moe(h2, idx, gates, rank, counts, x, g2, fw, wgu, bgu, wd, bd, n_ctx_rows, final_norm):
    nt = h2.shape[0]
    bm = MOE_BM
    m_total = nt * TOP_K
    n_blocks = m_total // bm + N_EXPERTS
    cnt = counts[:, 0].astype(I32)
    padded = (cnt + bm - 1) // bm * bm
    pad_ends = jnp.cumsum(padded)
    pad_starts = pad_ends - padded
    pos = pad_starts[idx] + rank
    tok = jnp.broadcast_to(jnp.arange(nt, dtype=I32)[None], (TOP_K, nt))
    pad_tok = jnp.zeros((n_blocks * bm,), I32).at[pos.reshape(-1)].set(tok.reshape(-1))
    block_e = jnp.minimum(jnp.searchsorted(pad_ends, jnp.arange(n_blocks, dtype=I32) * bm, side="right"),
                          N_EXPERTS - 1).astype(I32)
    n_used = (pad_ends[-1] // bm).astype(I32).reshape(1)
    xs = _moe_gather(pad_tok, h2, n_blocks)
    ys = _moe_ffn(block_e, n_used, xs, wgu, bgu, wd, bd)
    return _moe_combine(pos, x, gates.T, g2, fw, ys, n_ctx_rows, final_norm)


def _pad_heads(a):
    lead = a.shape[:-1]
    a = a.reshape(*lead, 4 * ML_HEADS, ML_HEAD_DIM)
    a = jnp.pad(a, [(0, 0)] * len(lead) + [(0, 0), (0, LANES - ML_HEAD_DIM)])
    return a.reshape(*lead, 4 * ML_HEADS * LANES)


def _block_diag(blocks):
    return jax.scipy.linalg.block_diag(*[blocks[g] for g in range(blocks.shape[0])])


def _s5_discretise(lam_re, lam_im, log_dt, b_re, b_im):
    dt = jnp.exp(log_dt)[:, None]
    mag = jnp.exp(lam_re * dt)
    ar, ai = mag * jnp.cos(lam_im * dt), mag * jnp.sin(lam_im * dt)
    den = lam_re * lam_re + lam_im * lam_im
    cr = ((ar - 1.0) * lam_re + ai * lam_im) / den
    ci = (ai * lam_re - (ar - 1.0) * lam_im) / den
    bbr = cr[..., None] * b_re - ci[..., None] * b_im
    bbi = cr[..., None] * b_im + ci[..., None] * b_re
    return ar, ai, bbr, bbi


def kernel(x, c, ctx, c_ctx, ada_w, ada_b, norm1_w, w_in, b_in, rg_conv_w, rg_conv_b, rg_wa, rg_ba, rg_wx, rg_bx, rg_lambda, s5_lambda_re, s5_lambda_im, s5_log_dt, s5_b_re, s5_b_im, s5_c_re, s5_c_im, s5_d, s5_glu_w, s5_glu_b, mix_norm_w, w_out, norm2_w, router_w, router_b, moe_w_gate_up, moe_b_gate_up, moe_w_down, moe_b_down, final_norm_w):
    bsz, seq, dm = x.shape
    sc = ctx.shape[1]
    assert bsz == BATCH and dm == D_MODEL and seq % GRID_W == 0
    n_ctx_rows = sc * BATCH
    n_lat_rows = seq * BATCH
    assert n_ctx_rows % CHUNK_ROWS == 0 and n_lat_rows % CHUNK_ROWS == 0 and GRID_W * BATCH == ROW_TILE
    n_ctx_chunks = n_ctx_rows // CHUNK_ROWS
    depth = ada_w.shape[0]

    xs = jnp.concatenate([ctx.transpose(1, 0, 2).reshape(n_ctx_rows, dm),
                          x.transpose(1, 0, 2).reshape(n_lat_rows, dm)], axis=0)

    c_rows = jnp.concatenate([c, c_ctx[None], jnp.zeros((16 - bsz - 1, dm), F32)], axis=0)
    mods = _modulation(c_rows, ada_w, ada_b)
    mods = mods.reshape(depth, 16, 6, dm)
    mod_lat = mods[:, :bsz].transpose(0, 2, 1, 3)
    mod_ctx = jnp.broadcast_to(mods[:, bsz][:, :, None, :], mod_lat.shape)
    mod = jnp.stack([mod_ctx, mod_lat], axis=1)

    for l in range(depth):
        with_ctx = l < depth - 1
        wa_cols = w_in[l][:, :4 * ML_HEADS * ML_HEAD_DIM]
        ba_cols = b_in[l][:4 * ML_HEADS * ML_HEAD_DIM]
        kscale = jnp.ones((4, ML_HEADS * LANES), F32).at[1].set(ML_HEAD_DIM ** -0.5).reshape(-1)
        ones_col = jnp.zeros((4, ML_HEADS, LANES), F32).at[2, :, ML_HEAD_DIM].set(1.0).reshape(-1)
        g0 = 4 * ML_HEADS * ML_HEAD_DIM
        ng = 4 * ML_HEADS
        w_cat = jnp.concatenate([
            _pad_heads(wa_cols) * kscale,
            jnp.pad(w_in[l][:, g0:g0 + ng], ((0, 0), (0, LANES - ng))),
            w_in[l][:, g0 + ng:]], axis=1).astype(BF16)
        b_cat = jnp.concatenate([
            _pad_heads(ba_cols) * kscale + ones_col,
            jnp.pad(b_in[l][g0:g0 + ng], (0, LANES - ng)),
            b_in[l][g0 + ng:]])[None]
        wg = [jnp.concatenate([_block_diag(rg_wa[l, d]), _block_diag(rg_wx[l, d])], axis=1).astype(BF16)
              for d in range(2)]
        bg = [jnp.concatenate([rg_ba[l, d], rg_bx[l, d]])[None] for d in range(2)]
        s5p = []
        for d in range(2):
            ar, ai, bbr, bbi = _s5_discretise(s5_lambda_re[l, d], s5_lambda_im[l, d], s5_log_dt[l, d],
                                              s5_b_re[l], s5_b_im[l])
            a = jnp.stack([ar.reshape(-1), ai.reshape(-1)])
            bd = jnp.concatenate([_block_diag(bbr.transpose(0, 2, 1)), _block_diag(bbi.transpose(0, 2, 1))],
                                 axis=1).astype(BF16)
            cd = jnp.concatenate([_block_diag(s5_c_re[l].transpose(0, 2, 1)),
                                  -_block_diag(s5_c_im[l].transpose(0, 2, 1))], axis=0).astype(BF16)
            s5p.append((a, bd, cd))
        mw = jnp.concatenate([_pad_heads(jnp.tile(mix_norm_w[l][:ML_HEADS * ML_HEAD_DIM], 4))[:ML_HEADS * LANES],
                              mix_norm_w[l][ML_HEADS * ML_HEAD_DIM:]])[None]
        wo_a = w_out[l][:ML_HEADS * ML_HEAD_DIM].reshape(ML_HEADS, ML_HEAD_DIM, dm)
        wo_a = jnp.pad(wo_a, ((0, 0), (0, LANES - ML_HEAD_DIM), (0, 0))).reshape(ML_HEADS * LANES, dm)
        wo = jnp.concatenate([wo_a, w_out[l][ML_HEADS * ML_HEAD_DIM:]], axis=0).astype(BF16)

        pa, pb, pcc, pcl = _inproj(xs, mod[l, :, 0:2], norm1_w[l][None], w_cat, b_cat, n_ctx_rows)
        pcl = pcl.reshape(n_lat_rows, S5_WIDTH)
        hb = _mlstm_pass(pa, None, 1, n_ctx_chunks)
        ya = _mlstm_pass(pa, hb, 0, n_ctx_chunks)
        rgb = _rglru_pass(pb, None, rg_conv_w[l], rg_conv_b[l][None], wg[1], bg[1], rg_lambda[l, 1][None], 1, n_ctx_chunks)
        yb = _rglru_pass(pb, rgb, rg_conv_w[l], rg_conv_b[l][None], wg[0], bg[0], rg_lambda[l, 0][None], 0, n_ctx_chunks)
        zero_state = jnp.zeros((BATCH, 2 * S5_NSTATE), F32)
        a1, bd1, cd1 = s5p[1]
        a0, bd0, cd0 = s5p[0]
        dsk, gw, gb = s5_d[l][None], s5_glu_w[l].astype(BF16), s5_glu_b[l][None]
        ycb_c, st = _s5_pass(pcc, None, zero_state, a1, bd1, cd1, None, None, None, 1)
        ycb_l, _ = _s5_pass(pcl, None, st, a1, bd1, cd1, None, None, None, 1)
        yc_c, st = _s5_pass(pcc, ycb_c, zero_state, a0, bd0, cd0, dsk, gw, gb, 0)
        yc_l, _ = _s5_pass(pcl, ycb_l, st, a0, bd0, cd0, dsk, gw, gb, 0)

        tile0 = 0 if with_ctx else n_ctx_rows // ROW_TILE
        xo, h2, idx, gates, rank, counts = _outproj(
            xs, ya, yb, yc_c, yc_l, mod[l, :, 2:5], mw, wo, norm2_w[l][None], router_w[l].T,
            router_b[l][:, None], tile0, n_ctx_rows)

        xs_new = _moe(h2, idx, gates, rank, counts, xo, mod[l, :, 5], final_norm_w[None],
                      moe_w_gate_up[l], moe_b_gate_up[l], moe_w_down[l], moe_b_down[l],
                      n_ctx_rows if with_ctx else 0, l == depth - 1)
        xs = xs_new

    out = xs.reshape(seq, bsz, dm).transpose(1, 0, 2)
    return out
```

```python
import functools
import math

import jax
import jax.numpy as jnp
from jax import lax
from jax.experimental import pallas as pl
from jax.experimental.pallas import tpu as pltpu

F32 = jnp.float32
BF16 = jnp.bfloat16
I32 = jnp.int32

D_MODEL = 1024
BATCH = 8
DEPTH = 4
GRID_W = 64
ML_HEADS = 4
ML_HEAD_DIM = 96
ML_CHUNK = 128
ML_M_INIT = -1e30
RG_BLOCKS = 6
RG_BLOCK_DIM = 64
RG_WIDTH = RG_BLOCKS * RG_BLOCK_DIM
RG_CONV = 4
RG_C = 8.0
S5_GROUPS = 16
S5_GROUP = 16
S5_WIDTH = S5_GROUPS * S5_GROUP
S5_STATE = 64
S5_NSTATE = S5_GROUPS * S5_STATE
N_EXPERTS = 32
TOP_K = 4
D_EXPERT = D_MODEL
SWIGLU_LIMIT = 7.0
SWIGLU_ALPHA = 1.702
EPS = 1e-6

LANES = 128
SUBLANES = 8

ML_SLABS = 4 * ML_HEADS + 1
ML_GATE_SLAB = 4 * ML_HEADS
A_PAD = ML_SLABS * LANES
MERGED = ML_HEADS * LANES + RG_WIDTH + S5_WIDTH

ROW_TILE = 512
T_CHUNK = 128
CHUNK_ROWS = T_CHUNK * BATCH
MOE_BM = 256
DISPATCH_TILE = 256
COMBINE_TILE = 256
VMEM_LIMIT = 56 * 1024 * 1024


def _cparams(*sem):
    return pltpu.CompilerParams(dimension_semantics=sem, vmem_limit_bytes=VMEM_LIMIT)


def _sigmoid(x):
    return 1.0 / (1.0 + jnp.exp(-x))


def _log_sigmoid(x):
    return jnp.minimum(x, 0.0) - jnp.log1p(jnp.exp(-jnp.abs(x)))


def _softplus(x):
    return jnp.maximum(x, 0.0) + jnp.log1p(jnp.exp(-jnp.abs(x)))


def _gelu(x):
    return 0.5 * x * (1.0 + jnp.tanh(0.7978845608028654 * (x + 0.044715 * (x * x * x))))


def _dot(a, b):
    return jnp.dot(a, b, preferred_element_type=F32)


def _dot_nt(a, b):
    return lax.dot_general(a, b, (((1,), (1,)), ((), ())), preferred_element_type=F32)


def _split(a):
    hi = a.astype(BF16)
    lo = (a - hi.astype(F32)).astype(BF16)
    return hi, lo


def _dot_lx(a_exact, b):
    hi, lo = _split(b)
    return _dot(a_exact, hi) + _dot(a_exact, lo)


def _dot_xr(a, b_exact):
    hi, lo = _split(a)
    return _dot(hi, b_exact) + _dot(lo, b_exact)


def _dot3(a, b, nt=False):
    ah, al = _split(a)
    bh, bl = _split(b)
    d = _dot_nt if nt else _dot
    return d(ah, bh) + (d(ah, bl) + d(al, bh))


def _group_ones(n, shift):
    r = lax.broadcasted_iota(I32, (n, n), 0)
    c = lax.broadcasted_iota(I32, (n, n), 1)
    return (lax.shift_right_logical(r, shift) == lax.shift_right_logical(c, shift)).astype(BF16)


def _store_token_tiles(ref, tok0, val):
    n = val.shape[0]
    for s in range(SUBLANES):
        ref[pl.ds(tok0 * SUBLANES + s, n, stride=SUBLANES), :] = val[:, s * LANES:(s + 1) * LANES]


def _load_token_tiles(ref, tok0, n):
    return jnp.concatenate(
        [ref[pl.ds(tok0 * SUBLANES + s, n, stride=SUBLANES), :] for s in range(SUBLANES)], axis=1)


def _mod_kernel(c_ref, w_ref, b_ref, o_ref):
    c = c_ref[...]
    s = c * _sigmoid(c)
    o_ref[0] = _dot3(s, w_ref[0]) + b_ref[0]


def _modulation(c_rows, ada_w, ada_b):
    depth, d, n = ada_w.shape
    tn = 1536
    return pl.pallas_call(
        _mod_kernel,
        out_shape=jax.ShapeDtypeStruct((depth, 16, n), F32),
        grid=(depth, n // tn),
        in_specs=[
            pl.BlockSpec((16, d), lambda l, j: (0, 0)),
            pl.BlockSpec((1, d, tn), lambda l, j: (l, 0, j)),
            pl.BlockSpec((1, 1, tn), lambda l, j: (l, 0, j)),
        ],
        out_specs=pl.BlockSpec((1, 16, tn), lambda l, j: (l, 0, j)),
        compiler_params=_cparams("arbitrary", "arbitrary"),
        name="adaln_mod",
    )(c_rows, ada_w, ada_b.reshape(depth, 1, n))


def _inproj_kernel(x_ref, mod_ref, nw_ref, w_ref, b_ref, pa_ref, pb_ref, pcc_ref, pcl_ref, *, n_ctx_tiles):
    i = pl.program_id(0)
    x = x_ref[...]
    tm = x.shape[0]
    ms = jnp.mean(x * x, axis=-1, keepdims=True)
    xn = (x * lax.rsqrt(ms + EPS)) * nw_ref[...]
    xn = xn.reshape(tm // BATCH, BATCH, D_MODEL)
    h = xn * (1.0 + mod_ref[0, 1][None]) + mod_ref[0, 0][None]
    h = h.reshape(tm, D_MODEL).astype(BF16)
    for j in range(0, ML_SLABS - 1, 2):
        p = _dot(h, w_ref[:, j * LANES:(j + 2) * LANES]) + b_ref[:, j * LANES:(j + 2) * LANES]
        pa_ref[j] = p[:, :LANES]
        pa_ref[j + 1] = p[:, LANES:]
    c0 = ML_GATE_SLAB * LANES
    p = _dot(h, w_ref[:, c0:c0 + LANES]) + b_ref[:, c0:c0 + LANES]
    pa_ref[ML_GATE_SLAB] = p
    c0 = A_PAD
    pb_ref[...] = _dot(h, w_ref[:, c0:c0 + 2 * RG_WIDTH]) + b_ref[:, c0:c0 + 2 * RG_WIDTH]
    c0 = A_PAD + 2 * RG_WIDTH
    pc = _dot(h, w_ref[:, c0:c0 + S5_WIDTH]) + b_ref[:, c0:c0 + S5_WIDTH]

    @pl.when(i < n_ctx_tiles)
    def _():
        pcc_ref[...] = pc

    @pl.when(i >= n_ctx_tiles)
    def _():
        pcl_ref[...] = pc.reshape(GRID_W, 1, BATCH, S5_WIDTH)


def _inproj(x, mod, nw, w, b, n_ctx_rows):
    nt = x.shape[0]
    tm = ROW_TILE
    nct = n_ctx_rows // tm
    n_lat_rows = nt - n_ctx_rows
    rows = n_lat_rows // tm
    ncols = w.shape[1]
    return pl.pallas_call(
        functools.partial(_inproj_kernel, n_ctx_tiles=nct),
        out_shape=(
            jax.ShapeDtypeStruct((ML_SLABS, nt, LANES), F32),
            jax.ShapeDtypeStruct((nt, 2 * RG_WIDTH), F32),
            jax.ShapeDtypeStruct((n_ctx_rows, S5_WIDTH), F32),
            jax.ShapeDtypeStruct((GRID_W, rows, BATCH, S5_WIDTH), F32),
        ),
        grid=(nt // tm,),
        in_specs=[
            pl.BlockSpec((tm, D_MODEL), lambda i: (i, 0)),
            pl.BlockSpec((1, 2, BATCH, D_MODEL), lambda i: (jnp.where(i < nct, 0, 1), 0, 0, 0)),
            pl.BlockSpec((1, D_MODEL), lambda i: (0, 0)),
            pl.BlockSpec((D_MODEL, ncols), lambda i: (0, 0)),
            pl.BlockSpec((1, ncols), lambda i: (0, 0)),
        ],
        out_specs=(
            pl.BlockSpec((ML_SLABS, tm, LANES), lambda i: (0, i, 0)),
            pl.BlockSpec((tm, 2 * RG_WIDTH), lambda i: (i, 0)),
            pl.BlockSpec((tm, S5_WIDTH), lambda i: (jnp.minimum(i, nct - 1), 0)),
            pl.BlockSpec((GRID_W, 1, BATCH, S5_WIDTH), lambda i: (0, jnp.maximum(i - nct, 0), 0, 0)),
        ),
        compiler_params=_cparams("arbitrary"),
        name="inproj",
    )(x, mod, nw, w, b)


def _mlstm_kernel(*refs, direction, final, n_chunks):
    if final:
        pa_ref, hb_ref, out_ref, c_sc, m_sc = refs
    else:
        pa_ref, out_ref, c_sc, m_sc = refs
        hb_ref = None
    i = pl.program_id(0)
    L = ML_CHUNK

    @pl.when(i == 0)
    def _():
        c_sc[...] = jnp.zeros(c_sc.shape, F32)
        m_sc[...] = jnp.full(m_sc.shape, ML_M_INIT, F32)

    row = lax.broadcasted_iota(I32, (L, L), 0)
    col = lax.broadcasted_iota(I32, (L, L), 1)
    causal = (row >= col) if direction == 0 else (row <= col)
    tri = causal.astype(BF16)
    tri_t = ((row <= col) if direction == 0 else (row >= col)).astype(BF16)
    ones = jnp.ones((L, L), BF16)
    head_lane = col < ML_HEAD_DIM

    def per_batch(b, carry):
        def ld(ref, j):
            return ref[j, pl.ds(b, L, stride=BATCH), :]

        gates = ld(pa_ref, ML_GATE_SLAB)
        lf = _log_sigmoid(gates)
        bcol = _dot_lx(tri, lf)
        btot = _dot_lx(ones, lf)
        gates_t = gates.T
        brow = _dot_xr(lf.T, tri_t)
        for h in range(ML_HEADS):
            ig_c = ML_HEADS * direction + h
            fg_c = 2 * ML_HEADS + ML_HEADS * direction + h
            q = ld(pa_ref, h).astype(BF16)
            k = ld(pa_ref, ML_HEADS + h).astype(BF16)
            v = ld(pa_ref, 2 * ML_HEADS + h)
            bc = bcol[:, fg_c:fg_c + 1]
            bt = btot[:, fg_c:fg_c + 1]
            igc = gates[:, ig_c:ig_c + 1]
            r = gates_t[ig_c:ig_c + 1, :] - brow[fg_c:fg_c + 1, :]
            dm = jnp.where(causal, bc + r, -jnp.inf)
            idx = b * ML_HEADS + h
            m0 = m_sc[idx][:, 0:1]
            c0 = c_sc[idx]
            g = bc + m0
            mt = jnp.maximum(jnp.max(dm, axis=1, keepdims=True), g)
            pmat = jnp.exp(dm - mt)
            sqk = _dot_nt(q, k) * pmat
            inter = jnp.exp(g - mt)
            nc = _dot_nt(q, c0.astype(BF16))
            num = _dot(sqk.astype(BF16), v.astype(BF16)) + inter * nc
            den = jnp.sum(sqk, axis=1, keepdims=True) + inter * nc[:, ML_HEAD_DIM:ML_HEAD_DIM + 1]
            hh = num / jnp.maximum(jnp.abs(den), jnp.exp(-mt))
            hh = jnp.where(head_lane, hh, 0.0)
            wcol = bt - bc + igc
            mloc = jnp.max(wcol, axis=0, keepdims=True)
            ecol = jnp.exp(wcol - mloc)
            cloc = _dot((v * ecol).T.astype(BF16), k)
            mnew = jnp.maximum(bt + m0, mloc)
            a = jnp.exp(bt + m0 - mnew)
            sc = jnp.exp(mloc - mnew)
            c_sc[idx] = a * c0 + sc * cloc
            m_sc[idx] = jnp.broadcast_to(mnew, (L, L))
            if final:
                ht = hh + ld(hb_ref, h)
                ms = jnp.sum(ht * ht, axis=1, keepdims=True) * (1.0 / ML_HEAD_DIM)
                o = ld(pa_ref, 3 * ML_HEADS + h)
                hh = _sigmoid(o) * (ht * lax.rsqrt(ms + EPS))
            out_ref[h, pl.ds(b, L, stride=BATCH), :] = hh
        return carry

    lax.fori_loop(0, BATCH, per_batch, 0)


def _chunk_order(i, direction, n_ctx, n_all):
    if direction == 0:
        return i
    return jnp.where(i < n_ctx, n_ctx - 1 - i, n_all - 1 + n_ctx - i)


def _mlstm_pass(pa, hb, direction, n_ctx):
    nt = pa.shape[1]
    n_all = nt // CHUNK_ROWS
    final = hb is not None
    cmap = lambda i: (0, _chunk_order(i, direction, n_ctx, n_all), 0)
    in_specs = [pl.BlockSpec((ML_SLABS, CHUNK_ROWS, LANES), cmap)]
    args = [pa]
    if final:
        in_specs.append(pl.BlockSpec((ML_HEADS, CHUNK_ROWS, LANES), cmap))
        args.append(hb)
    return pl.pallas_call(
        functools.partial(_mlstm_kernel, direction=direction, final=final, n_chunks=n_all),
        out_shape=jax.ShapeDtypeStruct((ML_HEADS, nt, LANES), F32),
        grid=(n_all,),
        in_specs=in_specs,
        out_specs=pl.BlockSpec((ML_HEADS, CHUNK_ROWS, LANES), cmap),
        scratch_shapes=[
            pltpu.VMEM((BATCH * ML_HEADS, ML_CHUNK, LANES), F32),
            pltpu.VMEM((BATCH * ML_HEADS, ML_CHUNK, LANES), F32),
        ],
        compiler_params=_cparams("arbitrary"),
        name="mlstm_fwd" if final else "mlstm_bwd",
    )(*args)


def _rglru_kernel(*refs, direction, final, n_ctx, n_all):
    if final:
        cur_ref, prev_ref, next_ref, hb_ref, cw_ref, cb_ref, wg_ref, bg_ref, lam_ref, out_ref, a_sc, b_sc, h_sc = refs
    else:
        cur_ref, prev_ref, next_ref, cw_ref, cb_ref, wg_ref, bg_ref, lam_ref, out_ref, a_sc, b_sc, h_sc = refs
        hb_ref = None
    i = pl.program_id(0)
    c = _chunk_order(i, direction, n_ctx, n_all)
    w = RG_WIDTH

    @pl.when(i == 0)
    def _():
        h_sc[...] = jnp.zeros(h_sc.shape, F32)

    seg_first = jnp.logical_or(c == 0, c == n_ctx)
    seg_last = jnp.logical_or(c == n_ctx - 1, c == n_all - 1)
    cur = cur_ref[:, :w]
    prev = jnp.where(seg_first, 0.0, prev_ref[:, :w])
    nxt = jnp.where(seg_last, 0.0, next_ref[:, :w])
    xc = jnp.concatenate([prev, cur, nxt], axis=0)
    n = CHUNK_ROWS
    xconv = cb_ref[...] + cw_ref[0:1, :] * xc[0:n]
    for j in range(1, RG_CONV):
        xconv = xconv + cw_ref[j:j + 1, :] * xc[j * BATCH:j * BATCH + n]
    z = _dot(xconv.astype(BF16), wg_ref[...]) + bg_ref[...]
    r = _sigmoid(z[:, :w])
    ig = _sigmoid(z[:, w:])
    log_a = (-RG_C * _softplus(-lam_ref[...])) * r
    a = jnp.exp(log_a)
    a_sc[...] = a
    b_sc[...] = jnp.sqrt(-jnp.tanh(log_a) * (a * a + 1.0)) * (ig * xconv)

    def step(t, h):
        tt = t if direction == 0 else T_CHUNK - 1 - t
        rows = pl.ds(pl.multiple_of(tt * BATCH, BATCH), BATCH)
        h = a_sc[rows, :] * h + b_sc[rows, :]
        b_sc[rows, :] = h
        return h

    h_sc[...] = lax.fori_loop(0, T_CHUNK, step, h_sc[...], unroll=8)
    hs = b_sc[...]
    if final:
        ht = hs + hb_ref[...]
        ss = _dot_xr(ht * ht, _group_ones(w, 6)) * (1.0 / RG_BLOCK_DIM)
        hs = _gelu(cur_ref[:, w:]) * (ht * lax.rsqrt(ss + EPS))
    out_ref[...] = hs


def _rglru_pass(pb, hb, cw, cb, wg, bg, lam, direction, n_ctx):
    nt = pb.shape[0]
    n_all = nt // CHUNK_ROWS
    final = hb is not None
    w2 = 2 * RG_WIDTH
    order = lambda i: _chunk_order(i, direction, n_ctx, n_all)
    halo_p = CHUNK_ROWS // (2 * BATCH)
    halo_n = CHUNK_ROWS // BATCH
    in_specs = [
        pl.BlockSpec((CHUNK_ROWS, w2), lambda i: (order(i), 0)),
        pl.BlockSpec((2 * BATCH, w2), lambda i: (jnp.maximum(order(i) * halo_p - 1, 0), 0)),
        pl.BlockSpec((BATCH, w2), lambda i: (jnp.minimum((order(i) + 1) * halo_n, nt // BATCH - 1), 0)),
    ]
    args = [pb, pb, pb]
    if final:
        in_specs.append(pl.BlockSpec((CHUNK_ROWS, RG_WIDTH), lambda i: (order(i), 0)))
        args.append(hb)
    full = lambda s: pl.BlockSpec(s, lambda i: tuple(0 for _ in s))
    in_specs += [full((RG_CONV, RG_WIDTH)), full((1, RG_WIDTH)), full((RG_WIDTH, w2)), full((1, w2)), full((1, RG_WIDTH))]
    args += [cw, cb, wg, bg, lam]
    return pl.pallas_call(
        functools.partial(_rglru_kernel, direction=direction, final=final, n_ctx=n_ctx, n_all=n_all),
        out_shape=jax.ShapeDtypeStruct((nt, RG_WIDTH), F32),
        grid=(n_all,),
        in_specs=in_specs,
        out_specs=pl.BlockSpec((CHUNK_ROWS, RG_WIDTH), lambda i: (order(i), 0)),
        scratch_shapes=[
            pltpu.VMEM((CHUNK_ROWS, RG_WIDTH), F32),
            pltpu.VMEM((CHUNK_ROWS, RG_WIDTH), F32),
            pltpu.VMEM((BATCH, RG_WIDTH), F32),
        ],
        compiler_params=_cparams("arbitrary"),
        name="rglru_fwd" if final else "rglru_bwd",
    )(*args)


def _s5_kernel(*refs, direction, final):
    if final:
        (u_ref, yb_ref, h0_ref, a_ref, bd_ref, cd_ref, dsk_ref, gw_ref, gb_ref,
         out_ref, hN_ref, v_sc, h_sc) = refs
    else:
        u_ref, h0_ref, a_ref, bd_ref, cd_ref, out_ref, hN_ref, v_sc, h_sc = refs
    i = pl.program_id(0)
    n = S5_NSTATE

    @pl.when(i == 0)
    def _():
        h_sc[...] = h0_ref[...]

    u = u_ref[...]
    v_sc[...] = _dot(u.astype(BF16), bd_ref[...])
    ar = jnp.broadcast_to(a_ref[0:1, :], (BATCH, n))
    ai = jnp.broadcast_to(a_ref[1:2, :], (BATCH, n))

    def step(t, carry):
        hr, hi = carry
        tt = t if direction == 0 else T_CHUNK - 1 - t
        rows = pl.ds(pl.multiple_of(tt * BATCH, BATCH), BATCH)
        nr = ar * hr - ai * hi + v_sc[rows, :n]
        ni = ar * hi + ai * hr + v_sc[rows, n:]
        v_sc[rows, :n] = nr
        v_sc[rows, n:] = ni
        return nr, ni

    hr, hi = lax.fori_loop(0, T_CHUNK, step, (h_sc[:, :n], h_sc[:, n:]))
    h_sc[:, :n] = hr
    h_sc[:, n:] = hi
    hN_ref[...] = h_sc[...]
    y = _dot(v_sc[...].astype(BF16), cd_ref[...])
    if final:
        y = y + yb_ref[...] + dsk_ref[...] * u
        g = _gelu(y)
        o = g * _sigmoid(_dot(g.astype(BF16), gw_ref[...]) + gb_ref[...])
        ss = _dot_xr(o * o, _group_ones(S5_WIDTH, 4)) * (1.0 / S5_GROUP)
        y = o * lax.rsqrt(ss + EPS)
    out_ref[...] = y


def _s5_pass(u, yb, h0, a, bd, cd, dsk, gw, gb, direction):
    nt = u.shape[0]
    n_chunks = nt // CHUNK_ROWS
    final = yb is not None
    order = (lambda i: i) if direction == 0 else (lambda i: n_chunks - 1 - i)
    full = lambda s: pl.BlockSpec(s, lambda i: tuple(0 for _ in s))
    chunk = pl.BlockSpec((CHUNK_ROWS, S5_WIDTH), lambda i: (order(i), 0))
    in_specs = [chunk]
    args = [u]
    if final:
        in_specs.append(chunk)
        args.append(yb)
    in_specs += [full((BATCH, 2 * S5_NSTATE)), full((2, S5_NSTATE)), full((S5_WIDTH, 2 * S5_NSTATE)),
                 full((2 * S5_NSTATE, S5_WIDTH))]
    args += [h0, a, bd, cd]
    if final:
        in_specs += [full((1, S5_WIDTH)), full((S5_WIDTH, S5_WIDTH)), full((1, S5_WIDTH))]
        args += [dsk, gw, gb]
    return pl.pallas_call(
        functools.partial(_s5_kernel, direction=direction, final=final),
        out_shape=(jax.ShapeDtypeStruct((nt, S5_WIDTH), F32), jax.ShapeDtypeStruct((BATCH, 2 * S5_NSTATE), F32)),
        grid=(n_chunks,),
        in_specs=in_specs,
        out_specs=(chunk, full((BATCH, 2 * S5_NSTATE))),
        scratch_shapes=[
            pltpu.VMEM((CHUNK_ROWS, 2 * S5_NSTATE), F32),
            pltpu.VMEM((BATCH, 2 * S5_NSTATE), F32),
        ],
        compiler_params=_cparams("arbitrary"),
        name="s5_fwd" if final else "s5_bwd",
    )(*args)


def _outproj_kernel(x_ref, ya_ref, yb_ref, ycc_ref, ycl_ref, mod_ref, mw_ref, wo_ref, n2_ref, rw_ref, rb_ref,
                    xo_ref, h2_ref, idx_ref, gate_ref, rank_ref, cnt_ref, cnt_sc, *, tile0, n_ctx_tiles):
    i = pl.program_id(0)
    tile = i + tile0
    tm = x_ref.shape[0]

    @pl.when(i == 0)
    def _():
        cnt_sc[...] = jnp.zeros(cnt_sc.shape, F32)

    yc = jnp.where(tile < n_ctx_tiles, ycc_ref[...], ycl_ref[...].reshape(tm, S5_WIDTH))
    merged = jnp.concatenate([ya_ref[h] for h in range(ML_HEADS)] + [yb_ref[...], yc], axis=1)
    merged = (merged * mw_ref[...]).astype(BF16)
    proj = _dot(merged, wo_ref[...])
    g1 = mod_ref[0, 0]
    x = x_ref[...].reshape(tm // BATCH, BATCH, D_MODEL) + g1[None] * proj.reshape(tm // BATCH, BATCH, D_MODEL)
    xo_ref[...] = x.reshape(tm, D_MODEL)
    ms = jnp.mean(x * x, axis=-1, keepdims=True)
    xn = (x * lax.rsqrt(ms + EPS)) * n2_ref[...][None]
    h2 = (xn * (1.0 + mod_ref[0, 2][None]) + mod_ref[0, 1][None]).reshape(tm, D_MODEL)
    _store_token_tiles(h2_ref, 0, h2)
    logits = _dot3(rw_ref[...], h2, nt=True) + rb_ref[...]
    eidx = lax.broadcasted_iota(I32, (N_EXPERTS, tm), 0)
    vals, ids, hots = [], [], []
    for _ in range(TOP_K):
        m = jnp.max(logits, axis=0, keepdims=True)
        sel = jnp.min(jnp.where(logits == m, eidx, N_EXPERTS), axis=0, keepdims=True)
        hot = eidx == sel
        logits = jnp.where(hot, -jnp.inf, logits)
        vals.append(m)
        ids.append(sel)
        hots.append(hot)
    ex = [jnp.exp(v - vals[0]) for v in vals]
    tot = ex[0] + ex[1] + ex[2] + ex[3]
    idx_ref[...] = jnp.concatenate(ids, axis=0)
    gate_ref[...] = jnp.concatenate([e / tot for e in ex], axis=0)
    selm = hots[0].astype(F32) + hots[1].astype(F32) + hots[2].astype(F32) + hots[3].astype(F32)
    r = lax.broadcasted_iota(I32, (tm, tm), 0)
    c = lax.broadcasted_iota(I32, (tm, tm), 1)
    before = _dot(selm.astype(BF16), (r < c).astype(BF16)) + cnt_sc[:, 0:1]
    ranks = [jnp.sum(jnp.where(hot, before, 0.0), axis=0, keepdims=True) for hot in hots]
    rank_ref[...] = jnp.concatenate(ranks, axis=0).astype(I32)
    cnt_sc[...] = cnt_sc[...] + jnp.sum(selm, axis=1, keepdims=True)
    cnt_ref[...] = cnt_sc[...]


def _outproj(x, ya, yb, ycc, ycl, mod, mw, wo, n2, rw, rb, tile0, n_ctx_rows):
    nt = x.shape[0]
    tm = ROW_TILE
    nct = n_ctx_rows // tm
    n_tiles = nt // tm - tile0
    n_out = n_tiles * tm
    rows = ycl.shape[0] // tm
    ycl4 = ycl.reshape(GRID_W, rows, BATCH, S5_WIDTH)
    full = lambda s: pl.BlockSpec(s, lambda i: tuple(0 for _ in s))
    return pl.pallas_call(
        functools.partial(_outproj_kernel, tile0=tile0, n_ctx_tiles=nct),
        out_shape=(
            jax.ShapeDtypeStruct((n_out, D_MODEL), F32),
            jax.ShapeDtypeStruct((n_out * SUBLANES, LANES), F32),
            jax.ShapeDtypeStruct((TOP_K, n_out), I32),
            jax.ShapeDtypeStruct((TOP_K, n_out), F32),
            jax.ShapeDtypeStruct((TOP_K, n_out), I32),
            jax.ShapeDtypeStruct((N_EXPERTS, LANES), F32),
        ),
        grid=(n_tiles,),
        in_specs=[
            pl.BlockSpec((tm, D_MODEL), lambda i: (i + tile0, 0)),
            pl.BlockSpec((ML_HEADS, tm, LANES), lambda i: (0, i + tile0, 0)),
            pl.BlockSpec((tm, RG_WIDTH), lambda i: (i + tile0, 0)),
            pl.BlockSpec((tm, S5_WIDTH), lambda i: (jnp.minimum(i + tile0, nct - 1), 0)),
            pl.BlockSpec((GRID_W, 1, BATCH, S5_WIDTH), lambda i: (0, jnp.maximum(i + tile0 - nct, 0), 0, 0)),
            pl.BlockSpec((1, 3, BATCH, D_MODEL), lambda i: (jnp.where(i + tile0 < nct, 0, 1), 0, 0, 0)),
            full((1, MERGED)),
            full((MERGED, D_MODEL)),
            full((1, D_MODEL)),
            full((N_EXPERTS, D_MODEL)),
            full((N_EXPERTS, 1)),
        ],
        out_specs=(
            pl.BlockSpec((tm, D_MODEL), lambda i: (i, 0)),
            pl.BlockSpec((tm * SUBLANES, LANES), lambda i: (i, 0)),
            pl.BlockSpec((TOP_K, tm), lambda i: (0, i)),
            pl.BlockSpec((TOP_K, tm), lambda i: (0, i)),
            pl.BlockSpec((TOP_K, tm), lambda i: (0, i)),
            full((N_EXPERTS, LANES)),
        ),
        scratch_shapes=[pltpu.VMEM((N_EXPERTS, LANES), F32)],
        compiler_params=_cparams("arbitrary"),
        name="outproj_router",
    )(x, ya, yb, ycc, ycl4, mod, mw, wo, n2, rw, rb)


def _token_tile(ref, tok):
    return ref.at[pl.ds(pl.multiple_of(tok * SUBLANES, SUBLANES), SUBLANES), :]


def _dispatch_kernel(pe_ref, pd_ref, nu_ref, pos_ref, h_hbm, xs_hbm, zero_sc, sem, zsem, *, n_blocks):
    i = pl.program_id(0)
    tm = DISPATCH_TILE

    def zero_block(blk):
        start = pl.multiple_of(blk * (MOE_BM * SUBLANES), SUBLANES)
        return pltpu.make_async_copy(zero_sc, xs_hbm.at[pl.ds(start, MOE_BM * SUBLANES), :], zsem)

    @pl.when(i == 0)
    def _():
        zero_sc[...] = jnp.zeros(zero_sc.shape, F32)
        for e in range(N_EXPERTS):
            @pl.when(pd_ref[e] > 0)
            def _():
                zero_block(pe_ref[e] // MOE_BM - 1).start()
        for e in range(N_EXPERTS):
            @pl.when(pd_ref[e] > 0)
            def _():
                zero_block(0).wait()

        def fill(blk, carry):
            cp = zero_block(blk)
            cp.start()
            cp.wait()
            return carry

        lax.fori_loop(nu_ref[0], n_blocks, fill, 0)

    def issue(r, carry):
        for k in range(TOP_K):
            pltpu.make_async_copy(_token_tile(h_hbm, i * tm + r), _token_tile(xs_hbm, pos_ref[0, 0, k * tm + r]),
                                  sem).start()
        return carry

    lax.fori_loop(0, tm, issue, 0, unroll=8)

    def drain(r, carry):
        for k in range(TOP_K):
            pltpu.make_async_copy(_token_tile(h_hbm, 0), _token_tile(xs_hbm, 0), sem).wait()
        return carry

    lax.fori_loop(0, tm, drain, 0, unroll=8)


def _moe_dispatch(pad_ends, padded, n_used, pos, h2t, n_blocks):
    nt = pos.shape[1]
    tm = DISPATCH_TILE
    n_tiles = nt // tm
    posr = pos.reshape(TOP_K, n_tiles, tm).transpose(1, 0, 2).reshape(n_tiles, 1, TOP_K * tm)
    grid_spec = pltpu.PrefetchScalarGridSpec(
        num_scalar_prefetch=3,
        grid=(n_tiles,),
        in_specs=[
            pl.BlockSpec((1, 1, TOP_K * tm), lambda i, pe, pd, nu: (i, 0, 0), memory_space=pltpu.SMEM),
            pl.BlockSpec(memory_space=pl.ANY),
        ],
        out_specs=pl.BlockSpec(memory_space=pl.ANY),
        scratch_shapes=[
            pltpu.VMEM((MOE_BM * SUBLANES, LANES), F32),
            pltpu.SemaphoreType.DMA(()),
            pltpu.SemaphoreType.DMA(()),
        ],
    )
    return pl.pallas_call(
        functools.partial(_dispatch_kernel, n_blocks=n_blocks),
        out_shape=jax.ShapeDtypeStruct((n_blocks * MOE_BM * SUBLANES, LANES), F32),
        grid_spec=grid_spec,
        compiler_params=_cparams("arbitrary"),
        name="moe_dispatch",
    )(pad_ends, padded, n_used, posr, h2t)


def _ffn_kernel(be_ref, nu_ref, x_ref, wgu_ref, bgu_ref, wd_ref, bd_ref, y_ref, wgu_sc, wd_sc):
    i = pl.program_id(0)
    changed = jnp.logical_or(i == 0, be_ref[i] != be_ref[jnp.maximum(i - 1, 0)])

    @pl.when(changed)
    def _():
        wgu_sc[...] = wgu_ref[0].astype(BF16)
        wd_sc[...] = wd_ref[0].astype(BF16)

    @pl.when(i < nu_ref[0])
    def _():
        x = _load_token_tiles(x_ref, 0, MOE_BM).astype(BF16)
        gu = _dot(x, wgu_sc[...]) + bgu_ref[0]
        gate = jnp.minimum(gu[:, :D_EXPERT], SWIGLU_LIMIT)
        up = jnp.clip(gu[:, D_EXPERT:], -SWIGLU_LIMIT, SWIGLU_LIMIT)
        glu = gate * _sigmoid(SWIGLU_ALPHA * gate)
        act = ((up + 1.0) * glu).astype(BF16)
        _store_token_tiles(y_ref, 0, _dot(act, wd_sc[...]) + bd_ref[0])

    @pl.when(i >= nu_ref[0])
    def _():
        y_ref[...] = jnp.zeros(y_ref.shape, F32)


def _moe_ffn(block_e, n_used, xs, wgu, bgu, wd, bd):
    rows = MOE_BM * SUBLANES
    n_blocks = xs.shape[0] // rows
    ne = wgu.shape[0]
    grid_spec = pltpu.PrefetchScalarGridSpec(
        num_scalar_prefetch=2,
        grid=(n_blocks,),
        in_specs=[
            pl.BlockSpec((rows, LANES), lambda i, be, nu: (jnp.minimum(i, nu[0] - 1), 0)),
            pl.BlockSpec((1, D_MODEL, 2 * D_EXPERT), lambda i, be, nu: (be[i], 0, 0)),
            pl.BlockSpec((1, 1, 2 * D_EXPERT), lambda i, be, nu: (be[i], 0, 0)),
            pl.BlockSpec((1, D_EXPERT, D_MODEL), lambda i, be, nu: (be[i], 0, 0)),
            pl.BlockSpec((1, 1, D_MODEL), lambda i, be, nu: (be[i], 0, 0)),
        ],
        out_specs=pl.BlockSpec((rows, LANES), lambda i, be, nu: (i, 0)),
        scratch_shapes=[
            pltpu.VMEM((D_MODEL, 2 * D_EXPERT), BF16),
            pltpu.VMEM((D_EXPERT, D_MODEL), BF16),
        ],
    )
    return pl.pallas_call(
        _ffn_kernel,
        out_shape=jax.ShapeDtypeStruct((n_blocks * rows, LANES), F32),
        grid_spec=grid_spec,
        compiler_params=_cparams("arbitrary"),
        name="moe_ffn",
    )(block_e, n_used, xs, wgu, bgu.reshape(ne, 1, -1), wd, bd.reshape(ne, 1, -1))


def _combine_kernel(pos_ref, x_ref, gate_ref, mod_ref, fw_ref, ys_hbm, out_ref, buf, sem, *, final_norm):
    tm = x_ref.shape[0]
    n = TOP_K * tm

    def issue(r, carry):
        pltpu.make_async_copy(_token_tile(ys_hbm, pos_ref[0, 0, r]), _token_tile(buf, r), sem).start()
        return carry

    lax.fori_loop(0, n, issue, 0, unroll=8)

    def drain(r, carry):
        pltpu.make_async_copy(_token_tile(ys_hbm, 0), _token_tile(buf, 0), sem).wait()
        return carry

    lax.fori_loop(0, n, drain, 0, unroll=8)
    g = gate_ref[...]
    f = g[:, 0:1] * _load_token_tiles(buf, 0, tm)
    for k in range(1, TOP_K):
        f = f + g[:, k:k + 1] * _load_token_tiles(buf, k * tm, tm)
    x = x_ref[...].reshape(tm // BATCH, BATCH, D_MODEL) + mod_ref[0][None] * f.reshape(tm // BATCH, BATCH, D_MODEL)
    if final_norm:
        ms = jnp.mean(x * x, axis=-1, keepdims=True)
        x = (x * lax.rsqrt(ms + EPS)) * fw_ref[...][None]
    out_ref[...] = x.reshape(tm, D_MODEL)


def _moe_combine(pos, x, gates_t, mod, fw, ys, n_ctx_rows, final_norm):
    nt = x.shape[0]
    tm = COMBINE_TILE
    n_tiles = nt // tm
    nct = n_ctx_rows // tm
    posr = pos.reshape(TOP_K, n_tiles, tm).transpose(1, 0, 2).reshape(n_tiles, 1, TOP_K * tm)
    return pl.pallas_call(
        functools.partial(_combine_kernel, final_norm=final_norm),
        out_shape=jax.ShapeDtypeStruct((nt, D_MODEL), F32),
        grid=(n_tiles,),
        in_specs=[
            pl.BlockSpec((1, 1, TOP_K * tm), lambda i: (i, 0, 0), memory_space=pltpu.SMEM),
            pl.BlockSpec((tm, D_MODEL), lambda i: (i, 0)),
            pl.BlockSpec((tm, TOP_K), lambda i: (i, 0)),
            pl.BlockSpec((1, BATCH, D_MODEL), lambda i: (jnp.where(i < nct, 0, 1), 0, 0)),
            pl.BlockSpec((1, D_MODEL), lambda i: (0, 0)),
            pl.BlockSpec(memory_space=pl.ANY),
        ],
        out_specs=pl.BlockSpec((tm, D_MODEL), lambda i: (i, 0)),
        scratch_shapes=[pltpu.VMEM((TOP_K * tm * SUBLANES, LANES), F32), pltpu.SemaphoreType.DMA(())],
        compiler_params=_cparams("arbitrary"),
        name="moe_combine",
    )(posr, x, gates_t, mod, fw, ys)


def _moe(h2t, idx, gates, rank, counts, x, g2, fw, layer, wgu, bgu, wd, bd, n_ctx_rows, final_norm):
    nt = idx.shape[1]
    bm = MOE_BM
    n_blocks = (nt * TOP_K) // bm + N_EXPERTS
    cnt = counts[:, 0].astype(I32)
    padded = (cnt + bm - 1) // bm * bm
    pad_ends = jnp.cumsum(padded)
    pad_starts = pad_ends - padded
    experts = jnp.arange(N_EXPERTS, dtype=I32)
    pos = rank + jnp.sum(jnp.where(idx[None] == experts[:, None, None], pad_starts[:, None, None], 0), axis=0)
    block_start = jnp.arange(n_blocks, dtype=I32) * bm
    block_e = jnp.minimum(jnp.sum((pad_ends[None, :] <= block_start[:, None]).astype(I32), axis=1), N_EXPERTS - 1)
    n_used = (pad_ends[-1] // bm).astype(I32).reshape(1)
    xs = _moe_dispatch(pad_ends.astype(I32), padded, n_used, pos, h2t, n_blocks)
    ys = _moe_ffn(block_e + layer * N_EXPERTS, n_used, xs, wgu, bgu, wd, bd)
    return _moe_combine(pos, x, gates.T, g2, fw, ys, n_ctx_rows, final_norm)


def _pad_heads(a):
    lead = a.shape[:-1]
    a = a.reshape(*lead, 4 * ML_HEADS, ML_HEAD_DIM)
    a = jnp.pad(a, [(0, 0)] * len(lead) + [(0, 0), (0, LANES - ML_HEAD_DIM)])
    return a.reshape(*lead, 4 * ML_HEADS * LANES)


def _block_diag(blocks):
    return jax.scipy.linalg.block_diag(*[blocks[g] for g in range(blocks.shape[0])])


def _s5_discretise(lam_re, lam_im, log_dt, b_re, b_im):
    dt = jnp.exp(log_dt)[:, None]
    mag = jnp.exp(lam_re * dt)
    ar, ai = mag * jnp.cos(lam_im * dt), mag * jnp.sin(lam_im * dt)
    den = lam_re * lam_re + lam_im * lam_im
    cr = ((ar - 1.0) * lam_re + ai * lam_im) / den
    ci = (ai * lam_re - (ar - 1.0) * lam_im) / den
    bbr = cr[..., None] * b_re - ci[..., None] * b_im
    bbi = cr[..., None] * b_im + ci[..., None] * b_re
    return ar, ai, bbr, bbi


def kernel(x, c, ctx, c_ctx, ada_w, ada_b, norm1_w, w_in, b_in, rg_conv_w, rg_conv_b, rg_wa, rg_ba, rg_wx, rg_bx, rg_lambda, s5_lambda_re, s5_lambda_im, s5_log_dt, s5_b_re, s5_b_im, s5_c_re, s5_c_im, s5_d, s5_glu_w, s5_glu_b, mix_norm_w, w_out, norm2_w, router_w, router_b, moe_w_gate_up, moe_b_gate_up, moe_w_down, moe_b_down, final_norm_w):
    bsz, seq, dm = x.shape
    sc = ctx.shape[1]
    assert bsz == BATCH and dm == D_MODEL and seq % GRID_W == 0
    n_ctx_rows = sc * BATCH
    n_lat_rows = seq * BATCH
    assert n_ctx_rows % CHUNK_ROWS == 0 and n_lat_rows % CHUNK_ROWS == 0 and GRID_W * BATCH == ROW_TILE
    n_ctx_chunks = n_ctx_rows // CHUNK_ROWS
    depth = ada_w.shape[0]

    xs = jnp.concatenate([ctx.transpose(1, 0, 2).reshape(n_ctx_rows, dm),
                          x.transpose(1, 0, 2).reshape(n_lat_rows, dm)], axis=0)

    c_rows = jnp.concatenate([c, c_ctx[None], jnp.zeros((16 - bsz - 1, dm), F32)], axis=0)
    mods = _modulation(c_rows, ada_w, ada_b)
    mods = mods.reshape(depth, 16, 6, dm)
    mod_lat = mods[:, :bsz].transpose(0, 2, 1, 3)
    mod_ctx = jnp.broadcast_to(mods[:, bsz][:, :, None, :], mod_lat.shape)
    mod = jnp.stack([mod_ctx, mod_lat], axis=1)

    for l in range(depth):
        with_ctx = l < depth - 1
        wa_cols = w_in[l][:, :4 * ML_HEADS * ML_HEAD_DIM]
        ba_cols = b_in[l][:4 * ML_HEADS * ML_HEAD_DIM]
        kscale = jnp.ones((4, ML_HEADS * LANES), F32).at[1].set(ML_HEAD_DIM ** -0.5).reshape(-1)
        ones_col = jnp.zeros((4, ML_HEADS, LANES), F32).at[2, :, ML_HEAD_DIM].set(1.0).reshape(-1)
        g0 = 4 * ML_HEADS * ML_HEAD_DIM
        ng = 4 * ML_HEADS
        w_cat = jnp.concatenate([
            _pad_heads(wa_cols) * kscale,
            jnp.pad(w_in[l][:, g0:g0 + ng], ((0, 0), (0, LANES - ng))),
            w_in[l][:, g0 + ng:]], axis=1).astype(BF16)
        b_cat = jnp.concatenate([
            _pad_heads(ba_cols) * kscale + ones_col,
            jnp.pad(b_in[l][g0:g0 + ng], (0, LANES - ng)),
            b_in[l][g0 + ng:]])[None]
        wg = [jnp.concatenate([_block_diag(rg_wa[l, d]), _block_diag(rg_wx[l, d])], axis=1).astype(BF16)
              for d in range(2)]
        bg = [jnp.concatenate([rg_ba[l, d], rg_bx[l, d]])[None] for d in range(2)]
        s5p = []
        for d in range(2):
            ar, ai, bbr, bbi = _s5_discretise(s5_lambda_re[l, d], s5_lambda_im[l, d], s5_log_dt[l, d],
                                              s5_b_re[l], s5_b_im[l])
            a = jnp.stack([ar.reshape(-1), ai.reshape(-1)])
            bd = jnp.concatenate([_block_diag(bbr.transpose(0, 2, 1)), _block_diag(bbi.transpose(0, 2, 1))],
                                 axis=1).astype(BF16)
            cd = jnp.concatenate([_block_diag(s5_c_re[l].transpose(0, 2, 1)),
                                  -_block_diag(s5_c_im[l].transpose(0, 2, 1))], axis=0).astype(BF16)
            s5p.append((a, bd, cd))
        mw = jnp.concatenate([_pad_heads(jnp.tile(mix_norm_w[l][:ML_HEADS * ML_HEAD_DIM], 4))[:ML_HEADS * LANES],
                              mix_norm_w[l][ML_HEADS * ML_HEAD_DIM:]])[None]
        wo_a = w_out[l][:ML_HEADS * ML_HEAD_DIM].reshape(ML_HEADS, ML_HEAD_DIM, dm)
        wo_a = jnp.pad(wo_a, ((0, 0), (0, LANES - ML_HEAD_DIM), (0, 0))).reshape(ML_HEADS * LANES, dm)
        wo = jnp.concatenate([wo_a, w_out[l][ML_HEADS * ML_HEAD_DIM:]], axis=0).astype(BF16)

        pa, pb, pcc, pcl = _inproj(xs, mod[l, :, 0:2], norm1_w[l][None], w_cat, b_cat, n_ctx_rows)
        pcl = pcl.reshape(n_lat_rows, S5_WIDTH)
        hb = _mlstm_pass(pa, None, 1, n_ctx_chunks)
        ya = _mlstm_pass(pa, hb, 0, n_ctx_chunks)
        rgb = _rglru_pass(pb, None, rg_conv_w[l], rg_conv_b[l][None], wg[1], bg[1], rg_lambda[l, 1][None], 1, n_ctx_chunks)
        yb = _rglru_pass(pb, rgb, rg_conv_w[l], rg_conv_b[l][None], wg[0], bg[0], rg_lambda[l, 0][None], 0, n_ctx_chunks)
        zero_state = jnp.zeros((BATCH, 2 * S5_NSTATE), F32)
        a1, bd1, cd1 = s5p[1]
        a0, bd0, cd0 = s5p[0]
        dsk, gw, gb = s5_d[l][None], s5_glu_w[l].astype(BF16), s5_glu_b[l][None]
        ycb_c, st = _s5_pass(pcc, None, zero_state, a1, bd1, cd1, None, None, None, 1)
        ycb_l, _ = _s5_pass(pcl, None, st, a1, bd1, cd1, None, None, None, 1)
        yc_c, st = _s5_pass(pcc, ycb_c, zero_state, a0, bd0, cd0, dsk, gw, gb, 0)
        yc_l, _ = _s5_pass(pcl, ycb_l, st, a0, bd0, cd0, dsk, gw, gb, 0)

        tile0 = 0 if with_ctx else n_ctx_rows // ROW_TILE
        xo, h2, idx, gates, rank, counts = _outproj(
            xs, ya, yb, yc_c, yc_l, mod[l, :, 2:5], mw, wo, norm2_w[l][None], router_w[l].T,
            router_b[l][:, None], tile0, n_ctx_rows)

        xs_new = _moe(h2, idx, gates, rank, counts, xo, mod[l, :, 5], final_norm_w[None], l,
                      moe_w_gate_up.reshape(-1, dm, 2 * D_EXPERT), moe_b_gate_up.reshape(-1, 2 * D_EXPERT),
                      moe_w_down.reshape(-1, D_EXPERT, dm), moe_b_down.reshape(-1, dm),
                      n_ctx_rows if with_ctx else 0, l == depth - 1)
        xs = xs_new

    out = xs.reshape(seq, bsz, dm).transpose(1, 0, 2)
    return out
```

```python
import functools
import math

import jax
import jax.numpy as jnp
from jax import lax
from jax.experimental import pallas as pl
from jax.experimental.pallas import tpu as pltpu

F32 = jnp.float32
BF16 = jnp.bfloat16
I32 = jnp.int32

D_MODEL = 1024
BATCH = 8
DEPTH = 4
GRID_W = 64
ML_HEADS = 4
ML_HEAD_DIM = 96
ML_CHUNK = 128
ML_M_INIT = -1e30
RG_BLOCKS = 6
RG_BLOCK_DIM = 64
RG_WIDTH = RG_BLOCKS * RG_BLOCK_DIM
RG_CONV = 4
RG_C = 8.0
S5_GROUPS = 16
S5_GROUP = 16
S5_WIDTH = S5_GROUPS * S5_GROUP
S5_STATE = 64
S5_NSTATE = S5_GROUPS * S5_STATE
N_EXPERTS = 32
TOP_K = 4
D_EXPERT = D_MODEL
SWIGLU_LIMIT = 7.0
SWIGLU_ALPHA = 1.702
EPS = 1e-6

LANES = 128
SUBLANES = 8

ML_K, ML_V, ML_Q, ML_O = 0, ML_HEADS, 2 * ML_HEADS, 3 * ML_HEADS
ML_IG = 4 * ML_HEADS
ML_FG = ML_IG + 1
ML_SLABS = ML_FG + 1
ML_STATE_SLABS = 2 * ML_HEADS
A_PAD = ML_SLABS * LANES
MERGED = ML_HEADS * LANES + RG_WIDTH + S5_WIDTH

ROW_TILE = 512
T_CHUNK = 128
CHUNK_ROWS = T_CHUNK * BATCH
MOE_BM = 256
DISPATCH_TILE = 256
COMBINE_TILE = 256
VMEM_LIMIT = 56 * 1024 * 1024


def _cparams(*sem):
    return pltpu.CompilerParams(dimension_semantics=sem, vmem_limit_bytes=VMEM_LIMIT)


def _sigmoid(x):
    return 1.0 / (1.0 + jnp.exp(-x))


def _log_sigmoid(x):
    return jnp.minimum(x, 0.0) - jnp.log1p(jnp.exp(-jnp.abs(x)))


def _softplus(x):
    return jnp.maximum(x, 0.0) + jnp.log1p(jnp.exp(-jnp.abs(x)))


def _gelu(x):
    return 0.5 * x * (1.0 + jnp.tanh(0.7978845608028654 * (x + 0.044715 * (x * x * x))))


def _dot(a, b):
    return jnp.dot(a, b, preferred_element_type=F32)


def _dot_nt(a, b):
    return lax.dot_general(a, b, (((1,), (1,)), ((), ())), preferred_element_type=F32)


def _split(a):
    hi = a.astype(BF16)
    lo = (a - hi.astype(F32)).astype(BF16)
    return hi, lo


def _dot_lx(a_exact, b):
    hi, lo = _split(b)
    return _dot(a_exact, hi) + _dot(a_exact, lo)


def _dot_xr(a, b_exact):
    hi, lo = _split(a)
    return _dot(hi, b_exact) + _dot(lo, b_exact)


def _dot3(a, b, nt=False):
    ah, al = _split(a)
    bh, bl = _split(b)
    d = _dot_nt if nt else _dot
    return d(ah, bh) + (d(ah, bl) + d(al, bh))


def _group_ones(n, shift):
    r = lax.broadcasted_iota(I32, (n, n), 0)
    c = lax.broadcasted_iota(I32, (n, n), 1)
    return (lax.shift_right_logical(r, shift) == lax.shift_right_logical(c, shift)).astype(BF16)


def _store_token_tiles(ref, tok0, val):
    n = val.shape[0]
    for s in range(SUBLANES):
        ref[pl.ds(tok0 * SUBLANES + s, n, stride=SUBLANES), :] = val[:, s * LANES:(s + 1) * LANES]


def _load_token_tiles(ref, tok0, n):
    return jnp.concatenate(
        [ref[pl.ds(tok0 * SUBLANES + s, n, stride=SUBLANES), :] for s in range(SUBLANES)], axis=1)


def _mod_kernel(c_ref, w_ref, b_ref, o_ref):
    c = c_ref[...]
    s = c * _sigmoid(c)
    o_ref[0] = _dot3(s, w_ref[0]) + b_ref[0]


def _modulation(c_rows, ada_w, ada_b):
    depth, d, n = ada_w.shape
    tn = 1536
    return pl.pallas_call(
        _mod_kernel,
        out_shape=jax.ShapeDtypeStruct((depth, 16, n), F32),
        grid=(depth, n // tn),
        in_specs=[
            pl.BlockSpec((16, d), lambda l, j: (0, 0)),
            pl.BlockSpec((1, d, tn), lambda l, j: (l, 0, j)),
            pl.BlockSpec((1, 1, tn), lambda l, j: (l, 0, j)),
        ],
        out_specs=pl.BlockSpec((1, 16, tn), lambda l, j: (l, 0, j)),
        compiler_params=_cparams("arbitrary", "arbitrary"),
        name="adaln_mod",
    )(c_rows, ada_w, ada_b.reshape(depth, 1, n))


def _inproj_kernel(x_ref, mod_ref, nw_ref, w_ref, b_ref, pa_ref, pb_ref, pcc_ref, pcl_ref, *, n_ctx_tiles):
    i = pl.program_id(0)
    x = x_ref[...]
    tm = x.shape[0]
    ms = jnp.mean(x * x, axis=-1, keepdims=True)
    xn = (x * lax.rsqrt(ms + EPS)) * nw_ref[...]
    xn = xn.reshape(tm // BATCH, BATCH, D_MODEL)
    h = xn * (1.0 + mod_ref[0, 1][None]) + mod_ref[0, 0][None]
    h = h.reshape(tm, D_MODEL).astype(BF16)
    for j in range(0, ML_SLABS, 2):
        p = _dot(h, w_ref[:, j * LANES:(j + 2) * LANES]) + b_ref[:, j * LANES:(j + 2) * LANES]
        pa_ref[j] = p[:, :LANES]
        pa_ref[j + 1] = p[:, LANES:]
    c0 = A_PAD
    pb_ref[...] = _dot(h, w_ref[:, c0:c0 + 2 * RG_WIDTH]) + b_ref[:, c0:c0 + 2 * RG_WIDTH]
    c0 = A_PAD + 2 * RG_WIDTH
    pc = _dot(h, w_ref[:, c0:c0 + S5_WIDTH]) + b_ref[:, c0:c0 + S5_WIDTH]

    @pl.when(i < n_ctx_tiles)
    def _():
        pcc_ref[...] = pc

    @pl.when(i >= n_ctx_tiles)
    def _():
        pcl_ref[...] = pc.reshape(GRID_W, 1, BATCH, S5_WIDTH)


def _inproj(x, mod, nw, w, b, n_ctx_rows):
    nt = x.shape[0]
    tm = ROW_TILE
    nct = n_ctx_rows // tm
    n_lat_rows = nt - n_ctx_rows
    rows = n_lat_rows // tm
    ncols = w.shape[1]
    return pl.pallas_call(
        functools.partial(_inproj_kernel, n_ctx_tiles=nct),
        out_shape=(
            jax.ShapeDtypeStruct((ML_SLABS, nt, LANES), F32),
            jax.ShapeDtypeStruct((nt, 2 * RG_WIDTH), F32),
            jax.ShapeDtypeStruct((n_ctx_rows, S5_WIDTH), F32),
            jax.ShapeDtypeStruct((GRID_W, rows, BATCH, S5_WIDTH), F32),
        ),
        grid=(nt // tm,),
        in_specs=[
            pl.BlockSpec((tm, D_MODEL), lambda i: (i, 0)),
            pl.BlockSpec((1, 2, BATCH, D_MODEL), lambda i: (jnp.where(i < nct, 0, 1), 0, 0, 0)),
            pl.BlockSpec((1, D_MODEL), lambda i: (0, 0)),
            pl.BlockSpec((D_MODEL, ncols), lambda i: (0, 0)),
            pl.BlockSpec((1, ncols), lambda i: (0, 0)),
        ],
        out_specs=(
            pl.BlockSpec((ML_SLABS, tm, LANES), lambda i: (0, i, 0)),
            pl.BlockSpec((tm, 2 * RG_WIDTH), lambda i: (i, 0)),
            pl.BlockSpec((tm, S5_WIDTH), lambda i: (jnp.minimum(i, nct - 1), 0)),
            pl.BlockSpec((GRID_W, 1, BATCH, S5_WIDTH), lambda i: (0, jnp.maximum(i - nct, 0), 0, 0)),
        ),
        compiler_params=_cparams("arbitrary"),
        name="inproj",
    )(x, mod, nw, w, b)


def _mlstm_gates(ig, fg):
    L = ML_CHUNK
    row = lax.broadcasted_iota(I32, (L, L), 0)
    col = lax.broadcasted_iota(I32, (L, L), 1)
    lf = _log_sigmoid(fg)
    b_fwd = _dot_lx((row >= col).astype(BF16), lf)
    b_tot = _dot_lx(jnp.ones((L, L), BF16), lf)
    b_bwd = b_tot - b_fwd + lf
    bcol = jnp.where(col < ML_HEADS, b_fwd, b_bwd)
    return ig - bcol, bcol, b_tot


def _mlstm_state_kernel(kv_ref, g_ref, c0_ref, m0_ref, c_sc, m_sc, *, direction):
    i = pl.program_id(0)
    L = ML_CHUNK

    @pl.when(i == 0)
    def _():
        c_sc[...] = jnp.zeros(c_sc.shape, F32)
        m_sc[...] = jnp.full(m_sc.shape, ML_M_INIT, F32)

    def per_batch(b, carry):
        rows = pl.ds(b, L, stride=BATCH)
        z, _, b_tot = _mlstm_gates(g_ref[0, rows, :], g_ref[1, rows, :])
        z_t = z.T
        for h in range(ML_HEADS):
            j = ML_HEADS * direction + h
            idx = b * ML_HEADS + h
            k = kv_ref[ML_K + h, rows, :].astype(BF16)
            v_t = kv_ref[ML_V + h, rows, :].T
            c0 = c_sc[idx]
            m0_tile = m_sc[idx]
            c0_ref[0, idx] = c0.astype(BF16)
            m0_ref[0, idx] = m0_tile
            m0 = m0_tile[0:1, :]
            bt = jnp.broadcast_to(b_tot[0:1, j:j + 1], (1, LANES))
            w = bt + z_t[j:j + 1, :]
            mloc = jnp.broadcast_to(jnp.max(w, axis=1, keepdims=True), (1, LANES))
            cloc = _dot((v_t * jnp.exp(w - mloc)).astype(BF16), k)
            mnew = jnp.maximum(bt + m0, mloc)
            a = jnp.exp(bt + m0 - mnew)
            sc = jnp.exp(mloc - mnew)
            c_sc[idx] = a * c0 + sc * cloc
            m_sc[idx] = jnp.broadcast_to(mnew, (SUBLANES, LANES))
        return carry

    lax.fori_loop(0, BATCH, per_batch, 0)


def _mlstm_out_kernel(pa_ref, c0f_ref, m0f_ref, c0b_ref, m0b_ref, out_ref):
    L = ML_CHUNK
    row = lax.broadcasted_iota(I32, (L, L), 0)
    col = lax.broadcasted_iota(I32, (L, L), 1)
    masks = (row <= col, row >= col)
    head_row = row < ML_HEAD_DIM
    m0_refs = (m0f_ref, m0b_ref)

    def per_batch(b, carry):
        rows = pl.ds(b, L, stride=BATCH)
        z, bcol, _ = _mlstm_gates(pa_ref[ML_IG, rows, :], pa_ref[ML_FG, rows, :])
        bcol_t = bcol.T
        for h in range(ML_HEADS):
            idx = b * ML_HEADS + h
            q = pa_ref[ML_Q + h, rows, :].astype(BF16)
            k = pa_ref[ML_K + h, rows, :].astype(BF16)
            v_t = pa_ref[ML_V + h, rows, :].T.astype(BF16)
            s_t = _dot_nt(k, q)
            nc_t = _dot_nt(jnp.concatenate([c0f_ref[0, idx], c0b_ref[0, idx]], axis=0), q)
            w_t = None
            h_t = None
            for d in range(2):
                j = ML_HEADS * d + h
                m0 = m0_refs[d][0, idx][0:1, :]
                r = jnp.where(masks[d], z[:, j:j + 1], -jnp.inf)
                u = jnp.maximum(jnp.max(r, axis=0, keepdims=True), m0)
                sqk = s_t * jnp.exp(r - u)
                inter = jnp.exp(m0 - u)
                ncd = nc_t[d * LANES:(d + 1) * LANES, :]
                den = jnp.sum(sqk, axis=0, keepdims=True) + inter * ncd[ML_HEAD_DIM:ML_HEAD_DIM + 1, :]
                inv = 1.0 / jnp.maximum(jnp.abs(den), jnp.exp(-(bcol_t[j:j + 1, :] + u)))
                w_t = sqk * inv if w_t is None else w_t + sqk * inv
                h_t = (inter * inv) * ncd if h_t is None else h_t + (inter * inv) * ncd
            h_t = jnp.where(head_row, h_t + _dot(v_t, w_t.astype(BF16)), 0.0)
            ms = jnp.sum(h_t * h_t, axis=0, keepdims=True) * (1.0 / ML_HEAD_DIM)
            o_t = pa_ref[ML_O + h, rows, :].T
            out_ref[h, rows, :] = (_sigmoid(o_t) * (h_t * lax.rsqrt(ms + EPS))).T
        return carry

    lax.fori_loop(0, BATCH, per_batch, 0)


def _chunk_order(i, direction, n_ctx, n_all):
    if direction == 0:
        return i
    return jnp.where(i < n_ctx, n_ctx - 1 - i, n_all - 1 + n_ctx - i)


def _mlstm_states(pa, direction, n_ctx):
    nt = pa.shape[1]
    n_all = nt // CHUNK_ROWS
    nbh = BATCH * ML_HEADS
    order = lambda i: _chunk_order(i, direction, n_ctx, n_all)
    return pl.pallas_call(
        functools.partial(_mlstm_state_kernel, direction=direction),
        out_shape=(jax.ShapeDtypeStruct((n_all, nbh, ML_CHUNK, LANES), BF16),
                   jax.ShapeDtypeStruct((n_all, nbh, SUBLANES, LANES), F32)),
        grid=(n_all,),
        in_specs=[
            pl.BlockSpec((ML_STATE_SLABS, CHUNK_ROWS, LANES), lambda i: (0, order(i), 0)),
            pl.BlockSpec((2, CHUNK_ROWS, LANES), lambda i: (ML_IG // 2, order(i), 0)),
        ],
        out_specs=(pl.BlockSpec((1, nbh, ML_CHUNK, LANES), lambda i: (order(i), 0, 0, 0)),
                   pl.BlockSpec((1, nbh, SUBLANES, LANES), lambda i: (order(i), 0, 0, 0))),
        scratch_shapes=[
            pltpu.VMEM((nbh, ML_CHUNK, LANES), F32),
            pltpu.VMEM((nbh, SUBLANES, LANES), F32),
        ],
        compiler_params=_cparams("arbitrary"),
        name="mlstm_state_fwd" if direction == 0 else "mlstm_state_bwd",
    )(pa, pa)


def _mlstm_outputs(pa, c0f, m0f, c0b, m0b):
    nt = pa.shape[1]
    n_all = nt // CHUNK_ROWS
    nbh = BATCH * ML_HEADS
    cspec = pl.BlockSpec((1, nbh, ML_CHUNK, LANES), lambda i: (i, 0, 0, 0))
    mspec = pl.BlockSpec((1, nbh, SUBLANES, LANES), lambda i: (i, 0, 0, 0))
    return pl.pallas_call(
        _mlstm_out_kernel,
        out_shape=jax.ShapeDtypeStruct((ML_HEADS, nt, LANES), F32),
        grid=(n_all,),
        in_specs=[pl.BlockSpec((ML_SLABS, CHUNK_ROWS, LANES), lambda i: (0, i, 0)), cspec, mspec, cspec, mspec],
        out_specs=pl.BlockSpec((ML_HEADS, CHUNK_ROWS, LANES), lambda i: (0, i, 0)),
        compiler_params=_cparams("arbitrary"),
        name="mlstm_out",
    )(pa, c0f, m0f, c0b, m0b)


def _rglru_kernel(*refs, direction, final, n_ctx, n_all):
    if final:
        cur_ref, prev_ref, next_ref, hb_ref, cw_ref, cb_ref, wg_ref, bg_ref, lam_ref, out_ref, a_sc, b_sc, h_sc = refs
    else:
        cur_ref, prev_ref, next_ref, cw_ref, cb_ref, wg_ref, bg_ref, lam_ref, out_ref, a_sc, b_sc, h_sc = refs
        hb_ref = None
    i = pl.program_id(0)
    c = _chunk_order(i, direction, n_ctx, n_all)
    w = RG_WIDTH

    @pl.when(i == 0)
    def _():
        h_sc[...] = jnp.zeros(h_sc.shape, F32)

    seg_first = jnp.logical_or(c == 0, c == n_ctx)
    seg_last = jnp.logical_or(c == n_ctx - 1, c == n_all - 1)
    cur = cur_ref[:, :w]
    prev = jnp.where(seg_first, 0.0, prev_ref[:, :w])
    nxt = jnp.where(seg_last, 0.0, next_ref[:, :w])
    xc = jnp.concatenate([prev, cur, nxt], axis=0)
    n = CHUNK_ROWS
    xconv = cb_ref[...] + cw_ref[0:1, :] * xc[0:n]
    for j in range(1, RG_CONV):
        xconv = xconv + cw_ref[j:j + 1, :] * xc[j * BATCH:j * BATCH + n]
    z = _dot(xconv.astype(BF16), wg_ref[...]) + bg_ref[...]
    r = _sigmoid(z[:, :w])
    ig = _sigmoid(z[:, w:])
    log_a = (-RG_C * _softplus(-lam_ref[...])) * r
    a = jnp.exp(log_a)
    a_sc[...] = a
    b_sc[...] = jnp.sqrt(-jnp.tanh(log_a) * (a * a + 1.0)) * (ig * xconv)

    def step(t, h):
        tt = t if direction == 0 else T_CHUNK - 1 - t
        rows = pl.ds(pl.multiple_of(tt * BATCH, BATCH), BATCH)
        h = a_sc[rows, :] * h + b_sc[rows, :]
        b_sc[rows, :] = h
        return h

    h_sc[...] = lax.fori_loop(0, T_CHUNK, step, h_sc[...], unroll=8)
    hs = b_sc[...]
    if final:
        ht = hs + hb_ref[...]
        ss = _dot_xr(ht * ht, _group_ones(w, 6)) * (1.0 / RG_BLOCK_DIM)
        hs = _gelu(cur_ref[:, w:]) * (ht * lax.rsqrt(ss + EPS))
    out_ref[...] = hs


def _rglru_pass(pb, hb, cw, cb, wg, bg, lam, direction, n_ctx):
    nt = pb.shape[0]
    n_all = nt // CHUNK_ROWS
    final = hb is not None
    w2 = 2 * RG_WIDTH
    order = lambda i: _chunk_order(i, direction, n_ctx, n_all)
    halo_p = CHUNK_ROWS // (2 * BATCH)
    halo_n = CHUNK_ROWS // BATCH
    in_specs = [
        pl.BlockSpec((CHUNK_ROWS, w2), lambda i: (order(i), 0)),
        pl.BlockSpec((2 * BATCH, w2), lambda i: (jnp.maximum(order(i) * halo_p - 1, 0), 0)),
        pl.BlockSpec((BATCH, w2), lambda i: (jnp.minimum((order(i) + 1) * halo_n, nt // BATCH - 1), 0)),
    ]
    args = [pb, pb, pb]
    if final:
        in_specs.append(pl.BlockSpec((CHUNK_ROWS, RG_WIDTH), lambda i: (order(i), 0)))
        args.append(hb)
    full = lambda s: pl.BlockSpec(s, lambda i: tuple(0 for _ in s))
    in_specs += [full((RG_CONV, RG_WIDTH)), full((1, RG_WIDTH)), full((RG_WIDTH, w2)), full((1, w2)), full((1, RG_WIDTH))]
    args += [cw, cb, wg, bg, lam]
    return pl.pallas_call(
        functools.partial(_rglru_kernel, direction=direction, final=final, n_ctx=n_ctx, n_all=n_all),
        out_shape=jax.ShapeDtypeStruct((nt, RG_WIDTH), F32),
        grid=(n_all,),
        in_specs=in_specs,
        out_specs=pl.BlockSpec((CHUNK_ROWS, RG_WIDTH), lambda i: (order(i), 0)),
        scratch_shapes=[
            pltpu.VMEM((CHUNK_ROWS, RG_WIDTH), F32),
            pltpu.VMEM((CHUNK_ROWS, RG_WIDTH), F32),
            pltpu.VMEM((BATCH, RG_WIDTH), F32),
        ],
        compiler_params=_cparams("arbitrary"),
        name="rglru_fwd" if final else "rglru_bwd",
    )(*args)


def _s5_kernel(*refs, direction, final):
    if final:
        (u_ref, yb_ref, h0_ref, a_ref, bd_ref, cd_ref, dsk_ref, gw_ref, gb_ref,
         out_ref, hN_ref, v_sc, h_sc) = refs
    else:
        u_ref, h0_ref, a_ref, bd_ref, cd_ref, out_ref, hN_ref, v_sc, h_sc = refs
    i = pl.program_id(0)
    n = S5_NSTATE

    @pl.when(i == 0)
    def _():
        h_sc[...] = h0_ref[...]

    u = u_ref[...]
    v_sc[...] = _dot(u.astype(BF16), bd_ref[...])
    ar = jnp.broadcast_to(a_ref[0:1, :], (BATCH, n))
    ai = jnp.broadcast_to(a_ref[1:2, :], (BATCH, n))

    def step(t, carry):
        hr, hi = carry
        tt = t if direction == 0 else T_CHUNK - 1 - t
        rows = pl.ds(pl.multiple_of(tt * BATCH, BATCH), BATCH)
        nr = ar * hr - ai * hi + v_sc[rows, :n]
        ni = ar * hi + ai * hr + v_sc[rows, n:]
        v_sc[rows, :n] = nr
        v_sc[rows, n:] = ni
        return nr, ni

    hr, hi = lax.fori_loop(0, T_CHUNK, step, (h_sc[:, :n], h_sc[:, n:]))
    h_sc[:, :n] = hr
    h_sc[:, n:] = hi
    hN_ref[...] = h_sc[...]
    y = _dot(v_sc[...].astype(BF16), cd_ref[...])
    if final:
        y = y + yb_ref[...] + dsk_ref[...] * u
        g = _gelu(y)
        o = g * _sigmoid(_dot(g.astype(BF16), gw_ref[...]) + gb_ref[...])
        ss = _dot_xr(o * o, _group_ones(S5_WIDTH, 4)) * (1.0 / S5_GROUP)
        y = o * lax.rsqrt(ss + EPS)
    out_ref[...] = y


def _s5_pass(u, yb, h0, a, bd, cd, dsk, gw, gb, direction):
    nt = u.shape[0]
    n_chunks = nt // CHUNK_ROWS
    final = yb is not None
    order = (lambda i: i) if direction == 0 else (lambda i: n_chunks - 1 - i)
    full = lambda s: pl.BlockSpec(s, lambda i: tuple(0 for _ in s))
    chunk = pl.BlockSpec((CHUNK_ROWS, S5_WIDTH), lambda i: (order(i), 0))
    in_specs = [chunk]
    args = [u]
    if final:
        in_specs.append(chunk)
        args.append(yb)
    in_specs += [full((BATCH, 2 * S5_NSTATE)), full((2, S5_NSTATE)), full((S5_WIDTH, 2 * S5_NSTATE)),
                 full((2 * S5_NSTATE, S5_WIDTH))]
    args += [h0, a, bd, cd]
    if final:
        in_specs += [full((1, S5_WIDTH)), full((S5_WIDTH, S5_WIDTH)), full((1, S5_WIDTH))]
        args += [dsk, gw, gb]
    return pl.pallas_call(
        functools.partial(_s5_kernel, direction=direction, final=final),
        out_shape=(jax.ShapeDtypeStruct((nt, S5_WIDTH), F32), jax.ShapeDtypeStruct((BATCH, 2 * S5_NSTATE), F32)),
        grid=(n_chunks,),
        in_specs=in_specs,
        out_specs=(chunk, full((BATCH, 2 * S5_NSTATE))),
        scratch_shapes=[
            pltpu.VMEM((CHUNK_ROWS, 2 * S5_NSTATE), F32),
            pltpu.VMEM((BATCH, 2 * S5_NSTATE), F32),
        ],
        compiler_params=_cparams("arbitrary"),
        name="s5_fwd" if final else "s5_bwd",
    )(*args)


def _outproj_kernel(x_ref, ya_ref, yb_ref, ycc_ref, ycl_ref, mod_ref, mw_ref, wo_ref, n2_ref, rw_ref, rb_ref,
                    xo_ref, h2_ref, idx_ref, gate_ref, rank_ref, cnt_ref, cnt_sc, *, tile0, n_ctx_tiles):
    i = pl.program_id(0)
    tile = i + tile0
    tm = x_ref.shape[0]

    @pl.when(i == 0)
    def _():
        cnt_sc[...] = jnp.zeros(cnt_sc.shape, F32)

    yc = jnp.where(tile < n_ctx_tiles, ycc_ref[...], ycl_ref[...].reshape(tm, S5_WIDTH))
    merged = jnp.concatenate([ya_ref[h] for h in range(ML_HEADS)] + [yb_ref[...], yc], axis=1)
    merged = (merged * mw_ref[...]).astype(BF16)
    proj = _dot(merged, wo_ref[...])
    g1 = mod_ref[0, 0]
    x = x_ref[...].reshape(tm // BATCH, BATCH, D_MODEL) + g1[None] * proj.reshape(tm // BATCH, BATCH, D_MODEL)
    xo_ref[...] = x.reshape(tm, D_MODEL)
    ms = jnp.mean(x * x, axis=-1, keepdims=True)
    xn = (x * lax.rsqrt(ms + EPS)) * n2_ref[...][None]
    h2 = (xn * (1.0 + mod_ref[0, 2][None]) + mod_ref[0, 1][None]).reshape(tm, D_MODEL)
    _store_token_tiles(h2_ref, 0, h2)
    logits = _dot3(rw_ref[...], h2, nt=True) + rb_ref[...]
    eidx = lax.broadcasted_iota(I32, (N_EXPERTS, tm), 0)
    vals, ids, hots = [], [], []
    for _ in range(TOP_K):
        m = jnp.max(logits, axis=0, keepdims=True)
        sel = jnp.min(jnp.where(logits == m, eidx, N_EXPERTS), axis=0, keepdims=True)
        hot = eidx == sel
        logits = jnp.where(hot, -jnp.inf, logits)
        vals.append(m)
        ids.append(sel)
        hots.append(hot)
    ex = [jnp.exp(v - vals[0]) for v in vals]
    tot = ex[0] + ex[1] + ex[2] + ex[3]
    idx_ref[...] = jnp.concatenate(ids, axis=0)
    gate_ref[...] = jnp.concatenate([e / tot for e in ex], axis=0)
    selm = hots[0].astype(F32) + hots[1].astype(F32) + hots[2].astype(F32) + hots[3].astype(F32)
    r = lax.broadcasted_iota(I32, (tm, tm), 0)
    c = lax.broadcasted_iota(I32, (tm, tm), 1)
    before = _dot(selm.astype(BF16), (r < c).astype(BF16)) + cnt_sc[:, 0:1]
    ranks = [jnp.sum(jnp.where(hot, before, 0.0), axis=0, keepdims=True) for hot in hots]
    rank_ref[...] = jnp.concatenate(ranks, axis=0).astype(I32)
    cnt_sc[...] = cnt_sc[...] + jnp.sum(selm, axis=1, keepdims=True)
    cnt_ref[...] = cnt_sc[...]


def _outproj(x, ya, yb, ycc, ycl, mod, mw, wo, n2, rw, rb, tile0, n_ctx_rows):
    nt = x.shape[0]
    tm = ROW_TILE
    nct = n_ctx_rows // tm
    n_tiles = nt // tm - tile0
    n_out = n_tiles * tm
    rows = ycl.shape[0] // tm
    ycl4 = ycl.reshape(GRID_W, rows, BATCH, S5_WIDTH)
    full = lambda s: pl.BlockSpec(s, lambda i: tuple(0 for _ in s))
    return pl.pallas_call(
        functools.partial(_outproj_kernel, tile0=tile0, n_ctx_tiles=nct),
        out_shape=(
            jax.ShapeDtypeStruct((n_out, D_MODEL), F32),
            jax.ShapeDtypeStruct((n_out * SUBLANES, LANES), F32),
            jax.ShapeDtypeStruct((TOP_K, n_out), I32),
            jax.ShapeDtypeStruct((TOP_K, n_out), F32),
            jax.ShapeDtypeStruct((TOP_K, n_out), I32),
            jax.ShapeDtypeStruct((N_EXPERTS, LANES), F32),
        ),
        grid=(n_tiles,),
        in_specs=[
            pl.BlockSpec((tm, D_MODEL), lambda i: (i + tile0, 0)),
            pl.BlockSpec((ML_HEADS, tm, LANES), lambda i: (0, i + tile0, 0)),
            pl.BlockSpec((tm, RG_WIDTH), lambda i: (i + tile0, 0)),
            pl.BlockSpec((tm, S5_WIDTH), lambda i: (jnp.minimum(i + tile0, nct - 1), 0)),
            pl.BlockSpec((GRID_W, 1, BATCH, S5_WIDTH), lambda i: (0, jnp.maximum(i + tile0 - nct, 0), 0, 0)),
            pl.BlockSpec((1, 3, BATCH, D_MODEL), lambda i: (jnp.where(i + tile0 < nct, 0, 1), 0, 0, 0)),
            full((1, MERGED)),
            full((MERGED, D_MODEL)),
            full((1, D_MODEL)),
            full((N_EXPERTS, D_MODEL)),
            full((N_EXPERTS, 1)),
        ],
        out_specs=(
            pl.BlockSpec((tm, D_MODEL), lambda i: (i, 0)),
            pl.BlockSpec((tm * SUBLANES, LANES), lambda i: (i, 0)),
            pl.BlockSpec((TOP_K, tm), lambda i: (0, i)),
            pl.BlockSpec((TOP_K, tm), lambda i: (0, i)),
            pl.BlockSpec((TOP_K, tm), lambda i: (0, i)),
            full((N_EXPERTS, LANES)),
        ),
        scratch_shapes=[pltpu.VMEM((N_EXPERTS, LANES), F32)],
        compiler_params=_cparams("arbitrary"),
        name="outproj_router",
    )(x, ya, yb, ycc, ycl4, mod, mw, wo, n2, rw, rb)


def _token_tile(ref, tok):
    return ref.at[pl.ds(pl.multiple_of(tok * SUBLANES, SUBLANES), SUBLANES), :]


def _dispatch_kernel(pe_ref, pd_ref, nu_ref, pos_ref, h_ref, xs_hbm, zero_sc, sem, zsem, *, n_blocks):
    i = pl.program_id(0)
    tm = DISPATCH_TILE

    def zero_block(blk):
        start = pl.multiple_of(blk * (MOE_BM * SUBLANES), SUBLANES)
        return pltpu.make_async_copy(zero_sc, xs_hbm.at[pl.ds(start, MOE_BM * SUBLANES), :], zsem)

    @pl.when(i == 0)
    def _():
        zero_sc[...] = jnp.zeros(zero_sc.shape, F32)
        for e in range(N_EXPERTS):
            @pl.when(pd_ref[e] > 0)
            def _():
                zero_block(pe_ref[e] // MOE_BM - 1).start()
        for e in range(N_EXPERTS):
            @pl.when(pd_ref[e] > 0)
            def _():
                zero_block(0).wait()

        def fill(blk, carry):
            cp = zero_block(blk)
            cp.start()
            cp.wait()
            return carry

        lax.fori_loop(nu_ref[0], n_blocks, fill, 0)

    def issue(r, carry):
        for k in range(TOP_K):
            pltpu.make_async_copy(_token_tile(h_ref, r), _token_tile(xs_hbm, pos_ref[0, 0, k * tm + r]),
                                  sem).start()
        return carry

    lax.fori_loop(0, tm, issue, 0, unroll=8)

    def drain(r, carry):
        for k in range(TOP_K):
            pltpu.make_async_copy(_token_tile(h_ref, 0), _token_tile(xs_hbm, 0), sem).wait()
        return carry

    lax.fori_loop(0, tm, drain, 0, unroll=8)


def _moe_dispatch(pad_ends, padded, n_used, pos, h2t, n_blocks):
    nt = pos.shape[1]
    tm = DISPATCH_TILE
    n_tiles = nt // tm
    posr = pos.reshape(TOP_K, n_tiles, tm).transpose(1, 0, 2).reshape(n_tiles, 1, TOP_K * tm)
    grid_spec = pltpu.PrefetchScalarGridSpec(
        num_scalar_prefetch=3,
        grid=(n_tiles,),
        in_specs=[
            pl.BlockSpec((1, 1, TOP_K * tm), lambda i, pe, pd, nu: (i, 0, 0), memory_space=pltpu.SMEM),
            pl.BlockSpec((tm * SUBLANES, LANES), lambda i, pe, pd, nu: (i, 0)),
        ],
        out_specs=pl.BlockSpec(memory_space=pl.ANY),
        scratch_shapes=[
            pltpu.VMEM((MOE_BM * SUBLANES, LANES), F32),
            pltpu.SemaphoreType.DMA(()),
            pltpu.SemaphoreType.DMA(()),
        ],
    )
    return pl.pallas_call(
        functools.partial(_dispatch_kernel, n_blocks=n_blocks),
        out_shape=jax.ShapeDtypeStruct((n_blocks * MOE_BM * SUBLANES, LANES), F32),
        grid_spec=grid_spec,
        compiler_params=_cparams("arbitrary"),
        name="moe_dispatch",
    )(pad_ends, padded, n_used, posr, h2t)


def _ffn_kernel(be_ref, nu_ref, x_ref, wgu_ref, bgu_ref, wd_ref, bd_ref, y_ref, wgu_sc, wd_sc):
    i = pl.program_id(0)
    changed = jnp.logical_or(i == 0, be_ref[i] != be_ref[jnp.maximum(i - 1, 0)])

    @pl.when(changed)
    def _():
        wgu_sc[...] = wgu_ref[0].astype(BF16)
        wd_sc[...] = wd_ref[0].astype(BF16)

    @pl.when(i < nu_ref[0])
    def _():
        x = _load_token_tiles(x_ref, 0, MOE_BM).astype(BF16)
        gu = _dot(x, wgu_sc[...]) + bgu_ref[0]
        gate = jnp.minimum(gu[:, :D_EXPERT], SWIGLU_LIMIT)
        up = jnp.clip(gu[:, D_EXPERT:], -SWIGLU_LIMIT, SWIGLU_LIMIT)
        glu = gate * _sigmoid(SWIGLU_ALPHA * gate)
        act = ((up + 1.0) * glu).astype(BF16)
        _store_token_tiles(y_ref, 0, _dot(act, wd_sc[...]) + bd_ref[0])

    @pl.when(i >= nu_ref[0])
    def _():
        y_ref[...] = jnp.zeros(y_ref.shape, F32)


def _moe_ffn(block_e, n_used, xs, wgu, bgu, wd, bd):
    rows = MOE_BM * SUBLANES
    n_blocks = xs.shape[0] // rows
    ne = wgu.shape[0]
    grid_spec = pltpu.PrefetchScalarGridSpec(
        num_scalar_prefetch=2,
        grid=(n_blocks,),
        in_specs=[
            pl.BlockSpec((rows, LANES), lambda i, be, nu: (jnp.minimum(i, nu[0] - 1), 0)),
            pl.BlockSpec((1, D_MODEL, 2 * D_EXPERT), lambda i, be, nu: (be[i], 0, 0)),
            pl.BlockSpec((1, 1, 2 * D_EXPERT), lambda i, be, nu: (be[i], 0, 0)),
            pl.BlockSpec((1, D_EXPERT, D_MODEL), lambda i, be, nu: (be[i], 0, 0)),
            pl.BlockSpec((1, 1, D_MODEL), lambda i, be, nu: (be[i], 0, 0)),
        ],
        out_specs=pl.BlockSpec((rows, LANES), lambda i, be, nu: (i, 0)),
        scratch_shapes=[
            pltpu.VMEM((D_MODEL, 2 * D_EXPERT), BF16),
            pltpu.VMEM((D_EXPERT, D_MODEL), BF16),
        ],
    )
    return pl.pallas_call(
        _ffn_kernel,
        out_shape=jax.ShapeDtypeStruct((n_blocks * rows, LANES), F32),
        grid_spec=grid_spec,
        compiler_params=_cparams("arbitrary"),
        name="moe_ffn",
    )(block_e, n_used, xs, wgu, bgu.reshape(ne, 1, -1), wd, bd.reshape(ne, 1, -1))


def _combine_kernel(pos_ref, x_ref, gate_ref, mod_ref, fw_ref, ys_hbm, out_ref, buf, sem, *, final_norm):
    tm = x_ref.shape[0]
    n = TOP_K * tm

    def issue(r, carry):
        pltpu.make_async_copy(_token_tile(ys_hbm, pos_ref[0, 0, r]), _token_tile(buf, r), sem).start()
        return carry

    lax.fori_loop(0, n, issue, 0, unroll=8)

    def drain(r, carry):
        pltpu.make_async_copy(_token_tile(ys_hbm, 0), _token_tile(buf, 0), sem).wait()
        return carry

    lax.fori_loop(0, n, drain, 0, unroll=8)
    g = gate_ref[...]
    f = g[:, 0:1] * _load_token_tiles(buf, 0, tm)
    for k in range(1, TOP_K):
        f = f + g[:, k:k + 1] * _load_token_tiles(buf, k * tm, tm)
    x = x_ref[...].reshape(tm // BATCH, BATCH, D_MODEL) + mod_ref[0][None] * f.reshape(tm // BATCH, BATCH, D_MODEL)
    if final_norm:
        ms = jnp.mean(x * x, axis=-1, keepdims=True)
        x = (x * lax.rsqrt(ms + EPS)) * fw_ref[...][None]
    out_ref[...] = x.reshape(tm, D_MODEL)


def _moe_combine(pos, x, gates_t, mod, fw, ys, n_ctx_rows, final_norm):
    nt = x.shape[0]
    tm = COMBINE_TILE
    n_tiles = nt // tm
    nct = n_ctx_rows // tm
    posr = pos.reshape(TOP_K, n_tiles, tm).transpose(1, 0, 2).reshape(n_tiles, 1, TOP_K * tm)
    return pl.pallas_call(
        functools.partial(_combine_kernel, final_norm=final_norm),
        out_shape=jax.ShapeDtypeStruct((nt, D_MODEL), F32),
        grid=(n_tiles,),
        in_specs=[
            pl.BlockSpec((1, 1, TOP_K * tm), lambda i: (i, 0, 0), memory_space=pltpu.SMEM),
            pl.BlockSpec((tm, D_MODEL), lambda i: (i, 0)),
            pl.BlockSpec((tm, TOP_K), lambda i: (i, 0)),
            pl.BlockSpec((1, BATCH, D_MODEL), lambda i: (jnp.where(i < nct, 0, 1), 0, 0)),
            pl.BlockSpec((1, D_MODEL), lambda i: (0, 0)),
            pl.BlockSpec(memory_space=pl.ANY),
        ],
        out_specs=pl.BlockSpec((tm, D_MODEL), lambda i: (i, 0)),
        scratch_shapes=[pltpu.VMEM((TOP_K * tm * SUBLANES, LANES), F32), pltpu.SemaphoreType.DMA(())],
        compiler_params=_cparams("arbitrary"),
        name="moe_combine",
    )(posr, x, gates_t, mod, fw, ys)


def _moe(h2t, idx, gates, rank, counts, x, g2, fw, layer, wgu, bgu, wd, bd, n_ctx_rows, final_norm):
    nt = idx.shape[1]
    bm = MOE_BM
    n_blocks = (nt * TOP_K) // bm + N_EXPERTS
    cnt = counts[:, 0].astype(I32)
    padded = (cnt + bm - 1) // bm * bm
    pad_ends = jnp.cumsum(padded)
    pad_starts = pad_ends - padded
    experts = jnp.arange(N_EXPERTS, dtype=I32)
    pos = rank + jnp.sum(jnp.where(idx[None] == experts[:, None, None], pad_starts[:, None, None], 0), axis=0)
    block_start = jnp.arange(n_blocks, dtype=I32) * bm
    block_e = jnp.minimum(jnp.sum((pad_ends[None, :] <= block_start[:, None]).astype(I32), axis=1), N_EXPERTS - 1)
    n_used = (pad_ends[-1] // bm).astype(I32).reshape(1)
    xs = _moe_dispatch(pad_ends.astype(I32), padded, n_used, pos, h2t, n_blocks)
    ys = _moe_ffn(block_e + layer * N_EXPERTS, n_used, xs, wgu, bgu, wd, bd)
    return _moe_combine(pos, x, gates.T, g2, fw, ys, n_ctx_rows, final_norm)


def _pad_heads(a):
    lead = a.shape[:-1]
    a = a.reshape(*lead, 4 * ML_HEADS, ML_HEAD_DIM)
    a = jnp.pad(a, [(0, 0)] * len(lead) + [(0, 0), (0, LANES - ML_HEAD_DIM)])
    return a.reshape(*lead, 4 * ML_HEADS * LANES)


def _kvqo(a):
    w = ML_HEADS * ML_HEAD_DIM
    return jnp.concatenate([a[..., w:2 * w], a[..., 2 * w:3 * w], a[..., :w], a[..., 3 * w:4 * w]], axis=-1)


def _block_diag(blocks):
    return jax.scipy.linalg.block_diag(*[blocks[g] for g in range(blocks.shape[0])])


def _s5_discretise(lam_re, lam_im, log_dt, b_re, b_im):
    dt = jnp.exp(log_dt)[:, None]
    mag = jnp.exp(lam_re * dt)
    ar, ai = mag * jnp.cos(lam_im * dt), mag * jnp.sin(lam_im * dt)
    den = lam_re * lam_re + lam_im * lam_im
    cr = ((ar - 1.0) * lam_re + ai * lam_im) / den
    ci = (ai * lam_re - (ar - 1.0) * lam_im) / den
    bbr = cr[..., None] * b_re - ci[..., None] * b_im
    bbi = cr[..., None] * b_im + ci[..., None] * b_re
    return ar, ai, bbr, bbi


def kernel(x, c, ctx, c_ctx, ada_w, ada_b, norm1_w, w_in, b_in, rg_conv_w, rg_conv_b, rg_wa, rg_ba, rg_wx, rg_bx, rg_lambda, s5_lambda_re, s5_lambda_im, s5_log_dt, s5_b_re, s5_b_im, s5_c_re, s5_c_im, s5_d, s5_glu_w, s5_glu_b, mix_norm_w, w_out, norm2_w, router_w, router_b, moe_w_gate_up, moe_b_gate_up, moe_w_down, moe_b_down, final_norm_w):
    bsz, seq, dm = x.shape
    sc = ctx.shape[1]
    assert bsz == BATCH and dm == D_MODEL and seq % GRID_W == 0
    n_ctx_rows = sc * BATCH
    n_lat_rows = seq * BATCH
    assert n_ctx_rows % CHUNK_ROWS == 0 and n_lat_rows % CHUNK_ROWS == 0 and GRID_W * BATCH == ROW_TILE
    n_ctx_chunks = n_ctx_rows // CHUNK_ROWS
    depth = ada_w.shape[0]

    xs = jnp.concatenate([ctx.transpose(1, 0, 2).reshape(n_ctx_rows, dm),
                          x.transpose(1, 0, 2).reshape(n_lat_rows, dm)], axis=0)

    c_rows = jnp.concatenate([c, c_ctx[None], jnp.zeros((16 - bsz - 1, dm), F32)], axis=0)
    mods = _modulation(c_rows, ada_w, ada_b)
    mods = mods.reshape(depth, 16, 6, dm)
    mod_lat = mods[:, :bsz].transpose(0, 2, 1, 3)
    mod_ctx = jnp.broadcast_to(mods[:, bsz][:, :, None, :], mod_lat.shape)
    mod = jnp.stack([mod_ctx, mod_lat], axis=1)

    for l in range(depth):
        with_ctx = l < depth - 1
        g0 = 4 * ML_HEADS * ML_HEAD_DIM
        ng = 2 * ML_HEADS
        wa_cols = _kvqo(w_in[l][:, :g0])
        ba_cols = _kvqo(b_in[l][:g0])
        kscale = jnp.ones((4, ML_HEADS * LANES), F32).at[0].set(ML_HEAD_DIM ** -0.5).reshape(-1)
        ones_col = jnp.zeros((4, ML_HEADS, LANES), F32).at[1, :, ML_HEAD_DIM].set(1.0).reshape(-1)
        w_cat = jnp.concatenate([
            _pad_heads(wa_cols) * kscale,
            jnp.pad(w_in[l][:, g0:g0 + ng], ((0, 0), (0, LANES - ng))),
            jnp.pad(w_in[l][:, g0 + ng:g0 + 2 * ng], ((0, 0), (0, LANES - ng))),
            w_in[l][:, g0 + 2 * ng:]], axis=1).astype(BF16)
        b_cat = jnp.concatenate([
            _pad_heads(ba_cols) * kscale + ones_col,
            jnp.pad(b_in[l][g0:g0 + ng], (0, LANES - ng)),
            jnp.pad(b_in[l][g0 + ng:g0 + 2 * ng], (0, LANES - ng)),
            b_in[l][g0 + 2 * ng:]])[None]
        wg = [jnp.concatenate([_block_diag(rg_wa[l, d]), _block_diag(rg_wx[l, d])], axis=1).astype(BF16)
              for d in range(2)]
        bg = [jnp.concatenate([rg_ba[l, d], rg_bx[l, d]])[None] for d in range(2)]
        s5p = []
        for d in range(2):
            ar, ai, bbr, bbi = _s5_discretise(s5_lambda_re[l, d], s5_lambda_im[l, d], s5_log_dt[l, d],
                                              s5_b_re[l], s5_b_im[l])
            a = jnp.stack([ar.reshape(-1), ai.reshape(-1)])
            bd = jnp.concatenate([_block_diag(bbr.transpose(0, 2, 1)), _block_diag(bbi.transpose(0, 2, 1))],
                                 axis=1).astype(BF16)
            cd = jnp.concatenate([_block_diag(s5_c_re[l].transpose(0, 2, 1)),
                                  -_block_diag(s5_c_im[l].transpose(0, 2, 1))], axis=0).astype(BF16)
            s5p.append((a, bd, cd))
        mw = jnp.concatenate([_pad_heads(jnp.tile(mix_norm_w[l][:ML_HEADS * ML_HEAD_DIM], 4))[:ML_HEADS * LANES],
                              mix_norm_w[l][ML_HEADS * ML_HEAD_DIM:]])[None]
        wo_a = w_out[l][:ML_HEADS * ML_HEAD_DIM].reshape(ML_HEADS, ML_HEAD_DIM, dm)
        wo_a = jnp.pad(wo_a, ((0, 0), (0, LANES - ML_HEAD_DIM), (0, 0))).reshape(ML_HEADS * LANES, dm)
        wo = jnp.concatenate([wo_a, w_out[l][ML_HEADS * ML_HEAD_DIM:]], axis=0).astype(BF16)

        pa, pb, pcc, pcl = _inproj(xs, mod[l, :, 0:2], norm1_w[l][None], w_cat, b_cat, n_ctx_rows)
        pcl = pcl.reshape(n_lat_rows, S5_WIDTH)
        c0f, m0f = _mlstm_states(pa, 0, n_ctx_chunks)
        c0b, m0b = _mlstm_states(pa, 1, n_ctx_chunks)
        ya = _mlstm_outputs(pa, c0f, m0f, c0b, m0b)
        rgb = _rglru_pass(pb, None, rg_conv_w[l], rg_conv_b[l][None], wg[1], bg[1], rg_lambda[l, 1][None], 1, n_ctx_chunks)
        yb = _rglru_pass(pb, rgb, rg_conv_w[l], rg_conv_b[l][None], wg[0], bg[0], rg_lambda[l, 0][None], 0, n_ctx_chunks)
        zero_state = jnp.zeros((BATCH, 2 * S5_NSTATE), F32)
        a1, bd1, cd1 = s5p[1]
        a0, bd0, cd0 = s5p[0]
        dsk, gw, gb = s5_d[l][None], s5_glu_w[l].astype(BF16), s5_glu_b[l][None]
        ycb_c, st = _s5_pass(pcc, None, zero_state, a1, bd1, cd1, None, None, None, 1)
        ycb_l, _ = _s5_pass(pcl, None, st, a1, bd1, cd1, None, None, None, 1)
        yc_c, st = _s5_pass(pcc, ycb_c, zero_state, a0, bd0, cd0, dsk, gw, gb, 0)
        yc_l, _ = _s5_pass(pcl, ycb_l, st, a0, bd0, cd0, dsk, gw, gb, 0)

        tile0 = 0 if with_ctx else n_ctx_rows // ROW_TILE
        xo, h2, idx, gates, rank, counts = _outproj(
            xs, ya, yb, yc_c, yc_l, mod[l, :, 2:5], mw, wo, norm2_w[l][None], router_w[l].T,
            router_b[l][:, None], tile0, n_ctx_rows)

        xs_new = _moe(h2, idx, gates, rank, counts, xo, mod[l, :, 5], final_norm_w[None], l,
                      moe_w_gate_up.reshape(-1, dm, 2 * D_EXPERT), moe_b_gate_up.reshape(-1, 2 * D_EXPERT),
                      moe_w_down.reshape(-1, D_EXPERT, dm), moe_b_down.reshape(-1, dm),
                      n_ctx_rows if with_ctx else 0, l == depth - 1)
        xs = xs_new

    out = xs.reshape(seq, bsz, dm).transpose(1, 0, 2)
    return out
```

```python
import functools
import math

import jax
import jax.numpy as jnp
from jax import lax
from jax.experimental import pallas as pl
from jax.experimental.pallas import tpu as pltpu

F32 = jnp.float32
BF16 = jnp.bfloat16
I32 = jnp.int32

D_MODEL = 1024
BATCH = 8
DEPTH = 4
GRID_W = 64
ML_HEADS = 4
ML_HEAD_DIM = 96
ML_CHUNK = 128
ML_M_INIT = -1e30
RG_BLOCKS = 6
RG_BLOCK_DIM = 64
RG_WIDTH = RG_BLOCKS * RG_BLOCK_DIM
RG_CONV = 4
RG_C = 8.0
S5_GROUPS = 16
S5_GROUP = 16
S5_WIDTH = S5_GROUPS * S5_GROUP
S5_STATE = 64
S5_NSTATE = S5_GROUPS * S5_STATE
N_EXPERTS = 32
TOP_K = 4
D_EXPERT = D_MODEL
SWIGLU_LIMIT = 7.0
SWIGLU_ALPHA = 1.702
EPS = 1e-6

LANES = 128
SUBLANES = 8

ML_K, ML_V, ML_Q, ML_O = 0, ML_HEADS, 2 * ML_HEADS, 3 * ML_HEADS
ML_IG = 4 * ML_HEADS
ML_FG = ML_IG + 1
ML_SLABS = ML_FG + 1
ML_STATE_SLABS = 2 * ML_HEADS
A_PAD = ML_SLABS * LANES
MERGED = ML_HEADS * LANES + RG_WIDTH + S5_WIDTH

ROW_TILE = 512
T_CHUNK = 128
CHUNK_ROWS = T_CHUNK * BATCH
MOE_BM = 512
DMA_PRIORITIES = 2
DISPATCH_TILE = 256
COMBINE_TILE = 256
VMEM_LIMIT = 56 * 1024 * 1024


def _cparams(*sem):
    return pltpu.CompilerParams(dimension_semantics=sem, vmem_limit_bytes=VMEM_LIMIT)


def _sigmoid(x):
    return 1.0 / (1.0 + jnp.exp(-x))


def _log_sigmoid(x):
    return jnp.minimum(x, 0.0) - jnp.log1p(jnp.exp(-jnp.abs(x)))


def _softplus(x):
    return jnp.maximum(x, 0.0) + jnp.log1p(jnp.exp(-jnp.abs(x)))


def _gelu(x):
    return 0.5 * x * (1.0 + jnp.tanh(0.7978845608028654 * (x + 0.044715 * (x * x * x))))


def _dot(a, b):
    return jnp.dot(a, b, preferred_element_type=F32)


def _dot_nt(a, b):
    return lax.dot_general(a, b, (((1,), (1,)), ((), ())), preferred_element_type=F32)


def _split(a):
    hi = a.astype(BF16)
    lo = (a - hi.astype(F32)).astype(BF16)
    return hi, lo


def _dot_lx(a_exact, b):
    hi, lo = _split(b)
    return _dot(a_exact, hi) + _dot(a_exact, lo)


def _dot_xr(a, b_exact):
    hi, lo = _split(a)
    return _dot(hi, b_exact) + _dot(lo, b_exact)


def _dot3(a, b, nt=False):
    ah, al = _split(a)
    bh, bl = _split(b)
    d = _dot_nt if nt else _dot
    return d(ah, bh) + (d(ah, bl) + d(al, bh))


def _group_ones(n, shift):
    r = lax.broadcasted_iota(I32, (n, n), 0)
    c = lax.broadcasted_iota(I32, (n, n), 1)
    return (lax.shift_right_logical(r, shift) == lax.shift_right_logical(c, shift)).astype(BF16)


def _store_token_tiles(ref, tok0, val):
    n = val.shape[0]
    for s in range(SUBLANES):
        ref[pl.ds(tok0 * SUBLANES + s, n, stride=SUBLANES), :] = val[:, s * LANES:(s + 1) * LANES]


def _load_token_tiles(ref, tok0, n):
    return jnp.concatenate(
        [ref[pl.ds(tok0 * SUBLANES + s, n, stride=SUBLANES), :] for s in range(SUBLANES)], axis=1)


def _mod_kernel(c_ref, w_ref, b_ref, o_ref):
    c = c_ref[...]
    s = c * _sigmoid(c)
    o_ref[0] = _dot3(s, w_ref[0]) + b_ref[0]


def _modulation(c_rows, ada_w, ada_b):
    depth, d, n = ada_w.shape
    tn = 1536
    return pl.pallas_call(
        _mod_kernel,
        out_shape=jax.ShapeDtypeStruct((depth, 16, n), F32),
        grid=(depth, n // tn),
        in_specs=[
            pl.BlockSpec((16, d), lambda l, j: (0, 0)),
            pl.BlockSpec((1, d, tn), lambda l, j: (l, 0, j)),
            pl.BlockSpec((1, 1, tn), lambda l, j: (l, 0, j)),
        ],
        out_specs=pl.BlockSpec((1, 16, tn), lambda l, j: (l, 0, j)),
        compiler_params=_cparams("arbitrary", "arbitrary"),
        name="adaln_mod",
    )(c_rows, ada_w, ada_b.reshape(depth, 1, n))


def _inproj_kernel(x_ref, mod_ref, nw_ref, w_ref, b_ref, pa_ref, pb_ref, pcc_ref, pcl_ref, *, n_ctx_tiles):
    i = pl.program_id(0)
    x = x_ref[...]
    tm = x.shape[0]
    ms = jnp.mean(x * x, axis=-1, keepdims=True)
    xn = (x * lax.rsqrt(ms + EPS)) * nw_ref[...]
    xn = xn.reshape(tm // BATCH, BATCH, D_MODEL)
    h = xn * (1.0 + mod_ref[0, 1][None]) + mod_ref[0, 0][None]
    h = h.reshape(tm, D_MODEL).astype(BF16)
    for j in range(0, ML_SLABS, 2):
        p = _dot(h, w_ref[:, j * LANES:(j + 2) * LANES]) + b_ref[:, j * LANES:(j + 2) * LANES]
        pa_ref[j] = p[:, :LANES]
        pa_ref[j + 1] = p[:, LANES:]
    c0 = A_PAD
    pb_ref[...] = _dot(h, w_ref[:, c0:c0 + 2 * RG_WIDTH]) + b_ref[:, c0:c0 + 2 * RG_WIDTH]
    c0 = A_PAD + 2 * RG_WIDTH
    pc = _dot(h, w_ref[:, c0:c0 + S5_WIDTH]) + b_ref[:, c0:c0 + S5_WIDTH]

    @pl.when(i < n_ctx_tiles)
    def _():
        pcc_ref[...] = pc

    @pl.when(i >= n_ctx_tiles)
    def _():
        pcl_ref[...] = pc.reshape(GRID_W, 1, BATCH, S5_WIDTH)


def _inproj(x, mod, nw, w, b, n_ctx_rows):
    nt = x.shape[0]
    tm = ROW_TILE
    nct = n_ctx_rows // tm
    n_lat_rows = nt - n_ctx_rows
    rows = n_lat_rows // tm
    ncols = w.shape[1]
    return pl.pallas_call(
        functools.partial(_inproj_kernel, n_ctx_tiles=nct),
        out_shape=(
            jax.ShapeDtypeStruct((ML_SLABS, nt, LANES), F32),
            jax.ShapeDtypeStruct((nt, 2 * RG_WIDTH), F32),
            jax.ShapeDtypeStruct((n_ctx_rows, S5_WIDTH), F32),
            jax.ShapeDtypeStruct((GRID_W, rows, BATCH, S5_WIDTH), F32),
        ),
        grid=(nt // tm,),
        in_specs=[
            pl.BlockSpec((tm, D_MODEL), lambda i: (i, 0)),
            pl.BlockSpec((1, 2, BATCH, D_MODEL), lambda i: (jnp.where(i < nct, 0, 1), 0, 0, 0)),
            pl.BlockSpec((1, D_MODEL), lambda i: (0, 0)),
            pl.BlockSpec((D_MODEL, ncols), lambda i: (0, 0)),
            pl.BlockSpec((1, ncols), lambda i: (0, 0)),
        ],
        out_specs=(
            pl.BlockSpec((ML_SLABS, tm, LANES), lambda i: (0, i, 0)),
            pl.BlockSpec((tm, 2 * RG_WIDTH), lambda i: (i, 0)),
            pl.BlockSpec((tm, S5_WIDTH), lambda i: (jnp.minimum(i, nct - 1), 0)),
            pl.BlockSpec((GRID_W, 1, BATCH, S5_WIDTH), lambda i: (0, jnp.maximum(i - nct, 0), 0, 0)),
        ),
        compiler_params=_cparams("arbitrary"),
        name="inproj",
    )(x, mod, nw, w, b)


def _mlstm_gates(ig, fg):
    L = ML_CHUNK
    row = lax.broadcasted_iota(I32, (L, L), 0)
    col = lax.broadcasted_iota(I32, (L, L), 1)
    lf = _log_sigmoid(fg)
    b_fwd = _dot_lx((row >= col).astype(BF16), lf)
    b_tot = _dot_lx(jnp.ones((L, L), BF16), lf)
    b_bwd = b_tot - b_fwd + lf
    bcol = jnp.where(col < ML_HEADS, b_fwd, b_bwd)
    return ig - bcol, bcol, b_tot


def _mlstm_state_kernel(kv_ref, g_ref, c0_ref, m0_ref, c_sc, m_sc, *, direction):
    i = pl.program_id(0)
    L = ML_CHUNK

    @pl.when(i == 0)
    def _():
        c_sc[...] = jnp.zeros(c_sc.shape, F32)
        m_sc[...] = jnp.full(m_sc.shape, ML_M_INIT, F32)

    def per_batch(b, carry):
        rows = pl.ds(b, L, stride=BATCH)
        z, _, b_tot = _mlstm_gates(g_ref[0, rows, :], g_ref[1, rows, :])
        z_t = z.T
        for h in range(ML_HEADS):
            j = ML_HEADS * direction + h
            idx = b * ML_HEADS + h
            k = kv_ref[ML_K + h, rows, :].astype(BF16)
            v_t = kv_ref[ML_V + h, rows, :].T
            c0 = c_sc[idx]
            m0_tile = m_sc[idx]
            c0_ref[0, idx] = c0.astype(BF16)
            m0_ref[0, idx] = m0_tile
            m0 = m0_tile[0:1, :]
            bt = jnp.broadcast_to(b_tot[0:1, j:j + 1], (1, LANES))
            w = bt + z_t[j:j + 1, :]
            mloc = jnp.broadcast_to(jnp.max(w, axis=1, keepdims=True), (1, LANES))
            cloc = _dot((v_t * jnp.exp(w - mloc)).astype(BF16), k)
            mnew = jnp.maximum(bt + m0, mloc)
            a = jnp.exp(bt + m0 - mnew)
            sc = jnp.exp(mloc - mnew)
            c_sc[idx] = a * c0 + sc * cloc
            m_sc[idx] = jnp.broadcast_to(mnew, (SUBLANES, LANES))
        return carry

    lax.fori_loop(0, BATCH, per_batch, 0, unroll=2)


def _mlstm_out_kernel(pa_ref, c0f_ref, m0f_ref, c0b_ref, m0b_ref, out_ref):
    L = ML_CHUNK
    row = lax.broadcasted_iota(I32, (L, L), 0)
    col = lax.broadcasted_iota(I32, (L, L), 1)
    masks = (row <= col, row >= col)
    head_row = row < ML_HEAD_DIM
    m0_refs = (m0f_ref, m0b_ref)

    def per_batch(b, carry):
        rows = pl.ds(b, L, stride=BATCH)
        z, bcol, _ = _mlstm_gates(pa_ref[ML_IG, rows, :], pa_ref[ML_FG, rows, :])
        bcol_t = bcol.T
        for h in range(ML_HEADS):
            idx = b * ML_HEADS + h
            q = pa_ref[ML_Q + h, rows, :].astype(BF16)
            k = pa_ref[ML_K + h, rows, :].astype(BF16)
            v_t = pa_ref[ML_V + h, rows, :].T.astype(BF16)
            s_t = _dot_nt(k, q)
            nc_t = _dot_nt(jnp.concatenate([c0f_ref[0, idx], c0b_ref[0, idx]], axis=0), q)
            w_t = None
            h_t = None
            for d in range(2):
                j = ML_HEADS * d + h
                m0 = m0_refs[d][0, idx][0:1, :]
                r = jnp.where(masks[d], z[:, j:j + 1], -jnp.inf)
                u = jnp.maximum(jnp.max(r, axis=0, keepdims=True), m0)
                sqk = s_t * jnp.exp(r - u)
                inter = jnp.exp(m0 - u)
                ncd = nc_t[d * LANES:(d + 1) * LANES, :]
                den = jnp.sum(sqk, axis=0, keepdims=True) + inter * ncd[ML_HEAD_DIM:ML_HEAD_DIM + 1, :]
                inv = 1.0 / jnp.maximum(jnp.abs(den), jnp.exp(-(bcol_t[j:j + 1, :] + u)))
                w_t = sqk * inv if w_t is None else w_t + sqk * inv
                h_t = (inter * inv) * ncd if h_t is None else h_t + (inter * inv) * ncd
            h_t = jnp.where(head_row, h_t + _dot(v_t, w_t.astype(BF16)), 0.0)
            ms = jnp.sum(h_t * h_t, axis=0, keepdims=True) * (1.0 / ML_HEAD_DIM)
            o_t = pa_ref[ML_O + h, rows, :].T
            out_ref[h, rows, :] = (_sigmoid(o_t) * (h_t * lax.rsqrt(ms + EPS))).T
        return carry

    lax.fori_loop(0, BATCH, per_batch, 0, unroll=2)


def _chunk_order(i, direction, n_ctx, n_all):
    if direction == 0:
        return i
    return jnp.where(i < n_ctx, n_ctx - 1 - i, n_all - 1 + n_ctx - i)


def _mlstm_states(pa, direction, n_ctx):
    nt = pa.shape[1]
    n_all = nt // CHUNK_ROWS
    nbh = BATCH * ML_HEADS
    order = lambda i: _chunk_order(i, direction, n_ctx, n_all)
    return pl.pallas_call(
        functools.partial(_mlstm_state_kernel, direction=direction),
        out_shape=(jax.ShapeDtypeStruct((n_all, nbh, ML_CHUNK, LANES), BF16),
                   jax.ShapeDtypeStruct((n_all, nbh, SUBLANES, LANES), F32)),
        grid=(n_all,),
        in_specs=[
            pl.BlockSpec((ML_STATE_SLABS, CHUNK_ROWS, LANES), lambda i: (0, order(i), 0)),
            pl.BlockSpec((2, CHUNK_ROWS, LANES), lambda i: (ML_IG // 2, order(i), 0)),
        ],
        out_specs=(pl.BlockSpec((1, nbh, ML_CHUNK, LANES), lambda i: (order(i), 0, 0, 0)),
                   pl.BlockSpec((1, nbh, SUBLANES, LANES), lambda i: (order(i), 0, 0, 0))),
        scratch_shapes=[
            pltpu.VMEM((nbh, ML_CHUNK, LANES), F32),
            pltpu.VMEM((nbh, SUBLANES, LANES), F32),
        ],
        compiler_params=_cparams("arbitrary"),
        name="mlstm_state_fwd" if direction == 0 else "mlstm_state_bwd",
    )(pa, pa)


def _mlstm_outputs(pa, c0f, m0f, c0b, m0b):
    nt = pa.shape[1]
    n_all = nt // CHUNK_ROWS
    nbh = BATCH * ML_HEADS
    cspec = pl.BlockSpec((1, nbh, ML_CHUNK, LANES), lambda i: (i, 0, 0, 0))
    mspec = pl.BlockSpec((1, nbh, SUBLANES, LANES), lambda i: (i, 0, 0, 0))
    return pl.pallas_call(
        _mlstm_out_kernel,
        out_shape=jax.ShapeDtypeStruct((ML_HEADS, nt, LANES), F32),
        grid=(n_all,),
        in_specs=[pl.BlockSpec((ML_SLABS, CHUNK_ROWS, LANES), lambda i: (0, i, 0)), cspec, mspec, cspec, mspec],
        out_specs=pl.BlockSpec((ML_HEADS, CHUNK_ROWS, LANES), lambda i: (0, i, 0)),
        compiler_params=_cparams("arbitrary"),
        name="mlstm_out",
    )(pa, c0f, m0f, c0b, m0b)


def _rglru_kernel(*refs, direction, final, n_ctx, n_all):
    if final:
        cur_ref, prev_ref, next_ref, hb_ref, cw_ref, cb_ref, wg_ref, bg_ref, lam_ref, out_ref, a_sc, b_sc, h_sc = refs
    else:
        cur_ref, prev_ref, next_ref, cw_ref, cb_ref, wg_ref, bg_ref, lam_ref, out_ref, a_sc, b_sc, h_sc = refs
        hb_ref = None
    i = pl.program_id(0)
    c = _chunk_order(i, direction, n_ctx, n_all)
    w = RG_WIDTH

    @pl.when(i == 0)
    def _():
        h_sc[...] = jnp.zeros(h_sc.shape, F32)

    seg_first = jnp.logical_or(c == 0, c == n_ctx)
    seg_last = jnp.logical_or(c == n_ctx - 1, c == n_all - 1)
    cur = cur_ref[:, :w]
    prev = jnp.where(seg_first, 0.0, prev_ref[:, :w])
    nxt = jnp.where(seg_last, 0.0, next_ref[:, :w])
    xc = jnp.concatenate([prev, cur, nxt], axis=0)
    n = CHUNK_ROWS
    xconv = cb_ref[...] + cw_ref[0:1, :] * xc[0:n]
    for j in range(1, RG_CONV):
        xconv = xconv + cw_ref[j:j + 1, :] * xc[j * BATCH:j * BATCH + n]
    z = _dot(xconv.astype(BF16), wg_ref[...]) + bg_ref[...]
    r = _sigmoid(z[:, :w])
    ig = _sigmoid(z[:, w:])
    log_a = (-RG_C * _softplus(-lam_ref[...])) * r
    a = jnp.exp(log_a)
    a_sc[...] = a
    b_sc[...] = jnp.sqrt(-jnp.tanh(log_a) * (a * a + 1.0)) * (ig * xconv)

    def step(t, h):
        tt = t if direction == 0 else T_CHUNK - 1 - t
        rows = pl.ds(pl.multiple_of(tt * BATCH, BATCH), BATCH)
        h = a_sc[rows, :] * h + b_sc[rows, :]
        b_sc[rows, :] = h
        return h

    h_sc[...] = lax.fori_loop(0, T_CHUNK, step, h_sc[...], unroll=8)
    hs = b_sc[...]
    if final:
        ht = hs + hb_ref[...]
        ss = _dot_xr(ht * ht, _group_ones(w, 6)) * (1.0 / RG_BLOCK_DIM)
        hs = _gelu(cur_ref[:, w:]) * (ht * lax.rsqrt(ss + EPS))
    out_ref[...] = hs


def _rglru_pass(pb, hb, cw, cb, wg, bg, lam, direction, n_ctx):
    nt = pb.shape[0]
    n_all = nt // CHUNK_ROWS
    final = hb is not None
    w2 = 2 * RG_WIDTH
    order = lambda i: _chunk_order(i, direction, n_ctx, n_all)
    halo_p = CHUNK_ROWS // (2 * BATCH)
    halo_n = CHUNK_ROWS // BATCH
    in_specs = [
        pl.BlockSpec((CHUNK_ROWS, w2), lambda i: (order(i), 0)),
        pl.BlockSpec((2 * BATCH, w2), lambda i: (jnp.maximum(order(i) * halo_p - 1, 0), 0)),
        pl.BlockSpec((BATCH, w2), lambda i: (jnp.minimum((order(i) + 1) * halo_n, nt // BATCH - 1), 0)),
    ]
    args = [pb, pb, pb]
    if final:
        in_specs.append(pl.BlockSpec((CHUNK_ROWS, RG_WIDTH), lambda i: (order(i), 0)))
        args.append(hb)
    full = lambda s: pl.BlockSpec(s, lambda i: tuple(0 for _ in s))
    in_specs += [full((RG_CONV, RG_WIDTH)), full((1, RG_WIDTH)), full((RG_WIDTH, w2)), full((1, w2)), full((1, RG_WIDTH))]
    args += [cw, cb, wg, bg, lam]
    return pl.pallas_call(
        functools.partial(_rglru_kernel, direction=direction, final=final, n_ctx=n_ctx, n_all=n_all),
        out_shape=jax.ShapeDtypeStruct((nt, RG_WIDTH), F32),
        grid=(n_all,),
        in_specs=in_specs,
        out_specs=pl.BlockSpec((CHUNK_ROWS, RG_WIDTH), lambda i: (order(i), 0)),
        scratch_shapes=[
            pltpu.VMEM((CHUNK_ROWS, RG_WIDTH), F32),
            pltpu.VMEM((CHUNK_ROWS, RG_WIDTH), F32),
            pltpu.VMEM((BATCH, RG_WIDTH), F32),
        ],
        compiler_params=_cparams("arbitrary"),
        name="rglru_fwd" if final else "rglru_bwd",
    )(*args)


def _s5_kernel(*refs, direction, final):
    if final:
        (u_ref, yb_ref, h0_ref, a_ref, bd_ref, cd_ref, dsk_ref, gw_ref, gb_ref,
         out_ref, hN_ref, v_sc, h_sc) = refs
    else:
        u_ref, h0_ref, a_ref, bd_ref, cd_ref, out_ref, hN_ref, v_sc, h_sc = refs
    i = pl.program_id(0)
    n = S5_NSTATE

    @pl.when(i == 0)
    def _():
        h_sc[...] = h0_ref[...]

    u = u_ref[...]
    v_sc[...] = _dot(u.astype(BF16), bd_ref[...])
    ar = jnp.broadcast_to(a_ref[0:1, :], (BATCH, n))
    ai = jnp.broadcast_to(a_ref[1:2, :], (BATCH, n))

    def step(t, carry):
        hr, hi = carry
        tt = t if direction == 0 else T_CHUNK - 1 - t
        rows = pl.ds(pl.multiple_of(tt * BATCH, BATCH), BATCH)
        nr = ar * hr - ai * hi + v_sc[rows, :n]
        ni = ar * hi + ai * hr + v_sc[rows, n:]
        v_sc[rows, :n] = nr
        v_sc[rows, n:] = ni
        return nr, ni

    hr, hi = lax.fori_loop(0, T_CHUNK, step, (h_sc[:, :n], h_sc[:, n:]))
    h_sc[:, :n] = hr
    h_sc[:, n:] = hi
    hN_ref[...] = h_sc[...]
    y = _dot(v_sc[...].astype(BF16), cd_ref[...])
    if final:
        y = y + yb_ref[...] + dsk_ref[...] * u
        g = _gelu(y)
        o = g * _sigmoid(_dot(g.astype(BF16), gw_ref[...]) + gb_ref[...])
        ss = _dot_xr(o * o, _group_ones(S5_WIDTH, 4)) * (1.0 / S5_GROUP)
        y = o * lax.rsqrt(ss + EPS)
    out_ref[...] = y


def _s5_pass(u, yb, h0, a, bd, cd, dsk, gw, gb, direction):
    nt = u.shape[0]
    n_chunks = nt // CHUNK_ROWS
    final = yb is not None
    order = (lambda i: i) if direction == 0 else (lambda i: n_chunks - 1 - i)
    full = lambda s: pl.BlockSpec(s, lambda i: tuple(0 for _ in s))
    chunk = pl.BlockSpec((CHUNK_ROWS, S5_WIDTH), lambda i: (order(i), 0))
    in_specs = [chunk]
    args = [u]
    if final:
        in_specs.append(chunk)
        args.append(yb)
    in_specs += [full((BATCH, 2 * S5_NSTATE)), full((2, S5_NSTATE)), full((S5_WIDTH, 2 * S5_NSTATE)),
                 full((2 * S5_NSTATE, S5_WIDTH))]
    args += [h0, a, bd, cd]
    if final:
        in_specs += [full((1, S5_WIDTH)), full((S5_WIDTH, S5_WIDTH)), full((1, S5_WIDTH))]
        args += [dsk, gw, gb]
    return pl.pallas_call(
        functools.partial(_s5_kernel, direction=direction, final=final),
        out_shape=(jax.ShapeDtypeStruct((nt, S5_WIDTH), F32), jax.ShapeDtypeStruct((BATCH, 2 * S5_NSTATE), F32)),
        grid=(n_chunks,),
        in_specs=in_specs,
        out_specs=(chunk, full((BATCH, 2 * S5_NSTATE))),
        scratch_shapes=[
            pltpu.VMEM((CHUNK_ROWS, 2 * S5_NSTATE), F32),
            pltpu.VMEM((BATCH, 2 * S5_NSTATE), F32),
        ],
        compiler_params=_cparams("arbitrary"),
        name="s5_fwd" if final else "s5_bwd",
    )(*args)


def _outproj_kernel(x_ref, ya_ref, yb_ref, ycc_ref, ycl_ref, mod_ref, mw_ref, wo_ref, n2_ref, rw_ref, rb_ref,
                    xo_ref, h2_ref, idx_ref, gate_ref, rank_ref, cnt_ref, cnt_sc, *, tile0, n_ctx_tiles):
    i = pl.program_id(0)
    tile = i + tile0
    tm = x_ref.shape[0]

    @pl.when(i == 0)
    def _():
        cnt_sc[...] = jnp.zeros(cnt_sc.shape, F32)

    yc = jnp.where(tile < n_ctx_tiles, ycc_ref[...], ycl_ref[...].reshape(tm, S5_WIDTH))
    merged = jnp.concatenate([ya_ref[h] for h in range(ML_HEADS)] + [yb_ref[...], yc], axis=1)
    merged = (merged * mw_ref[...]).astype(BF16)
    proj = _dot(merged, wo_ref[...])
    g1 = mod_ref[0, 0]
    x = x_ref[...].reshape(tm // BATCH, BATCH, D_MODEL) + g1[None] * proj.reshape(tm // BATCH, BATCH, D_MODEL)
    xo_ref[...] = x.reshape(tm, D_MODEL)
    ms = jnp.mean(x * x, axis=-1, keepdims=True)
    xn = (x * lax.rsqrt(ms + EPS)) * n2_ref[...][None]
    h2 = (xn * (1.0 + mod_ref[0, 2][None]) + mod_ref[0, 1][None]).reshape(tm, D_MODEL)
    _store_token_tiles(h2_ref, 0, h2)
    logits = _dot3(rw_ref[...], h2, nt=True) + rb_ref[...]
    eidx = lax.broadcasted_iota(I32, (N_EXPERTS, tm), 0)
    vals, ids, hots = [], [], []
    for _ in range(TOP_K):
        m = jnp.max(logits, axis=0, keepdims=True)
        sel = jnp.min(jnp.where(logits == m, eidx, N_EXPERTS), axis=0, keepdims=True)
        hot = eidx == sel
        logits = jnp.where(hot, -jnp.inf, logits)
        vals.append(m)
        ids.append(sel)
        hots.append(hot)
    ex = [jnp.exp(v - vals[0]) for v in vals]
    tot = ex[0] + ex[1] + ex[2] + ex[3]
    idx_ref[...] = jnp.concatenate(ids, axis=0)
    gate_ref[...] = jnp.concatenate([e / tot for e in ex], axis=0)
    selm = hots[0].astype(F32) + hots[1].astype(F32) + hots[2].astype(F32) + hots[3].astype(F32)
    r = lax.broadcasted_iota(I32, (tm, tm), 0)
    c = lax.broadcasted_iota(I32, (tm, tm), 1)
    before = _dot(selm.astype(BF16), (r < c).astype(BF16)) + cnt_sc[:, 0:1]
    ranks = [jnp.sum(jnp.where(hot, before, 0.0), axis=0, keepdims=True) for hot in hots]
    rank_ref[...] = jnp.concatenate(ranks, axis=0).astype(I32)
    cnt_sc[...] = cnt_sc[...] + jnp.sum(selm, axis=1, keepdims=True)
    cnt_ref[...] = cnt_sc[...]


def _outproj(x, ya, yb, ycc, ycl, mod, mw, wo, n2, rw, rb, tile0, n_ctx_rows):
    nt = x.shape[0]
    tm = ROW_TILE
    nct = n_ctx_rows // tm
    n_tiles = nt // tm - tile0
    n_out = n_tiles * tm
    rows = ycl.shape[0] // tm
    ycl4 = ycl.reshape(GRID_W, rows, BATCH, S5_WIDTH)
    full = lambda s: pl.BlockSpec(s, lambda i: tuple(0 for _ in s))
    return pl.pallas_call(
        functools.partial(_outproj_kernel, tile0=tile0, n_ctx_tiles=nct),
        out_shape=(
            jax.ShapeDtypeStruct((n_out, D_MODEL), F32),
            jax.ShapeDtypeStruct((n_out * SUBLANES, LANES), F32),
            jax.ShapeDtypeStruct((TOP_K, n_out), I32),
            jax.ShapeDtypeStruct((TOP_K, n_out), F32),
            jax.ShapeDtypeStruct((TOP_K, n_out), I32),
            jax.ShapeDtypeStruct((N_EXPERTS, LANES), F32),
        ),
        grid=(n_tiles,),
        in_specs=[
            pl.BlockSpec((tm, D_MODEL), lambda i: (i + tile0, 0)),
            pl.BlockSpec((ML_HEADS, tm, LANES), lambda i: (0, i + tile0, 0)),
            pl.BlockSpec((tm, RG_WIDTH), lambda i: (i + tile0, 0)),
            pl.BlockSpec((tm, S5_WIDTH), lambda i: (jnp.minimum(i + tile0, nct - 1), 0)),
            pl.BlockSpec((GRID_W, 1, BATCH, S5_WIDTH), lambda i: (0, jnp.maximum(i + tile0 - nct, 0), 0, 0)),
            pl.BlockSpec((1, 3, BATCH, D_MODEL), lambda i: (jnp.where(i + tile0 < nct, 0, 1), 0, 0, 0)),
            full((1, MERGED)),
            full((MERGED, D_MODEL)),
            full((1, D_MODEL)),
            full((N_EXPERTS, D_MODEL)),
            full((N_EXPERTS, 1)),
        ],
        out_specs=(
            pl.BlockSpec((tm, D_MODEL), lambda i: (i, 0)),
            pl.BlockSpec((tm * SUBLANES, LANES), lambda i: (i, 0)),
            pl.BlockSpec((TOP_K, tm), lambda i: (0, i)),
            pl.BlockSpec((TOP_K, tm), lambda i: (0, i)),
            pl.BlockSpec((TOP_K, tm), lambda i: (0, i)),
            full((N_EXPERTS, LANES)),
        ),
        scratch_shapes=[pltpu.VMEM((N_EXPERTS, LANES), F32)],
        compiler_params=_cparams("arbitrary"),
        name="outproj_router",
    )(x, ya, yb, ycc, ycl4, mod, mw, wo, n2, rw, rb)


def _token_tile(ref, tok):
    return ref.at[pl.ds(pl.multiple_of(tok * SUBLANES, SUBLANES), SUBLANES), :]


def _dispatch_kernel(pe_ref, pd_ref, nu_ref, pos_ref, h_ref, xs_hbm, zero_sc, sem, zsem, *, n_blocks):
    i = pl.program_id(0)
    tm = DISPATCH_TILE

    def zero_block(blk):
        start = pl.multiple_of(blk * (MOE_BM * SUBLANES), SUBLANES)
        return pltpu.make_async_copy(zero_sc, xs_hbm.at[pl.ds(start, MOE_BM * SUBLANES), :], zsem)

    @pl.when(i == 0)
    def _():
        zero_sc[...] = jnp.zeros(zero_sc.shape, F32)
        for e in range(N_EXPERTS):
            @pl.when(pd_ref[e] > 0)
            def _():
                zero_block(pe_ref[e] // MOE_BM - 1).start()
        for e in range(N_EXPERTS):
            @pl.when(pd_ref[e] > 0)
            def _():
                zero_block(0).wait()

        def fill(blk, carry):
            cp = zero_block(blk)
            cp.start()
            cp.wait()
            return carry

        lax.fori_loop(nu_ref[0], n_blocks, fill, 0)

    def issue(r, carry):
        for k in range(TOP_K):
            pltpu.make_async_copy(_token_tile(h_ref, r), _token_tile(xs_hbm, pos_ref[0, 0, k * tm + r]),
                                  sem).start(priority=k % DMA_PRIORITIES)
        return carry

    lax.fori_loop(0, tm, issue, 0, unroll=8)

    def drain(r, carry):
        for k in range(TOP_K):
            pltpu.make_async_copy(_token_tile(h_ref, 0), _token_tile(xs_hbm, 0), sem).wait()
        return carry

    lax.fori_loop(0, tm, drain, 0, unroll=8)


def _moe_dispatch(pad_ends, padded, n_used, pos, h2t, n_blocks):
    nt = pos.shape[1]
    tm = DISPATCH_TILE
    n_tiles = nt // tm
    posr = pos.reshape(TOP_K, n_tiles, tm).transpose(1, 0, 2).reshape(n_tiles, 1, TOP_K * tm)
    grid_spec = pltpu.PrefetchScalarGridSpec(
        num_scalar_prefetch=3,
        grid=(n_tiles,),
        in_specs=[
            pl.BlockSpec((1, 1, TOP_K * tm), lambda i, pe, pd, nu: (i, 0, 0), memory_space=pltpu.SMEM),
            pl.BlockSpec((tm * SUBLANES, LANES), lambda i, pe, pd, nu: (i, 0)),
        ],
        out_specs=pl.BlockSpec(memory_space=pl.ANY),
        scratch_shapes=[
            pltpu.VMEM((MOE_BM * SUBLANES, LANES), F32),
            pltpu.SemaphoreType.DMA(()),
            pltpu.SemaphoreType.DMA(()),
        ],
    )
    return pl.pallas_call(
        functools.partial(_dispatch_kernel, n_blocks=n_blocks),
        out_shape=jax.ShapeDtypeStruct((n_blocks * MOE_BM * SUBLANES, LANES), F32),
        grid_spec=grid_spec,
        compiler_params=_cparams("arbitrary"),
        name="moe_dispatch",
    )(pad_ends, padded, n_used, posr, h2t)


def _ffn_kernel(be_ref, nu_ref, x_ref, wgu_ref, bgu_ref, wd_ref, bd_ref, y_ref, wgu_sc, wd_sc):
    i = pl.program_id(0)
    changed = jnp.logical_or(i == 0, be_ref[i] != be_ref[jnp.maximum(i - 1, 0)])

    @pl.when(changed)
    def _():
        wgu_sc[...] = wgu_ref[0].astype(BF16)
        wd_sc[...] = wd_ref[0].astype(BF16)

    @pl.when(i < nu_ref[0])
    def _():
        x = _load_token_tiles(x_ref, 0, MOE_BM).astype(BF16)
        gu = _dot(x, wgu_sc[...]) + bgu_ref[0]
        gate = jnp.minimum(gu[:, :D_EXPERT], SWIGLU_LIMIT)
        up = jnp.clip(gu[:, D_EXPERT:], -SWIGLU_LIMIT, SWIGLU_LIMIT)
        glu = gate * _sigmoid(SWIGLU_ALPHA * gate)
        act = ((up + 1.0) * glu).astype(BF16)
        _store_token_tiles(y_ref, 0, _dot(act, wd_sc[...]) + bd_ref[0])

    @pl.when(i >= nu_ref[0])
    def _():
        y_ref[...] = jnp.zeros(y_ref.shape, F32)


def _moe_ffn(block_e, n_used, xs, wgu, bgu, wd, bd):
    rows = MOE_BM * SUBLANES
    n_blocks = xs.shape[0] // rows
    ne = wgu.shape[0]
    grid_spec = pltpu.PrefetchScalarGridSpec(
        num_scalar_prefetch=2,
        grid=(n_blocks,),
        in_specs=[
            pl.BlockSpec((rows, LANES), lambda i, be, nu: (jnp.minimum(i, nu[0] - 1), 0)),
            pl.BlockSpec((1, D_MODEL, 2 * D_EXPERT), lambda i, be, nu: (be[i], 0, 0)),
            pl.BlockSpec((1, 1, 2 * D_EXPERT), lambda i, be, nu: (be[i], 0, 0)),
            pl.BlockSpec((1, D_EXPERT, D_MODEL), lambda i, be, nu: (be[i], 0, 0)),
            pl.BlockSpec((1, 1, D_MODEL), lambda i, be, nu: (be[i], 0, 0)),
        ],
        out_specs=pl.BlockSpec((rows, LANES), lambda i, be, nu: (i, 0)),
        scratch_shapes=[
            pltpu.VMEM((D_MODEL, 2 * D_EXPERT), BF16),
            pltpu.VMEM((D_EXPERT, D_MODEL), BF16),
        ],
    )
    return pl.pallas_call(
        _ffn_kernel,
        out_shape=jax.ShapeDtypeStruct((n_blocks * rows, LANES), F32),
        grid_spec=grid_spec,
        compiler_params=_cparams("arbitrary"),
        name="moe_ffn",
    )(block_e, n_used, xs, wgu, bgu.reshape(ne, 1, -1), wd, bd.reshape(ne, 1, -1))


def _combine_kernel(pos_ref, x_ref, gate_ref, mod_ref, fw_ref, ys_hbm, out_ref, buf, sem, *, final_norm):
    tm = x_ref.shape[0]
    n = TOP_K * tm

    def issue(r2, carry):
        for p in range(DMA_PRIORITIES):
            r = r2 * DMA_PRIORITIES + p
            pltpu.make_async_copy(_token_tile(ys_hbm, pos_ref[0, 0, r]), _token_tile(buf, r), sem).start(priority=p)
        return carry

    lax.fori_loop(0, n // DMA_PRIORITIES, issue, 0, unroll=8)

    def drain(r, carry):
        pltpu.make_async_copy(_token_tile(ys_hbm, 0), _token_tile(buf, 0), sem).wait()
        return carry

    lax.fori_loop(0, n, drain, 0, unroll=8)
    g = gate_ref[...]
    f = g[:, 0:1] * _load_token_tiles(buf, 0, tm)
    for k in range(1, TOP_K):
        f = f + g[:, k:k + 1] * _load_token_tiles(buf, k * tm, tm)
    x = x_ref[...].reshape(tm // BATCH, BATCH, D_MODEL) + mod_ref[0][None] * f.reshape(tm // BATCH, BATCH, D_MODEL)
    if final_norm:
        ms = jnp.mean(x * x, axis=-1, keepdims=True)
        x = (x * lax.rsqrt(ms + EPS)) * fw_ref[...][None]
    out_ref[...] = x.reshape(tm, D_MODEL)


def _moe_combine(pos, x, gates_t, mod, fw, ys, n_ctx_rows, final_norm):
    nt = x.shape[0]
    tm = COMBINE_TILE
    n_tiles = nt // tm
    nct = n_ctx_rows // tm
    posr = pos.reshape(TOP_K, n_tiles, tm).transpose(1, 0, 2).reshape(n_tiles, 1, TOP_K * tm)
    return pl.pallas_call(
        functools.partial(_combine_kernel, final_norm=final_norm),
        out_shape=jax.ShapeDtypeStruct((nt, D_MODEL), F32),
        grid=(n_tiles,),
        in_specs=[
            pl.BlockSpec((1, 1, TOP_K * tm), lambda i: (i, 0, 0), memory_space=pltpu.SMEM),
            pl.BlockSpec((tm, D_MODEL), lambda i: (i, 0)),
            pl.BlockSpec((tm, TOP_K), lambda i: (i, 0)),
            pl.BlockSpec((1, BATCH, D_MODEL), lambda i: (jnp.where(i < nct, 0, 1), 0, 0)),
            pl.BlockSpec((1, D_MODEL), lambda i: (0, 0)),
            pl.BlockSpec(memory_space=pl.ANY),
        ],
        out_specs=pl.BlockSpec((tm, D_MODEL), lambda i: (i, 0)),
        scratch_shapes=[pltpu.VMEM((TOP_K * tm * SUBLANES, LANES), F32), pltpu.SemaphoreType.DMA(())],
        compiler_params=_cparams("arbitrary"),
        name="moe_combine",
    )(posr, x, gates_t, mod, fw, ys)


def _moe(h2t, idx, gates, rank, counts, x, g2, fw, layer, wgu, bgu, wd, bd, n_ctx_rows, final_norm):
    nt = idx.shape[1]
    bm = MOE_BM
    n_blocks = (nt * TOP_K) // bm + N_EXPERTS
    cnt = counts[:, 0].astype(I32)
    padded = (cnt + bm - 1) // bm * bm
    pad_ends = jnp.cumsum(padded)
    pad_starts = pad_ends - padded
    experts = jnp.arange(N_EXPERTS, dtype=I32)
    pos = rank + jnp.sum(jnp.where(idx[None] == experts[:, None, None], pad_starts[:, None, None], 0), axis=0)
    block_start = jnp.arange(n_blocks, dtype=I32) * bm
    block_e = jnp.minimum(jnp.sum((pad_ends[None, :] <= block_start[:, None]).astype(I32), axis=1), N_EXPERTS - 1)
    n_used = (pad_ends[-1] // bm).astype(I32).reshape(1)
    xs = _moe_dispatch(pad_ends.astype(I32), padded, n_used, pos, h2t, n_blocks)
    ys = _moe_ffn(block_e + layer * N_EXPERTS, n_used, xs, wgu, bgu, wd, bd)
    return _moe_combine(pos, x, gates.T, g2, fw, ys, n_ctx_rows, final_norm)


def _pad_heads(a):
    lead = a.shape[:-1]
    a = a.reshape(*lead, 4 * ML_HEADS, ML_HEAD_DIM)
    a = jnp.pad(a, [(0, 0)] * len(lead) + [(0, 0), (0, LANES - ML_HEAD_DIM)])
    return a.reshape(*lead, 4 * ML_HEADS * LANES)


def _kvqo(a):
    w = ML_HEADS * ML_HEAD_DIM
    return jnp.concatenate([a[..., w:2 * w], a[..., 2 * w:3 * w], a[..., :w], a[..., 3 * w:4 * w]], axis=-1)


def _block_diag(blocks):
    return jax.scipy.linalg.block_diag(*[blocks[g] for g in range(blocks.shape[0])])


def _s5_discretise(lam_re, lam_im, log_dt, b_re, b_im):
    dt = jnp.exp(log_dt)[:, None]
    mag = jnp.exp(lam_re * dt)
    ar, ai = mag * jnp.cos(lam_im * dt), mag * jnp.sin(lam_im * dt)
    den = lam_re * lam_re + lam_im * lam_im
    cr = ((ar - 1.0) * lam_re + ai * lam_im) / den
    ci = (ai * lam_re - (ar - 1.0) * lam_im) / den
    bbr = cr[..., None] * b_re - ci[..., None] * b_im
    bbi = cr[..., None] * b_im + ci[..., None] * b_re
    return ar, ai, bbr, bbi


def kernel(x, c, ctx, c_ctx, ada_w, ada_b, norm1_w, w_in, b_in, rg_conv_w, rg_conv_b, rg_wa, rg_ba, rg_wx, rg_bx, rg_lambda, s5_lambda_re, s5_lambda_im, s5_log_dt, s5_b_re, s5_b_im, s5_c_re, s5_c_im, s5_d, s5_glu_w, s5_glu_b, mix_norm_w, w_out, norm2_w, router_w, router_b, moe_w_gate_up, moe_b_gate_up, moe_w_down, moe_b_down, final_norm_w):
    bsz, seq, dm = x.shape
    sc = ctx.shape[1]
    assert bsz == BATCH and dm == D_MODEL and seq % GRID_W == 0
    n_ctx_rows = sc * BATCH
    n_lat_rows = seq * BATCH
    assert n_ctx_rows % CHUNK_ROWS == 0 and n_lat_rows % CHUNK_ROWS == 0 and GRID_W * BATCH == ROW_TILE
    n_ctx_chunks = n_ctx_rows // CHUNK_ROWS
    depth = ada_w.shape[0]

    xs = jnp.concatenate([ctx.transpose(1, 0, 2).reshape(n_ctx_rows, dm),
                          x.transpose(1, 0, 2).reshape(n_lat_rows, dm)], axis=0)

    c_rows = jnp.concatenate([c, c_ctx[None], jnp.zeros((16 - bsz - 1, dm), F32)], axis=0)
    mods = _modulation(c_rows, ada_w, ada_b)
    mods = mods.reshape(depth, 16, 6, dm)
    mod_lat = mods[:, :bsz].transpose(0, 2, 1, 3)
    mod_ctx = jnp.broadcast_to(mods[:, bsz][:, :, None, :], mod_lat.shape)
    mod = jnp.stack([mod_ctx, mod_lat], axis=1)

    for l in range(depth):
        with_ctx = l < depth - 1
        g0 = 4 * ML_HEADS * ML_HEAD_DIM
        ng = 2 * ML_HEADS
        wa_cols = _kvqo(w_in[l][:, :g0])
        ba_cols = _kvqo(b_in[l][:g0])
        kscale = jnp.ones((4, ML_HEADS * LANES), F32).at[0].set(ML_HEAD_DIM ** -0.5).reshape(-1)
        ones_col = jnp.zeros((4, ML_HEADS, LANES), F32).at[1, :, ML_HEAD_DIM].set(1.0).reshape(-1)
        w_cat = jnp.concatenate([
            _pad_heads(wa_cols) * kscale,
            jnp.pad(w_in[l][:, g0:g0 + ng], ((0, 0), (0, LANES - ng))),
            jnp.pad(w_in[l][:, g0 + ng:g0 + 2 * ng], ((0, 0), (0, LANES - ng))),
            w_in[l][:, g0 + 2 * ng:]], axis=1).astype(BF16)
        b_cat = jnp.concatenate([
            _pad_heads(ba_cols) * kscale + ones_col,
            jnp.pad(b_in[l][g0:g0 + ng], (0, LANES - ng)),
            jnp.pad(b_in[l][g0 + ng:g0 + 2 * ng], (0, LANES - ng)),
            b_in[l][g0 + 2 * ng:]])[None]
        wg = [jnp.concatenate([_block_diag(rg_wa[l, d]), _block_diag(rg_wx[l, d])], axis=1).astype(BF16)
              for d in range(2)]
        bg = [jnp.concatenate([rg_ba[l, d], rg_bx[l, d]])[None] for d in range(2)]
        s5p = []
        for d in range(2):
            ar, ai, bbr, bbi = _s5_discretise(s5_lambda_re[l, d], s5_lambda_im[l, d], s5_log_dt[l, d],
                                              s5_b_re[l], s5_b_im[l])
            a = jnp.stack([ar.reshape(-1), ai.reshape(-1)])
            bd = jnp.concatenate([_block_diag(bbr.transpose(0, 2, 1)), _block_diag(bbi.transpose(0, 2, 1))],
                                 axis=1).astype(BF16)
            cd = jnp.concatenate([_block_diag(s5_c_re[l].transpose(0, 2, 1)),
                                  -_block_diag(s5_c_im[l].transpose(0, 2, 1))], axis=0).astype(BF16)
            s5p.append((a, bd, cd))
        mw = jnp.concatenate([_pad_heads(jnp.tile(mix_norm_w[l][:ML_HEADS * ML_HEAD_DIM], 4))[:ML_HEADS * LANES],
                              mix_norm_w[l][ML_HEADS * ML_HEAD_DIM:]])[None]
        wo_a = w_out[l][:ML_HEADS * ML_HEAD_DIM].reshape(ML_HEADS, ML_HEAD_DIM, dm)
        wo_a = jnp.pad(wo_a, ((0, 0), (0, LANES - ML_HEAD_DIM), (0, 0))).reshape(ML_HEADS * LANES, dm)
        wo = jnp.concatenate([wo_a, w_out[l][ML_HEADS * ML_HEAD_DIM:]], axis=0).astype(BF16)

        pa, pb, pcc, pcl = _inproj(xs, mod[l, :, 0:2], norm1_w[l][None], w_cat, b_cat, n_ctx_rows)
        pcl = pcl.reshape(n_lat_rows, S5_WIDTH)
        c0f, m0f = _mlstm_states(pa, 0, n_ctx_chunks)
        c0b, m0b = _mlstm_states(pa, 1, n_ctx_chunks)
        ya = _mlstm_outputs(pa, c0f, m0f, c0b, m0b)
        rgb = _rglru_pass(pb, None, rg_conv_w[l], rg_conv_b[l][None], wg[1], bg[1], rg_lambda[l, 1][None], 1, n_ctx_chunks)
        yb = _rglru_pass(pb, rgb, rg_conv_w[l], rg_conv_b[l][None], wg[0], bg[0], rg_lambda[l, 0][None], 0, n_ctx_chunks)
        zero_state = jnp.zeros((BATCH, 2 * S5_NSTATE), F32)
        a1, bd1, cd1 = s5p[1]
        a0, bd0, cd0 = s5p[0]
        dsk, gw, gb = s5_d[l][None], s5_glu_w[l].astype(BF16), s5_glu_b[l][None]
        ycb_c, st = _s5_pass(pcc, None, zero_state, a1, bd1, cd1, None, None, None, 1)
        ycb_l, _ = _s5_pass(pcl, None, st, a1, bd1, cd1, None, None, None, 1)
        yc_c, st = _s5_pass(pcc, ycb_c, zero_state, a0, bd0, cd0, dsk, gw, gb, 0)
        yc_l, _ = _s5_pass(pcl, ycb_l, st, a0, bd0, cd0, dsk, gw, gb, 0)

        tile0 = 0 if with_ctx else n_ctx_rows // ROW_TILE
        xo, h2, idx, gates, rank, counts = _outproj(
            xs, ya, yb, yc_c, yc_l, mod[l, :, 2:5], mw, wo, norm2_w[l][None], router_w[l].T,
            router_b[l][:, None], tile0, n_ctx_rows)

        xs_new = _moe(h2, idx, gates, rank, counts, xo, mod[l, :, 5], final_norm_w[None], l,
                      moe_w_gate_up.reshape(-1, dm, 2 * D_EXPERT), moe_b_gate_up.reshape(-1, 2 * D_EXPERT),
                      moe_w_down.reshape(-1, D_EXPERT, dm), moe_b_down.reshape(-1, dm),
                      n_ctx_rows if with_ctx else 0, l == depth - 1)
        xs = xs_new

    out = xs.reshape(seq, bsz, dm).transpose(1, 0, 2)
    return out
```

```python
import functools
import math

import jax
import jax.numpy as jnp
from jax import lax
from jax.experimental import pallas as pl
from jax.experimental.pallas import tpu as pltpu

F32 = jnp.float32
BF16 = jnp.bfloat16
I32 = jnp.int32

D_MODEL = 1024
BATCH = 8
DEPTH = 4
GRID_W = 64
ML_HEADS = 4
ML_HEAD_DIM = 96
ML_CHUNK = 128
ML_M_INIT = -1e30
RG_BLOCKS = 6
RG_BLOCK_DIM = 64
RG_WIDTH = RG_BLOCKS * RG_BLOCK_DIM
RG_CONV = 4
RG_C = 8.0
S5_GROUPS = 16
S5_GROUP = 16
S5_WIDTH = S5_GROUPS * S5_GROUP
S5_STATE = 64
S5_NSTATE = S5_GROUPS * S5_STATE
N_EXPERTS = 32
TOP_K = 4
D_EXPERT = D_MODEL
SWIGLU_LIMIT = 7.0
SWIGLU_ALPHA = 1.702
EPS = 1e-6

LANES = 128
SUBLANES = 8

ML_K, ML_V, ML_Q, ML_O = 0, ML_HEADS, 2 * ML_HEADS, 3 * ML_HEADS
ML_IG = 4 * ML_HEADS
ML_FG = ML_IG + 1
ML_SLABS = ML_FG + 1
ML_STATE_SLABS = 2 * ML_HEADS
A_PAD = ML_SLABS * LANES
MERGED = ML_HEADS * LANES + RG_WIDTH + S5_WIDTH

ROW_TILE = 512
T_CHUNK = 128
CHUNK_ROWS = T_CHUNK * BATCH
MOE_BM = 512
DMA_PRIORITIES = 2
RUN_BITS = ROW_TILE.bit_length()
VMEM_LIMIT = 56 * 1024 * 1024


def _cparams(*sem):
    return pltpu.CompilerParams(dimension_semantics=sem, vmem_limit_bytes=VMEM_LIMIT)


def _sigmoid(x):
    return 1.0 / (1.0 + jnp.exp(-x))


def _log_sigmoid(x):
    return jnp.minimum(x, 0.0) - jnp.log1p(jnp.exp(-jnp.abs(x)))


def _softplus(x):
    return jnp.maximum(x, 0.0) + jnp.log1p(jnp.exp(-jnp.abs(x)))


def _gelu(x):
    return 0.5 * x * (1.0 + jnp.tanh(0.7978845608028654 * (x + 0.044715 * (x * x * x))))


def _dot(a, b):
    return jnp.dot(a, b, preferred_element_type=F32)


def _dot_nt(a, b):
    return lax.dot_general(a, b, (((1,), (1,)), ((), ())), preferred_element_type=F32)


def _split(a):
    hi = a.astype(BF16)
    lo = (a - hi.astype(F32)).astype(BF16)
    return hi, lo


def _dot_lx(a_exact, b):
    hi, lo = _split(b)
    return _dot(a_exact, hi) + _dot(a_exact, lo)


def _dot_xr(a, b_exact):
    hi, lo = _split(a)
    return _dot(hi, b_exact) + _dot(lo, b_exact)


def _dot3(a, b, nt=False):
    ah, al = _split(a)
    bh, bl = _split(b)
    d = _dot_nt if nt else _dot
    return d(ah, bh) + (d(ah, bl) + d(al, bh))


def _group_ones(n, shift):
    r = lax.broadcasted_iota(I32, (n, n), 0)
    c = lax.broadcasted_iota(I32, (n, n), 1)
    return (lax.shift_right_logical(r, shift) == lax.shift_right_logical(c, shift)).astype(BF16)


def _store_token_tiles(ref, tok0, val):
    n = val.shape[0]
    for s in range(SUBLANES):
        ref[pl.ds(tok0 * SUBLANES + s, n, stride=SUBLANES), :] = val[:, s * LANES:(s + 1) * LANES]


def _load_token_tiles(ref, tok0, n):
    return jnp.concatenate(
        [ref[pl.ds(tok0 * SUBLANES + s, n, stride=SUBLANES), :] for s in range(SUBLANES)], axis=1)


def _mod_kernel(c_ref, w_ref, b_ref, o_ref):
    c = c_ref[...]
    s = c * _sigmoid(c)
    o_ref[0] = _dot3(s, w_ref[0]) + b_ref[0]


def _modulation(c_rows, ada_w, ada_b):
    depth, d, n = ada_w.shape
    tn = 1536
    return pl.pallas_call(
        _mod_kernel,
        out_shape=jax.ShapeDtypeStruct((depth, 16, n), F32),
        grid=(depth, n // tn),
        in_specs=[
            pl.BlockSpec((16, d), lambda l, j: (0, 0)),
            pl.BlockSpec((1, d, tn), lambda l, j: (l, 0, j)),
            pl.BlockSpec((1, 1, tn), lambda l, j: (l, 0, j)),
        ],
        out_specs=pl.BlockSpec((1, 16, tn), lambda l, j: (l, 0, j)),
        compiler_params=_cparams("arbitrary", "arbitrary"),
        name="adaln_mod",
    )(c_rows, ada_w, ada_b.reshape(depth, 1, n))


def _inproj_kernel(x_ref, mod_ref, nw_ref, w_ref, b_ref, pa_ref, pb_ref, pcc_ref, pcl_ref, *, n_ctx_tiles):
    i = pl.program_id(0)
    x = x_ref[...]
    tm = x.shape[0]
    ms = jnp.mean(x * x, axis=-1, keepdims=True)
    xn = (x * lax.rsqrt(ms + EPS)) * nw_ref[...]
    xn = xn.reshape(tm // BATCH, BATCH, D_MODEL)
    h = xn * (1.0 + mod_ref[0, 1][None]) + mod_ref[0, 0][None]
    h = h.reshape(tm, D_MODEL).astype(BF16)
    for j in range(0, ML_SLABS, 2):
        p = _dot(h, w_ref[:, j * LANES:(j + 2) * LANES]) + b_ref[:, j * LANES:(j + 2) * LANES]
        pa_ref[j] = p[:, :LANES]
        pa_ref[j + 1] = p[:, LANES:]
    c0 = A_PAD
    pb_ref[...] = _dot(h, w_ref[:, c0:c0 + 2 * RG_WIDTH]) + b_ref[:, c0:c0 + 2 * RG_WIDTH]
    c0 = A_PAD + 2 * RG_WIDTH
    pc = _dot(h, w_ref[:, c0:c0 + S5_WIDTH]) + b_ref[:, c0:c0 + S5_WIDTH]

    @pl.when(i < n_ctx_tiles)
    def _():
        pcc_ref[...] = pc

    @pl.when(i >= n_ctx_tiles)
    def _():
        pcl_ref[...] = pc.reshape(GRID_W, 1, BATCH, S5_WIDTH)


def _inproj(x, mod, nw, w, b, n_ctx_rows):
    nt = x.shape[0]
    tm = ROW_TILE
    nct = n_ctx_rows // tm
    n_lat_rows = nt - n_ctx_rows
    rows = n_lat_rows // tm
    ncols = w.shape[1]
    return pl.pallas_call(
        functools.partial(_inproj_kernel, n_ctx_tiles=nct),
        out_shape=(
            jax.ShapeDtypeStruct((ML_SLABS, nt, LANES), F32),
            jax.ShapeDtypeStruct((nt, 2 * RG_WIDTH), F32),
            jax.ShapeDtypeStruct((n_ctx_rows, S5_WIDTH), F32),
            jax.ShapeDtypeStruct((GRID_W, rows, BATCH, S5_WIDTH), F32),
        ),
        grid=(nt // tm,),
        in_specs=[
            pl.BlockSpec((tm, D_MODEL), lambda i: (i, 0)),
            pl.BlockSpec((1, 2, BATCH, D_MODEL), lambda i: (jnp.where(i < nct, 0, 1), 0, 0, 0)),
            pl.BlockSpec((1, D_MODEL), lambda i: (0, 0)),
            pl.BlockSpec((D_MODEL, ncols), lambda i: (0, 0)),
            pl.BlockSpec((1, ncols), lambda i: (0, 0)),
        ],
        out_specs=(
            pl.BlockSpec((ML_SLABS, tm, LANES), lambda i: (0, i, 0)),
            pl.BlockSpec((tm, 2 * RG_WIDTH), lambda i: (i, 0)),
            pl.BlockSpec((tm, S5_WIDTH), lambda i: (jnp.minimum(i, nct - 1), 0)),
            pl.BlockSpec((GRID_W, 1, BATCH, S5_WIDTH), lambda i: (0, jnp.maximum(i - nct, 0), 0, 0)),
        ),
        compiler_params=_cparams("arbitrary"),
        name="inproj",
    )(x, mod, nw, w, b)


def _mlstm_gates(ig, fg):
    L = ML_CHUNK
    row = lax.broadcasted_iota(I32, (L, L), 0)
    col = lax.broadcasted_iota(I32, (L, L), 1)
    lf = _log_sigmoid(fg)
    b_fwd = _dot_lx((row >= col).astype(BF16), lf)
    b_tot = _dot_lx(jnp.ones((L, L), BF16), lf)
    b_bwd = b_tot - b_fwd + lf
    bcol = jnp.where(col < ML_HEADS, b_fwd, b_bwd)
    return ig - bcol, bcol, b_tot


def _mlstm_state_kernel(kv_ref, g_ref, c0_ref, m0_ref, c_sc, m_sc, *, direction):
    i = pl.program_id(0)
    L = ML_CHUNK

    @pl.when(i == 0)
    def _():
        c_sc[...] = jnp.zeros(c_sc.shape, F32)
        m_sc[...] = jnp.full(m_sc.shape, ML_M_INIT, F32)

    def per_batch(b, carry):
        rows = pl.ds(b, L, stride=BATCH)
        z, _, b_tot = _mlstm_gates(g_ref[0, rows, :], g_ref[1, rows, :])
        z_t = z.T
        for h in range(ML_HEADS):
            j = ML_HEADS * direction + h
            idx = b * ML_HEADS + h
            k = kv_ref[ML_K + h, rows, :].astype(BF16)
            v_t = kv_ref[ML_V + h, rows, :].T
            c0 = c_sc[idx]
            m0_tile = m_sc[idx]
            c0_ref[0, idx] = c0.astype(BF16)
            m0_ref[0, idx] = m0_tile
            m0 = m0_tile[0:1, :]
            bt = jnp.broadcast_to(b_tot[0:1, j:j + 1], (1, LANES))
            w = bt + z_t[j:j + 1, :]
            mloc = jnp.broadcast_to(jnp.max(w, axis=1, keepdims=True), (1, LANES))
            cloc = _dot((v_t * jnp.exp(w - mloc)).astype(BF16), k)
            mnew = jnp.maximum(bt + m0, mloc)
            a = jnp.exp(bt + m0 - mnew)
            sc = jnp.exp(mloc - mnew)
            c_sc[idx] = a * c0 + sc * cloc
            m_sc[idx] = jnp.broadcast_to(mnew, (SUBLANES, LANES))
        return carry

    lax.fori_loop(0, BATCH, per_batch, 0, unroll=2)


def _mlstm_out_kernel(pa_ref, c0f_ref, m0f_ref, c0b_ref, m0b_ref, out_ref):
    L = ML_CHUNK
    row = lax.broadcasted_iota(I32, (L, L), 0)
    col = lax.broadcasted_iota(I32, (L, L), 1)
    masks = (row <= col, row >= col)
    head_row = row < ML_HEAD_DIM
    m0_refs = (m0f_ref, m0b_ref)

    def per_batch(b, carry):
        rows = pl.ds(b, L, stride=BATCH)
        z, bcol, _ = _mlstm_gates(pa_ref[ML_IG, rows, :], pa_ref[ML_FG, rows, :])
        bcol_t = bcol.T
        for h in range(ML_HEADS):
            idx = b * ML_HEADS + h
            q = pa_ref[ML_Q + h, rows, :].astype(BF16)
            k = pa_ref[ML_K + h, rows, :].astype(BF16)
            v_t = pa_ref[ML_V + h, rows, :].T.astype(BF16)
            s_t = _dot_nt(k, q)
            nc_t = _dot_nt(jnp.concatenate([c0f_ref[0, idx], c0b_ref[0, idx]], axis=0), q)
            w_t = None
            h_t = None
            for d in range(2):
                j = ML_HEADS * d + h
                m0 = m0_refs[d][0, idx][0:1, :]
                r = jnp.where(masks[d], z[:, j:j + 1], -jnp.inf)
                u = jnp.maximum(jnp.max(r, axis=0, keepdims=True), m0)
                sqk = s_t * jnp.exp(r - u)
                inter = jnp.exp(m0 - u)
                ncd = nc_t[d * LANES:(d + 1) * LANES, :]
                den = jnp.sum(sqk, axis=0, keepdims=True) + inter * ncd[ML_HEAD_DIM:ML_HEAD_DIM + 1, :]
                inv = 1.0 / jnp.maximum(jnp.abs(den), jnp.exp(-(bcol_t[j:j + 1, :] + u)))
                w_t = sqk * inv if w_t is None else w_t + sqk * inv
                h_t = (inter * inv) * ncd if h_t is None else h_t + (inter * inv) * ncd
            h_t = jnp.where(head_row, h_t + _dot(v_t, w_t.astype(BF16)), 0.0)
            ms = jnp.sum(h_t * h_t, axis=0, keepdims=True) * (1.0 / ML_HEAD_DIM)
            o_t = pa_ref[ML_O + h, rows, :].T
            out_ref[h, rows, :] = (_sigmoid(o_t) * (h_t * lax.rsqrt(ms + EPS))).T
        return carry

    lax.fori_loop(0, BATCH, per_batch, 0, unroll=2)


def _chunk_order(i, direction, n_ctx, n_all):
    if direction == 0:
        return i
    return jnp.where(i < n_ctx, n_ctx - 1 - i, n_all - 1 + n_ctx - i)


def _mlstm_states(pa, direction, n_ctx):
    nt = pa.shape[1]
    n_all = nt // CHUNK_ROWS
    nbh = BATCH * ML_HEADS
    order = lambda i: _chunk_order(i, direction, n_ctx, n_all)
    return pl.pallas_call(
        functools.partial(_mlstm_state_kernel, direction=direction),
        out_shape=(jax.ShapeDtypeStruct((n_all, nbh, ML_CHUNK, LANES), BF16),
                   jax.ShapeDtypeStruct((n_all, nbh, SUBLANES, LANES), F32)),
        grid=(n_all,),
        in_specs=[
            pl.BlockSpec((ML_STATE_SLABS, CHUNK_ROWS, LANES), lambda i: (0, order(i), 0)),
            pl.BlockSpec((2, CHUNK_ROWS, LANES), lambda i: (ML_IG // 2, order(i), 0)),
        ],
        out_specs=(pl.BlockSpec((1, nbh, ML_CHUNK, LANES), lambda i: (order(i), 0, 0, 0)),
                   pl.BlockSpec((1, nbh, SUBLANES, LANES), lambda i: (order(i), 0, 0, 0))),
        scratch_shapes=[
            pltpu.VMEM((nbh, ML_CHUNK, LANES), F32),
            pltpu.VMEM((nbh, SUBLANES, LANES), F32),
        ],
        compiler_params=_cparams("arbitrary"),
        name="mlstm_state_fwd" if direction == 0 else "mlstm_state_bwd",
    )(pa, pa)


def _mlstm_outputs(pa, c0f, m0f, c0b, m0b):
    nt = pa.shape[1]
    n_all = nt // CHUNK_ROWS
    nbh = BATCH * ML_HEADS
    cspec = pl.BlockSpec((1, nbh, ML_CHUNK, LANES), lambda i: (i, 0, 0, 0))
    mspec = pl.BlockSpec((1, nbh, SUBLANES, LANES), lambda i: (i, 0, 0, 0))
    return pl.pallas_call(
        _mlstm_out_kernel,
        out_shape=jax.ShapeDtypeStruct((ML_HEADS, nt, LANES), F32),
        grid=(n_all,),
        in_specs=[pl.BlockSpec((ML_SLABS, CHUNK_ROWS, LANES), lambda i: (0, i, 0)), cspec, mspec, cspec, mspec],
        out_specs=pl.BlockSpec((ML_HEADS, CHUNK_ROWS, LANES), lambda i: (0, i, 0)),
        compiler_params=_cparams("arbitrary"),
        name="mlstm_out",
    )(pa, c0f, m0f, c0b, m0b)


def _rglru_kernel(*refs, direction, final, n_ctx, n_all):
    if final:
        cur_ref, prev_ref, next_ref, hb_ref, cw_ref, cb_ref, wg_ref, bg_ref, lam_ref, out_ref, a_sc, b_sc, h_sc = refs
    else:
        cur_ref, prev_ref, next_ref, cw_ref, cb_ref, wg_ref, bg_ref, lam_ref, out_ref, a_sc, b_sc, h_sc = refs
        hb_ref = None
    i = pl.program_id(0)
    c = _chunk_order(i, direction, n_ctx, n_all)
    w = RG_WIDTH

    @pl.when(i == 0)
    def _():
        h_sc[...] = jnp.zeros(h_sc.shape, F32)

    seg_first = jnp.logical_or(c == 0, c == n_ctx)
    seg_last = jnp.logical_or(c == n_ctx - 1, c == n_all - 1)
    cur = cur_ref[:, :w]
    prev = jnp.where(seg_first, 0.0, prev_ref[:, :w])
    nxt = jnp.where(seg_last, 0.0, next_ref[:, :w])
    xc = jnp.concatenate([prev, cur, nxt], axis=0)
    n = CHUNK_ROWS
    xconv = cb_ref[...] + cw_ref[0:1, :] * xc[0:n]
    for j in range(1, RG_CONV):
        xconv = xconv + cw_ref[j:j + 1, :] * xc[j * BATCH:j * BATCH + n]
    z = _dot(xconv.astype(BF16), wg_ref[...]) + bg_ref[...]
    r = _sigmoid(z[:, :w])
    ig = _sigmoid(z[:, w:])
    log_a = (-RG_C * _softplus(-lam_ref[...])) * r
    a = jnp.exp(log_a)
    a_sc[...] = a
    b_sc[...] = jnp.sqrt(-jnp.tanh(log_a) * (a * a + 1.0)) * (ig * xconv)

    def step(t, h):
        tt = t if direction == 0 else T_CHUNK - 1 - t
        rows = pl.ds(pl.multiple_of(tt * BATCH, BATCH), BATCH)
        h = a_sc[rows, :] * h + b_sc[rows, :]
        b_sc[rows, :] = h
        return h

    h_sc[...] = lax.fori_loop(0, T_CHUNK, step, h_sc[...], unroll=8)
    hs = b_sc[...]
    if final:
        ht = hs + hb_ref[...]
        ss = _dot_xr(ht * ht, _group_ones(w, 6)) * (1.0 / RG_BLOCK_DIM)
        hs = _gelu(cur_ref[:, w:]) * (ht * lax.rsqrt(ss + EPS))
    out_ref[...] = hs


def _rglru_pass(pb, hb, cw, cb, wg, bg, lam, direction, n_ctx):
    nt = pb.shape[0]
    n_all = nt // CHUNK_ROWS
    final = hb is not None
    w2 = 2 * RG_WIDTH
    order = lambda i: _chunk_order(i, direction, n_ctx, n_all)
    halo_p = CHUNK_ROWS // (2 * BATCH)
    halo_n = CHUNK_ROWS // BATCH
    in_specs = [
        pl.BlockSpec((CHUNK_ROWS, w2), lambda i: (order(i), 0)),
        pl.BlockSpec((2 * BATCH, w2), lambda i: (jnp.maximum(order(i) * halo_p - 1, 0), 0)),
        pl.BlockSpec((BATCH, w2), lambda i: (jnp.minimum((order(i) + 1) * halo_n, nt // BATCH - 1), 0)),
    ]
    args = [pb, pb, pb]
    if final:
        in_specs.append(pl.BlockSpec((CHUNK_ROWS, RG_WIDTH), lambda i: (order(i), 0)))
        args.append(hb)
    full = lambda s: pl.BlockSpec(s, lambda i: tuple(0 for _ in s))
    in_specs += [full((RG_CONV, RG_WIDTH)), full((1, RG_WIDTH)), full((RG_WIDTH, w2)), full((1, w2)), full((1, RG_WIDTH))]
    args += [cw, cb, wg, bg, lam]
    return pl.pallas_call(
        functools.partial(_rglru_kernel, direction=direction, final=final, n_ctx=n_ctx, n_all=n_all),
        out_shape=jax.ShapeDtypeStruct((nt, RG_WIDTH), F32),
        grid=(n_all,),
        in_specs=in_specs,
        out_specs=pl.BlockSpec((CHUNK_ROWS, RG_WIDTH), lambda i: (order(i), 0)),
        scratch_shapes=[
            pltpu.VMEM((CHUNK_ROWS, RG_WIDTH), F32),
            pltpu.VMEM((CHUNK_ROWS, RG_WIDTH), F32),
            pltpu.VMEM((BATCH, RG_WIDTH), F32),
        ],
        compiler_params=_cparams("arbitrary"),
        name="rglru_fwd" if final else "rglru_bwd",
    )(*args)


def _s5_kernel(*refs, direction, final):
    if final:
        (u_ref, yb_ref, h0_ref, a_ref, bd_ref, cd_ref, dsk_ref, gw_ref, gb_ref,
         out_ref, hN_ref, v_sc, h_sc) = refs
    else:
        u_ref, h0_ref, a_ref, bd_ref, cd_ref, out_ref, hN_ref, v_sc, h_sc = refs
    i = pl.program_id(0)
    n = S5_NSTATE

    @pl.when(i == 0)
    def _():
        h_sc[...] = h0_ref[...]

    u = u_ref[...]
    v_sc[...] = _dot(u.astype(BF16), bd_ref[...])
    ar = jnp.broadcast_to(a_ref[0:1, :], (BATCH, n))
    ai = jnp.broadcast_to(a_ref[1:2, :], (BATCH, n))

    def step(t, carry):
        hr, hi = carry
        tt = t if direction == 0 else T_CHUNK - 1 - t
        rows = pl.ds(pl.multiple_of(tt * BATCH, BATCH), BATCH)
        nr = ar * hr - ai * hi + v_sc[rows, :n]
        ni = ar * hi + ai * hr + v_sc[rows, n:]
        v_sc[rows, :n] = nr
        v_sc[rows, n:] = ni
        return nr, ni

    hr, hi = lax.fori_loop(0, T_CHUNK, step, (h_sc[:, :n], h_sc[:, n:]))
    h_sc[:, :n] = hr
    h_sc[:, n:] = hi
    hN_ref[...] = h_sc[...]
    y = _dot(v_sc[...].astype(BF16), cd_ref[...])
    if final:
        y = y + yb_ref[...] + dsk_ref[...] * u
        g = _gelu(y)
        o = g * _sigmoid(_dot(g.astype(BF16), gw_ref[...]) + gb_ref[...])
        ss = _dot_xr(o * o, _group_ones(S5_WIDTH, 4)) * (1.0 / S5_GROUP)
        y = o * lax.rsqrt(ss + EPS)
    out_ref[...] = y


def _s5_pass(u, yb, h0, a, bd, cd, dsk, gw, gb, direction):
    nt = u.shape[0]
    n_chunks = nt // CHUNK_ROWS
    final = yb is not None
    order = (lambda i: i) if direction == 0 else (lambda i: n_chunks - 1 - i)
    full = lambda s: pl.BlockSpec(s, lambda i: tuple(0 for _ in s))
    chunk = pl.BlockSpec((CHUNK_ROWS, S5_WIDTH), lambda i: (order(i), 0))
    in_specs = [chunk]
    args = [u]
    if final:
        in_specs.append(chunk)
        args.append(yb)
    in_specs += [full((BATCH, 2 * S5_NSTATE)), full((2, S5_NSTATE)), full((S5_WIDTH, 2 * S5_NSTATE)),
                 full((2 * S5_NSTATE, S5_WIDTH))]
    args += [h0, a, bd, cd]
    if final:
        in_specs += [full((1, S5_WIDTH)), full((S5_WIDTH, S5_WIDTH)), full((1, S5_WIDTH))]
        args += [dsk, gw, gb]
    return pl.pallas_call(
        functools.partial(_s5_kernel, direction=direction, final=final),
        out_shape=(jax.ShapeDtypeStruct((nt, S5_WIDTH), F32), jax.ShapeDtypeStruct((BATCH, 2 * S5_NSTATE), F32)),
        grid=(n_chunks,),
        in_specs=in_specs,
        out_specs=(chunk, full((BATCH, 2 * S5_NSTATE))),
        scratch_shapes=[
            pltpu.VMEM((CHUNK_ROWS, 2 * S5_NSTATE), F32),
            pltpu.VMEM((BATCH, 2 * S5_NSTATE), F32),
        ],
        compiler_params=_cparams("arbitrary"),
        name="s5_fwd" if final else "s5_bwd",
    )(*args)


def _outproj_kernel(x_ref, ya_ref, yb_ref, ycc_ref, ycl_ref, mod_ref, mw_ref, wo_ref, n2_ref, rw_ref, rb_ref,
                    xo_ref, h2_ref, idx_ref, gate_ref, rank_ref, base_ref, cnt_ref, cnt_sc, *, tile0, n_ctx_tiles):
    i = pl.program_id(0)
    tile = i + tile0
    tm = x_ref.shape[0]

    @pl.when(i == 0)
    def _():
        cnt_sc[...] = jnp.zeros(cnt_sc.shape, F32)

    yc = jnp.where(tile < n_ctx_tiles, ycc_ref[...], ycl_ref[...].reshape(tm, S5_WIDTH))
    merged = jnp.concatenate([ya_ref[h] for h in range(ML_HEADS)] + [yb_ref[...], yc], axis=1)
    merged = (merged * mw_ref[...]).astype(BF16)
    proj = _dot(merged, wo_ref[...])
    g1 = mod_ref[0, 0]
    x = x_ref[...].reshape(tm // BATCH, BATCH, D_MODEL) + g1[None] * proj.reshape(tm // BATCH, BATCH, D_MODEL)
    xo_ref[...] = x.reshape(tm, D_MODEL)
    ms = jnp.mean(x * x, axis=-1, keepdims=True)
    xn = (x * lax.rsqrt(ms + EPS)) * n2_ref[...][None]
    h2 = (xn * (1.0 + mod_ref[0, 2][None]) + mod_ref[0, 1][None]).reshape(tm, D_MODEL)
    _store_token_tiles(h2_ref, 0, h2)
    logits = _dot3(rw_ref[...], h2, nt=True) + rb_ref[...]
    eidx = lax.broadcasted_iota(I32, (N_EXPERTS, tm), 0)
    vals, ids, hots = [], [], []
    for _ in range(TOP_K):
        m = jnp.max(logits, axis=0, keepdims=True)
        sel = jnp.min(jnp.where(logits == m, eidx, N_EXPERTS), axis=0, keepdims=True)
        hot = eidx == sel
        logits = jnp.where(hot, -jnp.inf, logits)
        vals.append(m)
        ids.append(sel)
        hots.append(hot)
    ex = [jnp.exp(v - vals[0]) for v in vals]
    tot = ex[0] + ex[1] + ex[2] + ex[3]
    idx_ref[...] = jnp.concatenate(ids, axis=0)
    gate_ref[...] = jnp.concatenate([e / tot for e in ex], axis=0)
    selm = hots[0].astype(F32) + hots[1].astype(F32) + hots[2].astype(F32) + hots[3].astype(F32)
    r = lax.broadcasted_iota(I32, (tm, tm), 0)
    c = lax.broadcasted_iota(I32, (tm, tm), 1)
    before = _dot(selm.astype(BF16), (r < c).astype(BF16)) + cnt_sc[:, 0:1]
    ranks = [jnp.sum(jnp.where(hot, before, 0.0), axis=0, keepdims=True) for hot in hots]
    rank_ref[...] = jnp.concatenate(ranks, axis=0).astype(I32)
    base_ref[0] = cnt_sc[...]
    cnt_sc[...] = cnt_sc[...] + jnp.sum(selm, axis=1, keepdims=True)
    cnt_ref[...] = cnt_sc[...]


def _outproj(x, ya, yb, ycc, ycl, mod, mw, wo, n2, rw, rb, tile0, n_ctx_rows):
    nt = x.shape[0]
    tm = ROW_TILE
    nct = n_ctx_rows // tm
    n_tiles = nt // tm - tile0
    n_out = n_tiles * tm
    rows = ycl.shape[0] // tm
    ycl4 = ycl.reshape(GRID_W, rows, BATCH, S5_WIDTH)
    full = lambda s: pl.BlockSpec(s, lambda i: tuple(0 for _ in s))
    return pl.pallas_call(
        functools.partial(_outproj_kernel, tile0=tile0, n_ctx_tiles=nct),
        out_shape=(
            jax.ShapeDtypeStruct((n_out, D_MODEL), F32),
            jax.ShapeDtypeStruct((n_out * SUBLANES, LANES), F32),
            jax.ShapeDtypeStruct((TOP_K, n_out), I32),
            jax.ShapeDtypeStruct((TOP_K, n_out), F32),
            jax.ShapeDtypeStruct((TOP_K, n_out), I32),
            jax.ShapeDtypeStruct((n_tiles, N_EXPERTS, LANES), F32),
            jax.ShapeDtypeStruct((N_EXPERTS, LANES), F32),
        ),
        grid=(n_tiles,),
        in_specs=[
            pl.BlockSpec((tm, D_MODEL), lambda i: (i + tile0, 0)),
            pl.BlockSpec((ML_HEADS, tm, LANES), lambda i: (0, i + tile0, 0)),
            pl.BlockSpec((tm, RG_WIDTH), lambda i: (i + tile0, 0)),
            pl.BlockSpec((tm, S5_WIDTH), lambda i: (jnp.minimum(i + tile0, nct - 1), 0)),
            pl.BlockSpec((GRID_W, 1, BATCH, S5_WIDTH), lambda i: (0, jnp.maximum(i + tile0 - nct, 0), 0, 0)),
            pl.BlockSpec((1, 3, BATCH, D_MODEL), lambda i: (jnp.where(i + tile0 < nct, 0, 1), 0, 0, 0)),
            full((1, MERGED)),
            full((MERGED, D_MODEL)),
            full((1, D_MODEL)),
            full((N_EXPERTS, D_MODEL)),
            full((N_EXPERTS, 1)),
        ],
        out_specs=(
            pl.BlockSpec((tm, D_MODEL), lambda i: (i, 0)),
            pl.BlockSpec((tm * SUBLANES, LANES), lambda i: (i, 0)),
            pl.BlockSpec((TOP_K, tm), lambda i: (0, i)),
            pl.BlockSpec((TOP_K, tm), lambda i: (0, i)),
            pl.BlockSpec((TOP_K, tm), lambda i: (0, i)),
            pl.BlockSpec((1, N_EXPERTS, LANES), lambda i: (i, 0, 0)),
            full((N_EXPERTS, LANES)),
        ),
        scratch_shapes=[pltpu.VMEM((N_EXPERTS, LANES), F32)],
        compiler_params=_cparams("arbitrary"),
        name="outproj_router",
    )(x, ya, yb, ycc, ycl4, mod, mw, wo, n2, rw, rb)


def _token_tile(ref, tok):
    return ref.at[pl.ds(pl.multiple_of(tok * SUBLANES, SUBLANES), SUBLANES), :]


STAGE_ROWS = TOP_K * ROW_TILE * SUBLANES


def _start_run_copies(hbm, stage, runs_ref, sem, to_hbm):
    def per_expert(e, carry):
        row = runs_ref[0, 0, e]
        length = runs_ref[0, 0, N_EXPERTS + e]
        off = runs_ref[0, 0, 2 * N_EXPERTS + e]
        for bit in range(RUN_BITS):
            size = (1 << bit) * SUBLANES

            @pl.when((lax.shift_right_logical(length, bit) & 1) == 1)
            def _():
                done = lax.shift_left(lax.shift_right_logical(length, bit + 1), bit + 1)
                h = hbm.at[pl.ds(pl.multiple_of((row + done) * SUBLANES, SUBLANES), size), :]
                s = stage.at[pl.ds(pl.multiple_of((off + done) * SUBLANES, SUBLANES), size), :]
                cp = pltpu.make_async_copy(s, h, sem) if to_hbm else pltpu.make_async_copy(h, s, sem)
                cp.start(priority=bit % DMA_PRIORITIES)
        return carry

    lax.fori_loop(0, N_EXPERTS, per_expert, 0)


def _wait_run_copies(hbm, stage, sem, to_hbm):
    h = hbm.at[pl.ds(0, STAGE_ROWS), :]
    cp = pltpu.make_async_copy(stage, h, sem) if to_hbm else pltpu.make_async_copy(h, stage, sem)
    cp.wait()


def _dispatch_kernel(pe_ref, pd_ref, nu_ref, runs_ref, loc_ref, h_ref, xs_hbm, stage, zero_sc, sem, zsem, *,
                     n_blocks, n_tiles):
    i = pl.program_id(0)
    tm = ROW_TILE

    def zero_block(blk):
        start = pl.multiple_of(blk * (MOE_BM * SUBLANES), SUBLANES)
        return pltpu.make_async_copy(zero_sc, xs_hbm.at[pl.ds(start, MOE_BM * SUBLANES), :], zsem)

    @pl.when(i == 0)
    def _():
        zero_sc[...] = jnp.zeros(zero_sc.shape, F32)
        for e in range(N_EXPERTS):
            @pl.when(pd_ref[e] > 0)
            def _():
                zero_block(pe_ref[e] // MOE_BM - 1).start()
        for e in range(N_EXPERTS):
            @pl.when(pd_ref[e] > 0)
            def _():
                zero_block(0).wait()

        def fill(blk, carry):
            cp = zero_block(blk)
            cp.start()
            cp.wait()
            return carry

        lax.fori_loop(nu_ref[0], n_blocks, fill, 0)

    buf = lax.rem(i, 2)

    def place(r, carry):
        t = h_ref[pl.ds(pl.multiple_of(r * SUBLANES, SUBLANES), SUBLANES), :]
        for k in range(TOP_K):
            slot = loc_ref[0, 0, k * tm + r]
            stage[buf, pl.ds(pl.multiple_of(slot * SUBLANES, SUBLANES), SUBLANES), :] = t
        return carry

    lax.fori_loop(0, tm, place, 0, unroll=4)
    _start_run_copies(xs_hbm, stage.at[buf], runs_ref, sem.at[buf], to_hbm=True)

    @pl.when(i > 0)
    def _():
        _wait_run_copies(xs_hbm, stage.at[1 - buf], sem.at[1 - buf], to_hbm=True)

    @pl.when(i == n_tiles - 1)
    def _():
        _wait_run_copies(xs_hbm, stage.at[buf], sem.at[buf], to_hbm=True)


def _moe_dispatch(pad_ends, padded, n_used, runs, loc, h2t, n_blocks):
    tm = ROW_TILE
    n_tiles = runs.shape[0]
    smem = lambda n: pl.BlockSpec((1, 1, n), lambda i, pe, pd, nu: (i, 0, 0), memory_space=pltpu.SMEM)
    grid_spec = pltpu.PrefetchScalarGridSpec(
        num_scalar_prefetch=3,
        grid=(n_tiles,),
        in_specs=[
            smem(3 * N_EXPERTS),
            smem(TOP_K * tm),
            pl.BlockSpec((tm * SUBLANES, LANES), lambda i, pe, pd, nu: (i, 0)),
        ],
        out_specs=pl.BlockSpec(memory_space=pl.ANY),
        scratch_shapes=[
            pltpu.VMEM((2, STAGE_ROWS, LANES), F32),
            pltpu.VMEM((MOE_BM * SUBLANES, LANES), F32),
            pltpu.SemaphoreType.DMA((2,)),
            pltpu.SemaphoreType.DMA(()),
        ],
    )
    return pl.pallas_call(
        functools.partial(_dispatch_kernel, n_blocks=n_blocks, n_tiles=n_tiles),
        out_shape=jax.ShapeDtypeStruct((n_blocks * MOE_BM * SUBLANES, LANES), F32),
        grid_spec=grid_spec,
        compiler_params=_cparams("arbitrary"),
        name="moe_dispatch",
    )(pad_ends, padded, n_used, runs, loc, h2t)


def _ffn_kernel(be_ref, nu_ref, x_ref, wgu_ref, bgu_ref, wd_ref, bd_ref, y_ref, wgu_sc, wd_sc):
    i = pl.program_id(0)
    changed = jnp.logical_or(i == 0, be_ref[i] != be_ref[jnp.maximum(i - 1, 0)])

    @pl.when(changed)
    def _():
        wgu_sc[...] = wgu_ref[0].astype(BF16)
        wd_sc[...] = wd_ref[0].astype(BF16)

    @pl.when(i < nu_ref[0])
    def _():
        x = _load_token_tiles(x_ref, 0, MOE_BM).astype(BF16)
        gu = _dot(x, wgu_sc[...]) + bgu_ref[0]
        gate = jnp.minimum(gu[:, :D_EXPERT], SWIGLU_LIMIT)
        up = jnp.clip(gu[:, D_EXPERT:], -SWIGLU_LIMIT, SWIGLU_LIMIT)
        glu = gate * _sigmoid(SWIGLU_ALPHA * gate)
        act = ((up + 1.0) * glu).astype(BF16)
        _store_token_tiles(y_ref, 0, _dot(act, wd_sc[...]) + bd_ref[0])

    @pl.when(i >= nu_ref[0])
    def _():
        y_ref[...] = jnp.zeros(y_ref.shape, F32)


def _moe_ffn(block_e, n_used, xs, wgu, bgu, wd, bd):
    rows = MOE_BM * SUBLANES
    n_blocks = xs.shape[0] // rows
    ne = wgu.shape[0]
    grid_spec = pltpu.PrefetchScalarGridSpec(
        num_scalar_prefetch=2,
        grid=(n_blocks,),
        in_specs=[
            pl.BlockSpec((rows, LANES), lambda i, be, nu: (jnp.minimum(i, nu[0] - 1), 0)),
            pl.BlockSpec((1, D_MODEL, 2 * D_EXPERT), lambda i, be, nu: (be[i], 0, 0)),
            pl.BlockSpec((1, 1, 2 * D_EXPERT), lambda i, be, nu: (be[i], 0, 0)),
            pl.BlockSpec((1, D_EXPERT, D_MODEL), lambda i, be, nu: (be[i], 0, 0)),
            pl.BlockSpec((1, 1, D_MODEL), lambda i, be, nu: (be[i], 0, 0)),
        ],
        out_specs=pl.BlockSpec((rows, LANES), lambda i, be, nu: (i, 0)),
        scratch_shapes=[
            pltpu.VMEM((D_MODEL, 2 * D_EXPERT), BF16),
            pltpu.VMEM((D_EXPERT, D_MODEL), BF16),
        ],
    )
    return pl.pallas_call(
        _ffn_kernel,
        out_shape=jax.ShapeDtypeStruct((n_blocks * rows, LANES), F32),
        grid_spec=grid_spec,
        compiler_params=_cparams("arbitrary"),
        name="moe_ffn",
    )(block_e, n_used, xs, wgu, bgu.reshape(ne, 1, -1), wd, bd.reshape(ne, 1, -1))


def _combine_kernel(runs_ref, next_runs_ref, loc_ref, gate_ref, x_ref, mod_ref, fw_ref, ys_hbm, out_ref, stage, comb,
                    sem, *, final_norm, n_tiles):
    i = pl.program_id(0)
    tm = x_ref.shape[0]
    buf = lax.rem(i, 2)

    @pl.when(i == 0)
    def _():
        _start_run_copies(ys_hbm, stage.at[0], runs_ref, sem.at[0], to_hbm=False)

    @pl.when(i + 1 < n_tiles)
    def _():
        _start_run_copies(ys_hbm, stage.at[1 - buf], next_runs_ref, sem.at[1 - buf], to_hbm=False)

    _wait_run_copies(ys_hbm, stage.at[buf], sem.at[buf], to_hbm=False)

    def gather(r, carry):
        acc = None
        for k in range(TOP_K):
            slot = loc_ref[0, 0, k * tm + r]
            t = gate_ref[0, 0, k * tm + r] * stage[buf, pl.ds(pl.multiple_of(slot * SUBLANES, SUBLANES), SUBLANES), :]
            acc = t if acc is None else acc + t
        comb[pl.ds(pl.multiple_of(r * SUBLANES, SUBLANES), SUBLANES), :] = acc
        return carry

    lax.fori_loop(0, tm, gather, 0, unroll=4)
    f = _load_token_tiles(comb, 0, tm)
    x = x_ref[...].reshape(tm // BATCH, BATCH, D_MODEL) + mod_ref[0][None] * f.reshape(tm // BATCH, BATCH, D_MODEL)
    if final_norm:
        ms = jnp.mean(x * x, axis=-1, keepdims=True)
        x = (x * lax.rsqrt(ms + EPS)) * fw_ref[...][None]
    out_ref[...] = x.reshape(tm, D_MODEL)


def _moe_combine(runs, loc, gates_r, x, mod, fw, ys, n_ctx_rows, final_norm):
    nt = x.shape[0]
    tm = ROW_TILE
    n_tiles = nt // tm
    nct = n_ctx_rows // tm
    smem = lambda n: pl.BlockSpec((1, 1, n), lambda i: (i, 0, 0), memory_space=pltpu.SMEM)
    return pl.pallas_call(
        functools.partial(_combine_kernel, final_norm=final_norm, n_tiles=n_tiles),
        out_shape=jax.ShapeDtypeStruct((nt, D_MODEL), F32),
        grid=(n_tiles,),
        in_specs=[
            smem(3 * N_EXPERTS),
            pl.BlockSpec((1, 1, 3 * N_EXPERTS), lambda i: (jnp.minimum(i + 1, n_tiles - 1), 0, 0),
                         memory_space=pltpu.SMEM),
            smem(TOP_K * tm),
            smem(TOP_K * tm),
            pl.BlockSpec((tm, D_MODEL), lambda i: (i, 0)),
            pl.BlockSpec((1, BATCH, D_MODEL), lambda i: (jnp.where(i < nct, 0, 1), 0, 0)),
            pl.BlockSpec((1, D_MODEL), lambda i: (0, 0)),
            pl.BlockSpec(memory_space=pl.ANY),
        ],
        out_specs=pl.BlockSpec((tm, D_MODEL), lambda i: (i, 0)),
        scratch_shapes=[
            pltpu.VMEM((2, STAGE_ROWS, LANES), F32),
            pltpu.VMEM((tm * SUBLANES, LANES), F32),
            pltpu.SemaphoreType.DMA((2,)),
        ],
        compiler_params=_cparams("arbitrary"),
        name="moe_combine",
    )(runs, runs, loc, gates_r, x, mod, fw, ys)


def _moe(h2t, idx, gates, rank, tile_base, counts, x, g2, fw, layer, wgu, bgu, wd, bd, n_ctx_rows, final_norm):
    nt = idx.shape[1]
    bm = MOE_BM
    tm = ROW_TILE
    n_tiles = nt // tm
    n_blocks = (nt * TOP_K) // bm + N_EXPERTS
    cnt = counts[:, 0].astype(I32)
    padded = (cnt + bm - 1) // bm * bm
    pad_ends = jnp.cumsum(padded)
    pad_starts = pad_ends - padded
    experts = jnp.arange(N_EXPERTS, dtype=I32)
    base = tile_base[:, :, 0].astype(I32)
    run_len = jnp.concatenate([base[1:], cnt[None]], axis=0) - base
    run_off = jnp.cumsum(run_len, axis=1) - run_len
    run_row = pad_starts[None, :] + base
    runs = jnp.concatenate([run_row, run_len, run_off], axis=1).reshape(n_tiles, 1, 3 * N_EXPERTS)
    delta = (run_off - base).T
    idx3 = idx.reshape(TOP_K, n_tiles, tm)
    loc = rank.reshape(TOP_K, n_tiles, tm) + jnp.sum(
        jnp.where(idx3[None] == experts[:, None, None, None], delta[:, None, :, None], 0), axis=0)
    by_tile = lambda a: a.transpose(1, 0, 2).reshape(n_tiles, 1, TOP_K * tm)
    loc = by_tile(loc)
    block_start = jnp.arange(n_blocks, dtype=I32) * bm
    block_e = jnp.minimum(jnp.sum((pad_ends[None, :] <= block_start[:, None]).astype(I32), axis=1), N_EXPERTS - 1)
    n_used = (pad_ends[-1] // bm).astype(I32).reshape(1)
    xs = _moe_dispatch(pad_ends.astype(I32), padded, n_used, runs, loc, h2t, n_blocks)
    ys = _moe_ffn(block_e + layer * N_EXPERTS, n_used, xs, wgu, bgu, wd, bd)
    return _moe_combine(runs, loc, by_tile(gates.reshape(TOP_K, n_tiles, tm)), x, g2, fw, ys, n_ctx_rows, final_norm)


def _pad_heads(a):
    lead = a.shape[:-1]
    a = a.reshape(*lead, 4 * ML_HEADS, ML_HEAD_DIM)
    a = jnp.pad(a, [(0, 0)] * len(lead) + [(0, 0), (0, LANES - ML_HEAD_DIM)])
    return a.reshape(*lead, 4 * ML_HEADS * LANES)


def _kvqo(a):
    w = ML_HEADS * ML_HEAD_DIM
    return jnp.concatenate([a[..., w:2 * w], a[..., 2 * w:3 * w], a[..., :w], a[..., 3 * w:4 * w]], axis=-1)


def _block_diag(blocks):
    return jax.scipy.linalg.block_diag(*[blocks[g] for g in range(blocks.shape[0])])


def _s5_discretise(lam_re, lam_im, log_dt, b_re, b_im):
    dt = jnp.exp(log_dt)[:, None]
    mag = jnp.exp(lam_re * dt)
    ar, ai = mag * jnp.cos(lam_im * dt), mag * jnp.sin(lam_im * dt)
    den = lam_re * lam_re + lam_im * lam_im
    cr = ((ar - 1.0) * lam_re + ai * lam_im) / den
    ci = (ai * lam_re - (ar - 1.0) * lam_im) / den
    bbr = cr[..., None] * b_re - ci[..., None] * b_im
    bbi = cr[..., None] * b_im + ci[..., None] * b_re
    return ar, ai, bbr, bbi


def kernel(x, c, ctx, c_ctx, ada_w, ada_b, norm1_w, w_in, b_in, rg_conv_w, rg_conv_b, rg_wa, rg_ba, rg_wx, rg_bx, rg_lambda, s5_lambda_re, s5_lambda_im, s5_log_dt, s5_b_re, s5_b_im, s5_c_re, s5_c_im, s5_d, s5_glu_w, s5_glu_b, mix_norm_w, w_out, norm2_w, router_w, router_b, moe_w_gate_up, moe_b_gate_up, moe_w_down, moe_b_down, final_norm_w):
    bsz, seq, dm = x.shape
    sc = ctx.shape[1]
    assert bsz == BATCH and dm == D_MODEL and seq % GRID_W == 0
    n_ctx_rows = sc * BATCH
    n_lat_rows = seq * BATCH
    assert n_ctx_rows % CHUNK_ROWS == 0 and n_lat_rows % CHUNK_ROWS == 0 and GRID_W * BATCH == ROW_TILE
    n_ctx_chunks = n_ctx_rows // CHUNK_ROWS
    depth = ada_w.shape[0]

    xs = jnp.concatenate([ctx.transpose(1, 0, 2).reshape(n_ctx_rows, dm),
                          x.transpose(1, 0, 2).reshape(n_lat_rows, dm)], axis=0)

    c_rows = jnp.concatenate([c, c_ctx[None], jnp.zeros((16 - bsz - 1, dm), F32)], axis=0)
    mods = _modulation(c_rows, ada_w, ada_b)
    mods = mods.reshape(depth, 16, 6, dm)
    mod_lat = mods[:, :bsz].transpose(0, 2, 1, 3)
    mod_ctx = jnp.broadcast_to(mods[:, bsz][:, :, None, :], mod_lat.shape)
    mod = jnp.stack([mod_ctx, mod_lat], axis=1)

    for l in range(depth):
        with_ctx = l < depth - 1
        g0 = 4 * ML_HEADS * ML_HEAD_DIM
        ng = 2 * ML_HEADS
        wa_cols = _kvqo(w_in[l][:, :g0])
        ba_cols = _kvqo(b_in[l][:g0])
        kscale = jnp.ones((4, ML_HEADS * LANES), F32).at[0].set(ML_HEAD_DIM ** -0.5).reshape(-1)
        ones_col = jnp.zeros((4, ML_HEADS, LANES), F32).at[1, :, ML_HEAD_DIM].set(1.0).reshape(-1)
        w_cat = jnp.concatenate([
            _pad_heads(wa_cols) * kscale,
            jnp.pad(w_in[l][:, g0:g0 + ng], ((0, 0), (0, LANES - ng))),
            jnp.pad(w_in[l][:, g0 + ng:g0 + 2 * ng], ((0, 0), (0, LANES - ng))),
            w_in[l][:, g0 + 2 * ng:]], axis=1).astype(BF16)
        b_cat = jnp.concatenate([
            _pad_heads(ba_cols) * kscale + ones_col,
            jnp.pad(b_in[l][g0:g0 + ng], (0, LANES - ng)),
            jnp.pad(b_in[l][g0 + ng:g0 + 2 * ng], (0, LANES - ng)),
            b_in[l][g0 + 2 * ng:]])[None]
        wg = [jnp.concatenate([_block_diag(rg_wa[l, d]), _block_diag(rg_wx[l, d])], axis=1).astype(BF16)
              for d in range(2)]
        bg = [jnp.concatenate([rg_ba[l, d], rg_bx[l, d]])[None] for d in range(2)]
        s5p = []
        for d in range(2):
            ar, ai, bbr, bbi = _s5_discretise(s5_lambda_re[l, d], s5_lambda_im[l, d], s5_log_dt[l, d],
                                              s5_b_re[l], s5_b_im[l])
            a = jnp.stack([ar.reshape(-1), ai.reshape(-1)])
            bd = jnp.concatenate([_block_diag(bbr.transpose(0, 2, 1)), _block_diag(bbi.transpose(0, 2, 1))],
                                 axis=1).astype(BF16)
            cd = jnp.concatenate([_block_diag(s5_c_re[l].transpose(0, 2, 1)),
                                  -_block_diag(s5_c_im[l].transpose(0, 2, 1))], axis=0).astype(BF16)
            s5p.append((a, bd, cd))
        mw = jnp.concatenate([_pad_heads(jnp.tile(mix_norm_w[l][:ML_HEADS * ML_HEAD_DIM], 4))[:ML_HEADS * LANES],
                              mix_norm_w[l][ML_HEADS * ML_HEAD_DIM:]])[None]
        wo_a = w_out[l][:ML_HEADS * ML_HEAD_DIM].reshape(ML_HEADS, ML_HEAD_DIM, dm)
        wo_a = jnp.pad(wo_a, ((0, 0), (0, LANES - ML_HEAD_DIM), (0, 0))).reshape(ML_HEADS * LANES, dm)
        wo = jnp.concatenate([wo_a, w_out[l][ML_HEADS * ML_HEAD_DIM:]], axis=0).astype(BF16)

        pa, pb, pcc, pcl = _inproj(xs, mod[l, :, 0:2], norm1_w[l][None], w_cat, b_cat, n_ctx_rows)
        pcl = pcl.reshape(n_lat_rows, S5_WIDTH)
        c0f, m0f = _mlstm_states(pa, 0, n_ctx_chunks)
        c0b, m0b = _mlstm_states(pa, 1, n_ctx_chunks)
        ya = _mlstm_outputs(pa, c0f, m0f, c0b, m0b)
        rgb = _rglru_pass(pb, None, rg_conv_w[l], rg_conv_b[l][None], wg[1], bg[1], rg_lambda[l, 1][None], 1, n_ctx_chunks)
        yb = _rglru_pass(pb, rgb, rg_conv_w[l], rg_conv_b[l][None], wg[0], bg[0], rg_lambda[l, 0][None], 0, n_ctx_chunks)
        zero_state = jnp.zeros((BATCH, 2 * S5_NSTATE), F32)
        a1, bd1, cd1 = s5p[1]
        a0, bd0, cd0 = s5p[0]
        dsk, gw, gb = s5_d[l][None], s5_glu_w[l].astype(BF16), s5_glu_b[l][None]
        ycb_c, st = _s5_pass(pcc, None, zero_state, a1, bd1, cd1, None, None, None, 1)
        ycb_l, _ = _s5_pass(pcl, None, st, a1, bd1, cd1, None, None, None, 1)
        yc_c, st = _s5_pass(pcc, ycb_c, zero_state, a0, bd0, cd0, dsk, gw, gb, 0)
        yc_l, _ = _s5_pass(pcl, ycb_l, st, a0, bd0, cd0, dsk, gw, gb, 0)

        tile0 = 0 if with_ctx else n_ctx_rows // ROW_TILE
        xo, h2, idx, gates, rank, tile_base, counts = _outproj(
            xs, ya, yb, yc_c, yc_l, mod[l, :, 2:5], mw, wo, norm2_w[l][None], router_w[l].T,
            router_b[l][:, None], tile0, n_ctx_rows)

        xs_new = _moe(h2, idx, gates, rank, tile_base, counts, xo, mod[l, :, 5], final_norm_w[None], l,
                      moe_w_gate_up.reshape(-1, dm, 2 * D_EXPERT), moe_b_gate_up.reshape(-1, 2 * D_EXPERT),
                      moe_w_down.reshape(-1, D_EXPERT, dm), moe_b_down.reshape(-1, dm),
                      n_ctx_rows if with_ctx else 0, l == depth - 1)
        xs = xs_new

    out = xs.reshape(seq, bsz, dm).transpose(1, 0, 2)
    return out
```

```python
import functools

import jax
import jax.numpy as jnp
from jax import lax
from jax.experimental import pallas as pl
from jax.experimental.pallas import tpu as pltpu

F32 = jnp.float32
BF16 = jnp.bfloat16
I32 = jnp.int32

D_MODEL = 1024
BATCH = 8
DEPTH = 4
GRID_W = 64
ML_HEADS = 4
ML_HEAD_DIM = 96
ML_CHUNK = 128
ML_M_INIT = -1e30
RG_BLOCKS = 6
RG_BLOCK_DIM = 64
RG_WIDTH = RG_BLOCKS * RG_BLOCK_DIM
RG_CONV = 4
RG_C = 8.0
S5_GROUPS = 16
S5_GROUP = 16
S5_WIDTH = S5_GROUPS * S5_GROUP
S5_STATE = 64
S5_NSTATE = S5_GROUPS * S5_STATE
N_EXPERTS = 32
TOP_K = 4
D_EXPERT = D_MODEL
SWIGLU_LIMIT = 7.0
SWIGLU_ALPHA = 1.702
EPS = 1e-6

LANES = 128
SUBLANES = 8

ML_K, ML_V, ML_Q, ML_O = 0, ML_HEADS, 2 * ML_HEADS, 3 * ML_HEADS
ML_IG = 4 * ML_HEADS
ML_FG = ML_IG + 1
ML_SLABS = ML_FG + 1
ML_STATE_SLABS = 2 * ML_HEADS
A_PAD = ML_SLABS * LANES
MERGED = ML_HEADS * LANES + RG_WIDTH + S5_WIDTH

ROW_TILE = 512
T_CHUNK = 128
CHUNK_ROWS = T_CHUNK * BATCH
MOE_BM = 512
DMA_PRIORITIES = 2
RUN_BITS = ROW_TILE.bit_length()
VMEM_LIMIT = 56 * 1024 * 1024


def _cparams(*sem):
    return pltpu.CompilerParams(dimension_semantics=sem, vmem_limit_bytes=VMEM_LIMIT)


def _sigmoid(x):
    return 0.5 * (jnp.tanh(0.5 * x) + 1.0)


def _log_sigmoid(x):
    return jnp.minimum(x, 0.0) - jnp.log1p(jnp.exp(-jnp.abs(x)))


def _softplus(x):
    return jnp.maximum(x, 0.0) + jnp.log1p(jnp.exp(-jnp.abs(x)))


def _gelu(x):
    return 0.5 * x * (1.0 + jnp.tanh(0.7978845608028654 * (x + 0.044715 * (x * x * x))))


def _dot(a, b):
    return jnp.dot(a, b, preferred_element_type=F32)


def _dot_nt(a, b):
    return lax.dot_general(a, b, (((1,), (1,)), ((), ())), preferred_element_type=F32)


def _split(a):
    hi = a.astype(BF16)
    lo = (a - hi.astype(F32)).astype(BF16)
    return hi, lo


def _dot_lx(a_exact, b):
    hi, lo = _split(b)
    return _dot(a_exact, hi) + _dot(a_exact, lo)


def _dot_xr(a, b_exact):
    hi, lo = _split(a)
    return _dot(hi, b_exact) + _dot(lo, b_exact)


def _dot3(a, b, nt=False):
    ah, al = _split(a)
    bh, bl = _split(b)
    d = _dot_nt if nt else _dot
    return d(ah, bh) + (d(ah, bl) + d(al, bh))


def _group_ones(n, shift):
    r = lax.broadcasted_iota(I32, (n, n), 0)
    c = lax.broadcasted_iota(I32, (n, n), 1)
    return (lax.shift_right_logical(r, shift) == lax.shift_right_logical(c, shift)).astype(BF16)


def _store_token_tiles(ref, tok0, val):
    n = val.shape[0]
    for s in range(SUBLANES):
        ref[pl.ds(tok0 * SUBLANES + s, n, stride=SUBLANES), :] = val[:, s * LANES:(s + 1) * LANES]


def _load_token_tiles(ref, tok0, n):
    return jnp.concatenate(
        [ref[pl.ds(tok0 * SUBLANES + s, n, stride=SUBLANES), :] for s in range(SUBLANES)], axis=1)


def _mod_kernel(c_ref, w_ref, b_ref, o_ref):
    c = c_ref[...]
    s = c * _sigmoid(c)
    o_ref[0] = _dot3(s, w_ref[0]) + b_ref[0]


def _modulation(c_rows, ada_w, ada_b):
    depth, d, n = ada_w.shape
    tn = 1536
    return pl.pallas_call(
        _mod_kernel,
        out_shape=jax.ShapeDtypeStruct((depth, 16, n), F32),
        grid=(depth, n // tn),
        in_specs=[
            pl.BlockSpec((16, d), lambda l, j: (0, 0)),
            pl.BlockSpec((1, d, tn), lambda l, j: (l, 0, j)),
            pl.BlockSpec((1, 1, tn), lambda l, j: (l, 0, j)),
        ],
        out_specs=pl.BlockSpec((1, 16, tn), lambda l, j: (l, 0, j)),
        compiler_params=_cparams("arbitrary", "arbitrary"),
        name="adaln_mod",
    )(c_rows, ada_w, ada_b.reshape(depth, 1, n))


def _inproj_kernel(x_ref, mod_ref, nw_ref, w_ref, b_ref, pa_ref, pb_ref, pcc_ref, pcl_ref, *, n_ctx_tiles):
    i = pl.program_id(0)
    x = x_ref[...]
    tm = x.shape[0]
    ms = jnp.mean(x * x, axis=-1, keepdims=True)
    xn = (x * lax.rsqrt(ms + EPS)) * nw_ref[...]
    xn = xn.reshape(tm // BATCH, BATCH, D_MODEL)
    h = xn * (1.0 + mod_ref[0, 1][None]) + mod_ref[0, 0][None]
    h = h.reshape(tm, D_MODEL).astype(BF16)
    for j in range(0, ML_SLABS, 2):
        p = _dot(h, w_ref[:, j * LANES:(j + 2) * LANES]) + b_ref[:, j * LANES:(j + 2) * LANES]
        pa_ref[j] = p[:, :LANES]
        pa_ref[j + 1] = p[:, LANES:]
    c0 = A_PAD
    pb_ref[...] = _dot(h, w_ref[:, c0:c0 + 2 * RG_WIDTH]) + b_ref[:, c0:c0 + 2 * RG_WIDTH]
    c0 = A_PAD + 2 * RG_WIDTH
    pc = _dot(h, w_ref[:, c0:c0 + S5_WIDTH]) + b_ref[:, c0:c0 + S5_WIDTH]

    @pl.when(i < n_ctx_tiles)
    def _():
        pcc_ref[...] = pc

    @pl.when(i >= n_ctx_tiles)
    def _():
        pcl_ref[...] = pc.reshape(GRID_W, 1, BATCH, S5_WIDTH)


def _inproj(x, mod, nw, w, b, n_ctx_rows):
    nt = x.shape[0]
    tm = ROW_TILE
    nct = n_ctx_rows // tm
    n_lat_rows = nt - n_ctx_rows
    rows = n_lat_rows // tm
    ncols = w.shape[1]
    return pl.pallas_call(
        functools.partial(_inproj_kernel, n_ctx_tiles=nct),
        out_shape=(
            jax.ShapeDtypeStruct((ML_SLABS, nt, LANES), F32),
            jax.ShapeDtypeStruct((nt, 2 * RG_WIDTH), F32),
            jax.ShapeDtypeStruct((n_ctx_rows, S5_WIDTH), F32),
            jax.ShapeDtypeStruct((GRID_W, rows, BATCH, S5_WIDTH), F32),
        ),
        grid=(nt // tm,),
        in_specs=[
            pl.BlockSpec((tm, D_MODEL), lambda i: (i, 0)),
            pl.BlockSpec((1, 2, BATCH, D_MODEL), lambda i: (jnp.where(i < nct, 0, 1), 0, 0, 0)),
            pl.BlockSpec((1, D_MODEL), lambda i: (0, 0)),
            pl.BlockSpec((D_MODEL, ncols), lambda i: (0, 0)),
            pl.BlockSpec((1, ncols), lambda i: (0, 0)),
        ],
        out_specs=(
            pl.BlockSpec((ML_SLABS, tm, LANES), lambda i: (0, i, 0)),
            pl.BlockSpec((tm, 2 * RG_WIDTH), lambda i: (i, 0)),
            pl.BlockSpec((tm, S5_WIDTH), lambda i: (jnp.minimum(i, nct - 1), 0)),
            pl.BlockSpec((GRID_W, 1, BATCH, S5_WIDTH), lambda i: (0, jnp.maximum(i - nct, 0), 0, 0)),
        ),
        compiler_params=_cparams("arbitrary"),
        name="inproj",
    )(x, mod, nw, w, b)


def _mlstm_gates(ig, fg):
    L = ML_CHUNK
    row = lax.broadcasted_iota(I32, (L, L), 0)
    col = lax.broadcasted_iota(I32, (L, L), 1)
    lf = _log_sigmoid(fg)
    b_fwd = _dot_lx((row >= col).astype(BF16), lf)
    b_tot = _dot_lx(jnp.ones((L, L), BF16), lf)
    b_bwd = b_tot - b_fwd + lf
    bcol = jnp.where(col < ML_HEADS, b_fwd, b_bwd)
    return ig - bcol, bcol, b_tot


def _mlstm_state_kernel(kv_ref, g_ref, c0_ref, m0_ref, c_sc, m_sc, *, direction):
    i = pl.program_id(0)
    L = ML_CHUNK

    @pl.when(i == 0)
    def _():
        c_sc[...] = jnp.zeros(c_sc.shape, F32)
        m_sc[...] = jnp.full(m_sc.shape, ML_M_INIT, F32)

    def per_batch(b, carry):
        rows = pl.ds(b, L, stride=BATCH)
        z, _, b_tot = _mlstm_gates(g_ref[0, rows, :], g_ref[1, rows, :])
        z_t = z.T
        for h in range(ML_HEADS):
            j = ML_HEADS * direction + h
            idx = b * ML_HEADS + h
            k = kv_ref[ML_K + h, rows, :].astype(BF16)
            v_t = kv_ref[ML_V + h, rows, :].T
            c0 = c_sc[idx]
            m0_tile = m_sc[idx]
            c0_ref[0, idx] = c0.astype(BF16)
            m0_ref[0, idx] = m0_tile
            m0 = m0_tile[0:1, :]
            bt = jnp.broadcast_to(b_tot[0:1, j:j + 1], (1, LANES))
            w = bt + z_t[j:j + 1, :]
            mloc = jnp.broadcast_to(jnp.max(w, axis=1, keepdims=True), (1, LANES))
            cloc = _dot((v_t * jnp.exp(w - mloc)).astype(BF16), k)
            mnew = jnp.maximum(bt + m0, mloc)
            a = jnp.exp(bt + m0 - mnew)
            sc = jnp.exp(mloc - mnew)
            c_sc[idx] = a * c0 + sc * cloc
            m_sc[idx] = jnp.broadcast_to(mnew, (SUBLANES, LANES))
        return carry

    lax.fori_loop(0, BATCH, per_batch, 0, unroll=4)


def _mlstm_out_kernel(pa_ref, c0f_ref, m0f_ref, c0b_ref, m0b_ref, out_ref):
    L = ML_CHUNK
    row = lax.broadcasted_iota(I32, (L, L), 0)
    col = lax.broadcasted_iota(I32, (L, L), 1)
    masks = (row <= col, row >= col)
    head_row = row < ML_HEAD_DIM
    m0_refs = (m0f_ref, m0b_ref)

    def per_batch(b, carry):
        rows = pl.ds(b, L, stride=BATCH)
        z, bcol, _ = _mlstm_gates(pa_ref[ML_IG, rows, :], pa_ref[ML_FG, rows, :])
        bcol_t = bcol.T
        for h in range(ML_HEADS):
            idx = b * ML_HEADS + h
            q = pa_ref[ML_Q + h, rows, :].astype(BF16)
            k = pa_ref[ML_K + h, rows, :].astype(BF16)
            v_t = pa_ref[ML_V + h, rows, :].T.astype(BF16)
            s_t = _dot_nt(k, q)
            nc_t = _dot_nt(jnp.concatenate([c0f_ref[0, idx], c0b_ref[0, idx]], axis=0), q)
            w_t = None
            h_t = None
            for d in range(2):
                j = ML_HEADS * d + h
                m0 = m0_refs[d][0, idx][0:1, :]
                r = jnp.where(masks[d], z[:, j:j + 1], -jnp.inf)
                u = jnp.maximum(jnp.max(r, axis=0, keepdims=True), m0)
                sqk = s_t * jnp.exp(r - u)
                inter = jnp.exp(m0 - u)
                ncd = nc_t[d * LANES:(d + 1) * LANES, :]
                den = jnp.sum(sqk, axis=0, keepdims=True) + inter * ncd[ML_HEAD_DIM:ML_HEAD_DIM + 1, :]
                inv = 1.0 / jnp.maximum(jnp.abs(den), jnp.exp(-(bcol_t[j:j + 1, :] + u)))
                w_t = sqk * inv if w_t is None else w_t + sqk * inv
                h_t = (inter * inv) * ncd if h_t is None else h_t + (inter * inv) * ncd
            h_t = jnp.where(head_row, h_t + _dot(v_t, w_t.astype(BF16)), 0.0)
            ms = jnp.sum(h_t * h_t, axis=0, keepdims=True) * (1.0 / ML_HEAD_DIM)
            o_t = pa_ref[ML_O + h, rows, :].T
            out_ref[h, rows, :] = (_sigmoid(o_t) * (h_t * lax.rsqrt(ms + EPS))).T
        return carry

    lax.fori_loop(0, BATCH, per_batch, 0, unroll=4)


def _chunk_order(i, direction, n_ctx, n_all):
    if direction == 0:
        return i
    return jnp.where(i < n_ctx, n_ctx - 1 - i, n_all - 1 + n_ctx - i)


def _mlstm_states(pa, direction, n_ctx):
    nt = pa.shape[1]
    n_all = nt // CHUNK_ROWS
    nbh = BATCH * ML_HEADS
    order = lambda i: _chunk_order(i, direction, n_ctx, n_all)
    return pl.pallas_call(
        functools.partial(_mlstm_state_kernel, direction=direction),
        out_shape=(jax.ShapeDtypeStruct((n_all, nbh, ML_CHUNK, LANES), BF16),
                   jax.ShapeDtypeStruct((n_all, nbh, SUBLANES, LANES), F32)),
        grid=(n_all,),
        in_specs=[
            pl.BlockSpec((ML_STATE_SLABS, CHUNK_ROWS, LANES), lambda i: (0, order(i), 0)),
            pl.BlockSpec((2, CHUNK_ROWS, LANES), lambda i: (ML_IG // 2, order(i), 0)),
        ],
        out_specs=(pl.BlockSpec((1, nbh, ML_CHUNK, LANES), lambda i: (order(i), 0, 0, 0)),
                   pl.BlockSpec((1, nbh, SUBLANES, LANES), lambda i: (order(i), 0, 0, 0))),
        scratch_shapes=[
            pltpu.VMEM((nbh, ML_CHUNK, LANES), F32),
            pltpu.VMEM((nbh, SUBLANES, LANES), F32),
        ],
        compiler_params=_cparams("arbitrary"),
        name="mlstm_state_fwd" if direction == 0 else "mlstm_state_bwd",
    )(pa, pa)


def _mlstm_outputs(pa, c0f, m0f, c0b, m0b):
    nt = pa.shape[1]
    n_all = nt // CHUNK_ROWS
    nbh = BATCH * ML_HEADS
    cspec = pl.BlockSpec((1, nbh, ML_CHUNK, LANES), lambda i: (i, 0, 0, 0))
    mspec = pl.BlockSpec((1, nbh, SUBLANES, LANES), lambda i: (i, 0, 0, 0))
    return pl.pallas_call(
        _mlstm_out_kernel,
        out_shape=jax.ShapeDtypeStruct((ML_HEADS, nt, LANES), F32),
        grid=(n_all,),
        in_specs=[pl.BlockSpec((ML_SLABS, CHUNK_ROWS, LANES), lambda i: (0, i, 0)), cspec, mspec, cspec, mspec],
        out_specs=pl.BlockSpec((ML_HEADS, CHUNK_ROWS, LANES), lambda i: (0, i, 0)),
        compiler_params=_cparams("arbitrary"),
        name="mlstm_out",
    )(pa, c0f, m0f, c0b, m0b)


def _rglru_kernel(*refs, direction, final, n_ctx, n_all):
    if final:
        cur_ref, prev_ref, next_ref, hb_ref, cw_ref, cb_ref, wg_ref, bg_ref, lam_ref, out_ref, a_sc, b_sc, h_sc = refs
    else:
        cur_ref, prev_ref, next_ref, cw_ref, cb_ref, wg_ref, bg_ref, lam_ref, out_ref, a_sc, b_sc, h_sc = refs
        hb_ref = None
    i = pl.program_id(0)
    c = _chunk_order(i, direction, n_ctx, n_all)
    w = RG_WIDTH

    @pl.when(i == 0)
    def _():
        h_sc[...] = jnp.zeros(h_sc.shape, F32)

    seg_first = jnp.logical_or(c == 0, c == n_ctx)
    seg_last = jnp.logical_or(c == n_ctx - 1, c == n_all - 1)
    cur = cur_ref[:, :w]
    prev = jnp.where(seg_first, 0.0, prev_ref[:, :w])
    nxt = jnp.where(seg_last, 0.0, next_ref[:, :w])
    xc = jnp.concatenate([prev, cur, nxt], axis=0)
    n = CHUNK_ROWS
    xconv = cb_ref[...] + cw_ref[0:1, :] * xc[0:n]
    for j in range(1, RG_CONV):
        xconv = xconv + cw_ref[j:j + 1, :] * xc[j * BATCH:j * BATCH + n]
    z = _dot(xconv.astype(BF16), wg_ref[...]) + bg_ref[...]
    r = _sigmoid(z[:, :w])
    ig = _sigmoid(z[:, w:])
    log_a = (-RG_C * _softplus(-lam_ref[...])) * r
    a = jnp.exp(log_a)
    a_sc[...] = a
    b_sc[...] = jnp.sqrt(-jnp.tanh(log_a) * (a * a + 1.0)) * (ig * xconv)

    def step(t, h):
        tt = t if direction == 0 else T_CHUNK - 1 - t
        rows = pl.ds(pl.multiple_of(tt * BATCH, BATCH), BATCH)
        h = a_sc[rows, :] * h + b_sc[rows, :]
        b_sc[rows, :] = h
        return h

    h_sc[...] = lax.fori_loop(0, T_CHUNK, step, h_sc[...], unroll=8)
    hs = b_sc[...]
    if final:
        ht = hs + hb_ref[...]
        ss = _dot_xr(ht * ht, _group_ones(w, 6)) * (1.0 / RG_BLOCK_DIM)
        hs = _gelu(cur_ref[:, w:]) * (ht * lax.rsqrt(ss + EPS))
    out_ref[...] = hs


def _rglru_pass(pb, hb, cw, cb, wg, bg, lam, direction, n_ctx):
    nt = pb.shape[0]
    n_all = nt // CHUNK_ROWS
    final = hb is not None
    w2 = 2 * RG_WIDTH
    order = lambda i: _chunk_order(i, direction, n_ctx, n_all)
    halo_p = CHUNK_ROWS // (2 * BATCH)
    halo_n = CHUNK_ROWS // BATCH
    in_specs = [
        pl.BlockSpec((CHUNK_ROWS, w2), lambda i: (order(i), 0)),
        pl.BlockSpec((2 * BATCH, w2), lambda i: (jnp.maximum(order(i) * halo_p - 1, 0), 0)),
        pl.BlockSpec((BATCH, w2), lambda i: (jnp.minimum((order(i) + 1) * halo_n, nt // BATCH - 1), 0)),
    ]
    args = [pb, pb, pb]
    if final:
        in_specs.append(pl.BlockSpec((CHUNK_ROWS, RG_WIDTH), lambda i: (order(i), 0)))
        args.append(hb)
    full = lambda s: pl.BlockSpec(s, lambda i: tuple(0 for _ in s))
    in_specs += [full((RG_CONV, RG_WIDTH)), full((1, RG_WIDTH)), full((RG_WIDTH, w2)), full((1, w2)), full((1, RG_WIDTH))]
    args += [cw, cb, wg, bg, lam]
    return pl.pallas_call(
        functools.partial(_rglru_kernel, direction=direction, final=final, n_ctx=n_ctx, n_all=n_all),
        out_shape=jax.ShapeDtypeStruct((nt, RG_WIDTH), F32),
        grid=(n_all,),
        in_specs=in_specs,
        out_specs=pl.BlockSpec((CHUNK_ROWS, RG_WIDTH), lambda i: (order(i), 0)),
        scratch_shapes=[
            pltpu.VMEM((CHUNK_ROWS, RG_WIDTH), F32),
            pltpu.VMEM((CHUNK_ROWS, RG_WIDTH), F32),
            pltpu.VMEM((BATCH, RG_WIDTH), F32),
        ],
        compiler_params=_cparams("arbitrary"),
        name="rglru_fwd" if final else "rglru_bwd",
    )(*args)


def _s5_kernel(*refs, direction, final):
    if final:
        (u_ref, yb_ref, h0_ref, a_ref, bd_ref, cd_ref, dsk_ref, gw_ref, gb_ref,
         out_ref, hN_ref, v_sc, h_sc) = refs
    else:
        u_ref, h0_ref, a_ref, bd_ref, cd_ref, out_ref, hN_ref, v_sc, h_sc = refs
    i = pl.program_id(0)
    n = S5_NSTATE

    @pl.when(i == 0)
    def _():
        h_sc[...] = h0_ref[...]

    u = u_ref[...]
    v_sc[...] = _dot(u.astype(BF16), bd_ref[...])
    ar = jnp.broadcast_to(a_ref[0:1, :], (BATCH, n))
    ai = jnp.broadcast_to(a_ref[1:2, :], (BATCH, n))

    def step(t, carry):
        hr, hi = carry
        tt = t if direction == 0 else T_CHUNK - 1 - t
        rows = pl.ds(pl.multiple_of(tt * BATCH, BATCH), BATCH)
        nr = ar * hr - ai * hi + v_sc[rows, :n]
        ni = ar * hi + ai * hr + v_sc[rows, n:]
        v_sc[rows, :n] = nr
        v_sc[rows, n:] = ni
        return nr, ni

    hr, hi = lax.fori_loop(0, T_CHUNK, step, (h_sc[:, :n], h_sc[:, n:]))
    h_sc[:, :n] = hr
    h_sc[:, n:] = hi
    hN_ref[...] = h_sc[...]
    y = _dot(v_sc[...].astype(BF16), cd_ref[...])
    if final:
        y = y + yb_ref[...] + dsk_ref[...] * u
        g = _gelu(y)
        o = g * _sigmoid(_dot(g.astype(BF16), gw_ref[...]) + gb_ref[...])
        ss = _dot_xr(o * o, _group_ones(S5_WIDTH, 4)) * (1.0 / S5_GROUP)
        y = o * lax.rsqrt(ss + EPS)
    out_ref[...] = y


def _s5_pass(u, yb, h0, a, bd, cd, dsk, gw, gb, direction):
    nt = u.shape[0]
    n_chunks = nt // CHUNK_ROWS
    final = yb is not None
    order = (lambda i: i) if direction == 0 else (lambda i: n_chunks - 1 - i)
    full = lambda s: pl.BlockSpec(s, lambda i: tuple(0 for _ in s))
    chunk = pl.BlockSpec((CHUNK_ROWS, S5_WIDTH), lambda i: (order(i), 0))
    in_specs = [chunk]
    args = [u]
    if final:
        in_specs.append(chunk)
        args.append(yb)
    in_specs += [full((BATCH, 2 * S5_NSTATE)), full((2, S5_NSTATE)), full((S5_WIDTH, 2 * S5_NSTATE)),
                 full((2 * S5_NSTATE, S5_WIDTH))]
    args += [h0, a, bd, cd]
    if final:
        in_specs += [full((1, S5_WIDTH)), full((S5_WIDTH, S5_WIDTH)), full((1, S5_WIDTH))]
        args += [dsk, gw, gb]
    return pl.pallas_call(
        functools.partial(_s5_kernel, direction=direction, final=final),
        out_shape=(jax.ShapeDtypeStruct((nt, S5_WIDTH), F32), jax.ShapeDtypeStruct((BATCH, 2 * S5_NSTATE), F32)),
        grid=(n_chunks,),
        in_specs=in_specs,
        out_specs=(chunk, full((BATCH, 2 * S5_NSTATE))),
        scratch_shapes=[
            pltpu.VMEM((CHUNK_ROWS, 2 * S5_NSTATE), F32),
            pltpu.VMEM((BATCH, 2 * S5_NSTATE), F32),
        ],
        compiler_params=_cparams("arbitrary"),
        name="s5_fwd" if final else "s5_bwd",
    )(*args)


def _outproj_kernel(x_ref, ya_ref, yb_ref, ycc_ref, ycl_ref, mod_ref, mw_ref, wo_ref, n2_ref, rw_ref, rb_ref,
                    xo_ref, h2_ref, idx_ref, gate_ref, rank_ref, base_ref, cnt_ref, cnt_sc, *, tile0, n_ctx_tiles):
    i = pl.program_id(0)
    tile = i + tile0
    tm = x_ref.shape[0]

    @pl.when(i == 0)
    def _():
        cnt_sc[...] = jnp.zeros(cnt_sc.shape, F32)

    yc = jnp.where(tile < n_ctx_tiles, ycc_ref[...], ycl_ref[...].reshape(tm, S5_WIDTH))
    merged = jnp.concatenate([ya_ref[h] for h in range(ML_HEADS)] + [yb_ref[...], yc], axis=1)
    merged = (merged * mw_ref[...]).astype(BF16)
    proj = _dot(merged, wo_ref[...])
    g1 = mod_ref[0, 0]
    x = x_ref[...].reshape(tm // BATCH, BATCH, D_MODEL) + g1[None] * proj.reshape(tm // BATCH, BATCH, D_MODEL)
    xo_ref[...] = x.reshape(tm, D_MODEL)
    ms = jnp.mean(x * x, axis=-1, keepdims=True)
    xn = (x * lax.rsqrt(ms + EPS)) * n2_ref[...][None]
    h2 = (xn * (1.0 + mod_ref[0, 2][None]) + mod_ref[0, 1][None]).reshape(tm, D_MODEL)
    _store_token_tiles(h2_ref, 0, h2)
    logits = _dot3(rw_ref[...], h2, nt=True) + rb_ref[...]
    eidx = lax.broadcasted_iota(I32, (N_EXPERTS, tm), 0)
    vals, ids, hots = [], [], []
    for _ in range(TOP_K):
        m = jnp.max(logits, axis=0, keepdims=True)
        sel = jnp.min(jnp.where(logits == m, eidx, N_EXPERTS), axis=0, keepdims=True)
        hot = eidx == sel
        logits = jnp.where(hot, -jnp.inf, logits)
        vals.append(m)
        ids.append(sel)
        hots.append(hot)
    ex = [jnp.exp(v - vals[0]) for v in vals]
    tot = ex[0] + ex[1] + ex[2] + ex[3]
    idx_ref[...] = jnp.concatenate(ids, axis=0)
    gate_ref[...] = jnp.concatenate([e / tot for e in ex], axis=0)
    selm = hots[0].astype(F32) + hots[1].astype(F32) + hots[2].astype(F32) + hots[3].astype(F32)
    r = lax.broadcasted_iota(I32, (tm, tm), 0)
    c = lax.broadcasted_iota(I32, (tm, tm), 1)
    before = _dot(selm.astype(BF16), (r < c).astype(BF16)) + cnt_sc[:, 0:1]
    ranks = [jnp.sum(jnp.where(hot, before, 0.0), axis=0, keepdims=True) for hot in hots]
    rank_ref[...] = jnp.concatenate(ranks, axis=0).astype(I32)
    base_ref[0] = cnt_sc[...]
    cnt_sc[...] = cnt_sc[...] + jnp.sum(selm, axis=1, keepdims=True)
    cnt_ref[...] = cnt_sc[...]


def _outproj(x, ya, yb, ycc, ycl, mod, mw, wo, n2, rw, rb, tile0, n_ctx_rows):
    nt = x.shape[0]
    tm = ROW_TILE
    nct = n_ctx_rows // tm
    n_tiles = nt // tm - tile0
    n_out = n_tiles * tm
    rows = ycl.shape[0] // tm
    ycl4 = ycl.reshape(GRID_W, rows, BATCH, S5_WIDTH)
    full = lambda s: pl.BlockSpec(s, lambda i: tuple(0 for _ in s))
    return pl.pallas_call(
        functools.partial(_outproj_kernel, tile0=tile0, n_ctx_tiles=nct),
        out_shape=(
            jax.ShapeDtypeStruct((n_out, D_MODEL), F32),
            jax.ShapeDtypeStruct((n_out * SUBLANES, LANES), F32),
            jax.ShapeDtypeStruct((TOP_K, n_out), I32),
            jax.ShapeDtypeStruct((TOP_K, n_out), F32),
            jax.ShapeDtypeStruct((TOP_K, n_out), I32),
            jax.ShapeDtypeStruct((n_tiles, N_EXPERTS, LANES), F32),
            jax.ShapeDtypeStruct((N_EXPERTS, LANES), F32),
        ),
        grid=(n_tiles,),
        in_specs=[
            pl.BlockSpec((tm, D_MODEL), lambda i: (i + tile0, 0)),
            pl.BlockSpec((ML_HEADS, tm, LANES), lambda i: (0, i + tile0, 0)),
            pl.BlockSpec((tm, RG_WIDTH), lambda i: (i + tile0, 0)),
            pl.BlockSpec((tm, S5_WIDTH), lambda i: (jnp.minimum(i + tile0, nct - 1), 0)),
            pl.BlockSpec((GRID_W, 1, BATCH, S5_WIDTH), lambda i: (0, jnp.maximum(i + tile0 - nct, 0), 0, 0)),
            pl.BlockSpec((1, 3, BATCH, D_MODEL), lambda i: (jnp.where(i + tile0 < nct, 0, 1), 0, 0, 0)),
            full((1, MERGED)),
            full((MERGED, D_MODEL)),
            full((1, D_MODEL)),
            full((N_EXPERTS, D_MODEL)),
            full((N_EXPERTS, 1)),
        ],
        out_specs=(
            pl.BlockSpec((tm, D_MODEL), lambda i: (i, 0)),
            pl.BlockSpec((tm * SUBLANES, LANES), lambda i: (i, 0)),
            pl.BlockSpec((TOP_K, tm), lambda i: (0, i)),
            pl.BlockSpec((TOP_K, tm), lambda i: (0, i)),
            pl.BlockSpec((TOP_K, tm), lambda i: (0, i)),
            pl.BlockSpec((1, N_EXPERTS, LANES), lambda i: (i, 0, 0)),
            full((N_EXPERTS, LANES)),
        ),
        scratch_shapes=[pltpu.VMEM((N_EXPERTS, LANES), F32)],
        compiler_params=_cparams("arbitrary"),
        name="outproj_router",
    )(x, ya, yb, ycc, ycl4, mod, mw, wo, n2, rw, rb)


STAGE_ROWS = TOP_K * ROW_TILE * SUBLANES


def _start_run_copies(hbm, stage, runs_ref, sem, to_hbm):
    def per_expert(e, carry):
        row = runs_ref[0, 0, e]
        length = runs_ref[0, 0, N_EXPERTS + e]
        off = runs_ref[0, 0, 2 * N_EXPERTS + e]
        for bit in range(RUN_BITS):
            size = (1 << bit) * SUBLANES

            @pl.when((lax.shift_right_logical(length, bit) & 1) == 1)
            def _():
                done = lax.shift_left(lax.shift_right_logical(length, bit + 1), bit + 1)
                h = hbm.at[pl.ds(pl.multiple_of((row + done) * SUBLANES, SUBLANES), size), :]
                s = stage.at[pl.ds(pl.multiple_of((off + done) * SUBLANES, SUBLANES), size), :]
                cp = pltpu.make_async_copy(s, h, sem) if to_hbm else pltpu.make_async_copy(h, s, sem)
                cp.start(priority=bit % DMA_PRIORITIES)
        return carry

    lax.fori_loop(0, N_EXPERTS, per_expert, 0)


def _wait_run_copies(hbm, stage, sem, to_hbm):
    h = hbm.at[pl.ds(0, STAGE_ROWS), :]
    cp = pltpu.make_async_copy(stage, h, sem) if to_hbm else pltpu.make_async_copy(h, stage, sem)
    cp.wait()


def _dispatch_kernel(pe_ref, pd_ref, nu_ref, runs_ref, loc_ref, h_ref, xs_hbm, stage, zero_sc, sem, zsem, *,
                     n_blocks, n_tiles):
    i = pl.program_id(0)
    tm = ROW_TILE

    def zero_block(blk):
        start = pl.multiple_of(blk * (MOE_BM * SUBLANES), SUBLANES)
        return pltpu.make_async_copy(zero_sc, xs_hbm.at[pl.ds(start, MOE_BM * SUBLANES), :], zsem)

    @pl.when(i == 0)
    def _():
        zero_sc[...] = jnp.zeros(zero_sc.shape, F32)
        for e in range(N_EXPERTS):
            @pl.when(pd_ref[e] > 0)
            def _():
                zero_block(pe_ref[e] // MOE_BM - 1).start()
        for e in range(N_EXPERTS):
            @pl.when(pd_ref[e] > 0)
            def _():
                zero_block(0).wait()

        def fill(blk, carry):
            cp = zero_block(blk)
            cp.start()
            cp.wait()
            return carry

        lax.fori_loop(nu_ref[0], n_blocks, fill, 0)

    buf = lax.rem(i, 2)

    def place(r, carry):
        t = h_ref[pl.ds(pl.multiple_of(r * SUBLANES, SUBLANES), SUBLANES), :]
        for k in range(TOP_K):
            slot = loc_ref[0, 0, k * tm + r]
            stage[buf, pl.ds(pl.multiple_of(slot * SUBLANES, SUBLANES), SUBLANES), :] = t
        return carry

    lax.fori_loop(0, tm, place, 0, unroll=4)
    _start_run_copies(xs_hbm, stage.at[buf], runs_ref, sem.at[buf], to_hbm=True)

    @pl.when(i > 0)
    def _():
        _wait_run_copies(xs_hbm, stage.at[1 - buf], sem.at[1 - buf], to_hbm=True)

    @pl.when(i == n_tiles - 1)
    def _():
        _wait_run_copies(xs_hbm, stage.at[buf], sem.at[buf], to_hbm=True)


def _moe_dispatch(pad_ends, padded, n_used, runs, loc, h2t, n_blocks):
    tm = ROW_TILE
    n_tiles = runs.shape[0]
    smem = lambda n: pl.BlockSpec((1, 1, n), lambda i, pe, pd, nu: (i, 0, 0), memory_space=pltpu.SMEM)
    grid_spec = pltpu.PrefetchScalarGridSpec(
        num_scalar_prefetch=3,
        grid=(n_tiles,),
        in_specs=[
            smem(3 * N_EXPERTS),
            smem(TOP_K * tm),
            pl.BlockSpec((tm * SUBLANES, LANES), lambda i, pe, pd, nu: (i, 0)),
        ],
        out_specs=pl.BlockSpec(memory_space=pl.ANY),
        scratch_shapes=[
            pltpu.VMEM((2, STAGE_ROWS, LANES), F32),
            pltpu.VMEM((MOE_BM * SUBLANES, LANES), F32),
            pltpu.SemaphoreType.DMA((2,)),
            pltpu.SemaphoreType.DMA(()),
        ],
    )
    return pl.pallas_call(
        functools.partial(_dispatch_kernel, n_blocks=n_blocks, n_tiles=n_tiles),
        out_shape=jax.ShapeDtypeStruct((n_blocks * MOE_BM * SUBLANES, LANES), F32),
        grid_spec=grid_spec,
        compiler_params=_cparams("arbitrary"),
        name="moe_dispatch",
    )(pad_ends, padded, n_used, runs, loc, h2t)


def _ffn_kernel(be_ref, nu_ref, x_ref, wgu_ref, bgu_ref, wd_ref, bd_ref, y_ref, wgu_sc, wd_sc):
    i = pl.program_id(0)
    changed = jnp.logical_or(i == 0, be_ref[i] != be_ref[jnp.maximum(i - 1, 0)])

    @pl.when(changed)
    def _():
        wgu_sc[...] = wgu_ref[0].astype(BF16)
        wd_sc[...] = wd_ref[0].astype(BF16)

    @pl.when(i < nu_ref[0])
    def _():
        x = _load_token_tiles(x_ref, 0, MOE_BM).astype(BF16)
        gu = _dot(x, wgu_sc[...]) + bgu_ref[0]
        gate = jnp.minimum(gu[:, :D_EXPERT], SWIGLU_LIMIT)
        up = jnp.clip(gu[:, D_EXPERT:], -SWIGLU_LIMIT, SWIGLU_LIMIT)
        glu = gate * _sigmoid(SWIGLU_ALPHA * gate)
        act = ((up + 1.0) * glu).astype(BF16)
        _store_token_tiles(y_ref, 0, _dot(act, wd_sc[...]) + bd_ref[0])

    @pl.when(i >= nu_ref[0])
    def _():
        y_ref[...] = jnp.zeros(y_ref.shape, F32)


def _moe_ffn(block_e, n_used, xs, wgu, bgu, wd, bd):
    rows = MOE_BM * SUBLANES
    n_blocks = xs.shape[0] // rows
    ne = wgu.shape[0]
    grid_spec = pltpu.PrefetchScalarGridSpec(
        num_scalar_prefetch=2,
        grid=(n_blocks,),
        in_specs=[
            pl.BlockSpec((rows, LANES), lambda i, be, nu: (jnp.minimum(i, nu[0] - 1), 0)),
            pl.BlockSpec((1, D_MODEL, 2 * D_EXPERT), lambda i, be, nu: (be[i], 0, 0)),
            pl.BlockSpec((1, 1, 2 * D_EXPERT), lambda i, be, nu: (be[i], 0, 0)),
            pl.BlockSpec((1, D_EXPERT, D_MODEL), lambda i, be, nu: (be[i], 0, 0)),
            pl.BlockSpec((1, 1, D_MODEL), lambda i, be, nu: (be[i], 0, 0)),
        ],
        out_specs=pl.BlockSpec((rows, LANES), lambda i, be, nu: (i, 0)),
        scratch_shapes=[
            pltpu.VMEM((D_MODEL, 2 * D_EXPERT), BF16),
            pltpu.VMEM((D_EXPERT, D_MODEL), BF16),
        ],
    )
    return pl.pallas_call(
        _ffn_kernel,
        out_shape=jax.ShapeDtypeStruct((n_blocks * rows, LANES), F32),
        grid_spec=grid_spec,
        compiler_params=_cparams("arbitrary"),
        name="moe_ffn",
    )(block_e, n_used, xs, wgu, bgu.reshape(ne, 1, -1), wd, bd.reshape(ne, 1, -1))


def _combine_kernel(runs_ref, next_runs_ref, loc_ref, gate_ref, x_ref, mod_ref, fw_ref, ys_hbm, out_ref, stage, comb,
                    sem, *, final_norm, n_tiles):
    i = pl.program_id(0)
    tm = x_ref.shape[0]
    buf = lax.rem(i, 2)

    @pl.when(i == 0)
    def _():
        _start_run_copies(ys_hbm, stage.at[0], runs_ref, sem.at[0], to_hbm=False)

    @pl.when(i + 1 < n_tiles)
    def _():
        _start_run_copies(ys_hbm, stage.at[1 - buf], next_runs_ref, sem.at[1 - buf], to_hbm=False)

    _wait_run_copies(ys_hbm, stage.at[buf], sem.at[buf], to_hbm=False)

    def gather(r, carry):
        acc = None
        for k in range(TOP_K):
            slot = loc_ref[0, 0, k * tm + r]
            t = gate_ref[0, 0, k * tm + r] * stage[buf, pl.ds(pl.multiple_of(slot * SUBLANES, SUBLANES), SUBLANES), :]
            acc = t if acc is None else acc + t
        comb[pl.ds(pl.multiple_of(r * SUBLANES, SUBLANES), SUBLANES), :] = acc
        return carry

    lax.fori_loop(0, tm, gather, 0, unroll=4)
    f = _load_token_tiles(comb, 0, tm)
    x = x_ref[...].reshape(tm // BATCH, BATCH, D_MODEL) + mod_ref[0][None] * f.reshape(tm // BATCH, BATCH, D_MODEL)
    if final_norm:
        ms = jnp.mean(x * x, axis=-1, keepdims=True)
        x = (x * lax.rsqrt(ms + EPS)) * fw_ref[...][None]
    out_ref[...] = x.reshape(tm, D_MODEL)


def _moe_combine(runs, loc, gates_r, x, mod, fw, ys, n_ctx_rows, final_norm):
    nt = x.shape[0]
    tm = ROW_TILE
    n_tiles = nt // tm
    nct = n_ctx_rows // tm
    smem = lambda n: pl.BlockSpec((1, 1, n), lambda i: (i, 0, 0), memory_space=pltpu.SMEM)
    return pl.pallas_call(
        functools.partial(_combine_kernel, final_norm=final_norm, n_tiles=n_tiles),
        out_shape=jax.ShapeDtypeStruct((nt, D_MODEL), F32),
        grid=(n_tiles,),
        in_specs=[
            smem(3 * N_EXPERTS),
            pl.BlockSpec((1, 1, 3 * N_EXPERTS), lambda i: (jnp.minimum(i + 1, n_tiles - 1), 0, 0),
                         memory_space=pltpu.SMEM),
            smem(TOP_K * tm),
            smem(TOP_K * tm),
            pl.BlockSpec((tm, D_MODEL), lambda i: (i, 0)),
            pl.BlockSpec((1, BATCH, D_MODEL), lambda i: (jnp.where(i < nct, 0, 1), 0, 0)),
            pl.BlockSpec((1, D_MODEL), lambda i: (0, 0)),
            pl.BlockSpec(memory_space=pl.ANY),
        ],
        out_specs=pl.BlockSpec((tm, D_MODEL), lambda i: (i, 0)),
        scratch_shapes=[
            pltpu.VMEM((2, STAGE_ROWS, LANES), F32),
            pltpu.VMEM((tm * SUBLANES, LANES), F32),
            pltpu.SemaphoreType.DMA((2,)),
        ],
        compiler_params=_cparams("arbitrary"),
        name="moe_combine",
    )(runs, runs, loc, gates_r, x, mod, fw, ys)


def _moe(h2t, idx, gates, rank, tile_base, counts, x, g2, fw, layer, wgu, bgu, wd, bd, n_ctx_rows, final_norm):
    nt = idx.shape[1]
    bm = MOE_BM
    tm = ROW_TILE
    n_tiles = nt // tm
    n_blocks = (nt * TOP_K) // bm + N_EXPERTS
    cnt = counts[:, 0].astype(I32)
    padded = (cnt + bm - 1) // bm * bm
    pad_ends = jnp.cumsum(padded)
    pad_starts = pad_ends - padded
    experts = jnp.arange(N_EXPERTS, dtype=I32)
    base = tile_base[:, :, 0].astype(I32)
    run_len = jnp.concatenate([base[1:], cnt[None]], axis=0) - base
    run_off = jnp.cumsum(run_len, axis=1) - run_len
    run_row = pad_starts[None, :] + base
    runs = jnp.concatenate([run_row, run_len, run_off], axis=1).reshape(n_tiles, 1, 3 * N_EXPERTS)
    delta = (run_off - base).T
    idx3 = idx.reshape(TOP_K, n_tiles, tm)
    loc = rank.reshape(TOP_K, n_tiles, tm) + jnp.sum(
        jnp.where(idx3[None] == experts[:, None, None, None], delta[:, None, :, None], 0), axis=0)
    by_tile = lambda a: a.transpose(1, 0, 2).reshape(n_tiles, 1, TOP_K * tm)
    loc = by_tile(loc)
    block_start = jnp.arange(n_blocks, dtype=I32) * bm
    block_e = jnp.minimum(jnp.sum((pad_ends[None, :] <= block_start[:, None]).astype(I32), axis=1), N_EXPERTS - 1)
    n_used = (pad_ends[-1] // bm).astype(I32).reshape(1)
    xs = _moe_dispatch(pad_ends.astype(I32), padded, n_used, runs, loc, h2t, n_blocks)
    ys = _moe_ffn(block_e + layer * N_EXPERTS, n_used, xs, wgu, bgu, wd, bd)
    return _moe_combine(runs, loc, by_tile(gates.reshape(TOP_K, n_tiles, tm)), x, g2, fw, ys, n_ctx_rows, final_norm)


def _pad_heads(a):
    lead = a.shape[:-1]
    a = a.reshape(*lead, 4 * ML_HEADS, ML_HEAD_DIM)
    a = jnp.pad(a, [(0, 0)] * len(lead) + [(0, 0), (0, LANES - ML_HEAD_DIM)])
    return a.reshape(*lead, 4 * ML_HEADS * LANES)


def _kvqo(a):
    w = ML_HEADS * ML_HEAD_DIM
    return jnp.concatenate([a[..., w:2 * w], a[..., 2 * w:3 * w], a[..., :w], a[..., 3 * w:4 * w]], axis=-1)


def _block_diag(blocks):
    return jax.scipy.linalg.block_diag(*[blocks[g] for g in range(blocks.shape[0])])


def _s5_discretise(lam_re, lam_im, log_dt, b_re, b_im):
    dt = jnp.exp(log_dt)[:, None]
    mag = jnp.exp(lam_re * dt)
    ar, ai = mag * jnp.cos(lam_im * dt), mag * jnp.sin(lam_im * dt)
    den = lam_re * lam_re + lam_im * lam_im
    cr = ((ar - 1.0) * lam_re + ai * lam_im) / den
    ci = (ai * lam_re - (ar - 1.0) * lam_im) / den
    bbr = cr[..., None] * b_re - ci[..., None] * b_im
    bbi = cr[..., None] * b_im + ci[..., None] * b_re
    return ar, ai, bbr, bbi


def _layer_layouts(w_in, b_in, rg_wa, rg_ba, rg_wx, rg_bx, s5_lambda_re, s5_lambda_im, s5_log_dt, s5_b_re, s5_b_im,
                   s5_c_re, s5_c_im, mix_norm_w, w_out):
    dm = w_in.shape[0]
    g0 = 4 * ML_HEADS * ML_HEAD_DIM
    ng = 2 * ML_HEADS
    kscale = jnp.ones((4, ML_HEADS * LANES), F32).at[0].set(ML_HEAD_DIM ** -0.5).reshape(-1)
    ones_col = jnp.zeros((4, ML_HEADS, LANES), F32).at[1, :, ML_HEAD_DIM].set(1.0).reshape(-1)
    w_cat = jnp.concatenate([
        _pad_heads(_kvqo(w_in[:, :g0])) * kscale,
        jnp.pad(w_in[:, g0:g0 + ng], ((0, 0), (0, LANES - ng))),
        jnp.pad(w_in[:, g0 + ng:g0 + 2 * ng], ((0, 0), (0, LANES - ng))),
        w_in[:, g0 + 2 * ng:]], axis=1).astype(BF16)
    b_cat = jnp.concatenate([
        _pad_heads(_kvqo(b_in[:g0])) * kscale + ones_col,
        jnp.pad(b_in[g0:g0 + ng], (0, LANES - ng)),
        jnp.pad(b_in[g0 + ng:g0 + 2 * ng], (0, LANES - ng)),
        b_in[g0 + 2 * ng:]])[None]
    wg = [jnp.concatenate([_block_diag(rg_wa[d]), _block_diag(rg_wx[d])], axis=1).astype(BF16) for d in range(2)]
    bg = [jnp.concatenate([rg_ba[d], rg_bx[d]])[None] for d in range(2)]
    s5p = []
    for d in range(2):
        ar, ai, bbr, bbi = _s5_discretise(s5_lambda_re[d], s5_lambda_im[d], s5_log_dt[d], s5_b_re, s5_b_im)
        a = jnp.stack([ar.reshape(-1), ai.reshape(-1)])
        bd = jnp.concatenate([_block_diag(bbr.transpose(0, 2, 1)), _block_diag(bbi.transpose(0, 2, 1))],
                             axis=1).astype(BF16)
        cd = jnp.concatenate([_block_diag(s5_c_re.transpose(0, 2, 1)),
                              -_block_diag(s5_c_im.transpose(0, 2, 1))], axis=0).astype(BF16)
        s5p.append((a, bd, cd))
    wa = ML_HEADS * ML_HEAD_DIM
    mw_a = jnp.pad(mix_norm_w[:wa].reshape(ML_HEADS, ML_HEAD_DIM), ((0, 0), (0, LANES - ML_HEAD_DIM))).reshape(-1)
    mw = jnp.concatenate([mw_a, mix_norm_w[wa:]])[None]
    wo_a = jnp.pad(w_out[:wa].reshape(ML_HEADS, ML_HEAD_DIM, dm), ((0, 0), (0, LANES - ML_HEAD_DIM), (0, 0)))
    wo = jnp.concatenate([wo_a.reshape(ML_HEADS * LANES, dm), w_out[wa:]], axis=0).astype(BF16)
    return w_cat, b_cat, wg, bg, s5p, mw, wo


def kernel(x, c, ctx, c_ctx, ada_w, ada_b, norm1_w, w_in, b_in, rg_conv_w, rg_conv_b, rg_wa, rg_ba, rg_wx, rg_bx, rg_lambda, s5_lambda_re, s5_lambda_im, s5_log_dt, s5_b_re, s5_b_im, s5_c_re, s5_c_im, s5_d, s5_glu_w, s5_glu_b, mix_norm_w, w_out, norm2_w, router_w, router_b, moe_w_gate_up, moe_b_gate_up, moe_w_down, moe_b_down, final_norm_w):
    bsz, seq, dm = x.shape
    sc = ctx.shape[1]
    assert bsz == BATCH and dm == D_MODEL and seq % GRID_W == 0
    n_ctx_rows = sc * BATCH
    n_lat_rows = seq * BATCH
    assert n_ctx_rows % CHUNK_ROWS == 0 and n_lat_rows % CHUNK_ROWS == 0 and GRID_W * BATCH == ROW_TILE
    n_ctx_chunks = n_ctx_rows // CHUNK_ROWS
    depth = ada_w.shape[0]

    xs = jnp.concatenate([ctx.transpose(1, 0, 2).reshape(n_ctx_rows, dm),
                          x.transpose(1, 0, 2).reshape(n_lat_rows, dm)], axis=0)

    c_rows = jnp.concatenate([c, c_ctx[None], jnp.zeros((16 - bsz - 1, dm), F32)], axis=0)
    mods = _modulation(c_rows, ada_w, ada_b)
    mods = mods.reshape(depth, 16, 6, dm)
    mod_lat = mods[:, :bsz].transpose(0, 2, 1, 3)
    mod_ctx = jnp.broadcast_to(mods[:, bsz][:, :, None, :], mod_lat.shape)
    mod = jnp.stack([mod_ctx, mod_lat], axis=1)

    layouts = jax.vmap(_layer_layouts)(w_in, b_in, rg_wa, rg_ba, rg_wx, rg_bx, s5_lambda_re, s5_lambda_im, s5_log_dt,
                                       s5_b_re, s5_b_im, s5_c_re, s5_c_im, mix_norm_w, w_out)

    for l in range(depth):
        with_ctx = l < depth - 1
        w_cat, b_cat, wg, bg, s5p, mw, wo = jax.tree.map(lambda a: a[l], layouts)

        pa, pb, pcc, pcl = _inproj(xs, mod[l, :, 0:2], norm1_w[l][None], w_cat, b_cat, n_ctx_rows)
        pcl = pcl.reshape(n_lat_rows, S5_WIDTH)
        c0f, m0f = _mlstm_states(pa, 0, n_ctx_chunks)
        c0b, m0b = _mlstm_states(pa, 1, n_ctx_chunks)
        ya = _mlstm_outputs(pa, c0f, m0f, c0b, m0b)
        rgb = _rglru_pass(pb, None, rg_conv_w[l], rg_conv_b[l][None], wg[1], bg[1], rg_lambda[l, 1][None], 1, n_ctx_chunks)
        yb = _rglru_pass(pb, rgb, rg_conv_w[l], rg_conv_b[l][None], wg[0], bg[0], rg_lambda[l, 0][None], 0, n_ctx_chunks)
        zero_state = jnp.zeros((BATCH, 2 * S5_NSTATE), F32)
        a1, bd1, cd1 = s5p[1]
        a0, bd0, cd0 = s5p[0]
        dsk, gw, gb = s5_d[l][None], s5_glu_w[l].astype(BF16), s5_glu_b[l][None]
        ycb_c, st = _s5_pass(pcc, None, zero_state, a1, bd1, cd1, None, None, None, 1)
        ycb_l, _ = _s5_pass(pcl, None, st, a1, bd1, cd1, None, None, None, 1)
        yc_c, st = _s5_pass(pcc, ycb_c, zero_state, a0, bd0, cd0, dsk, gw, gb, 0)
        yc_l, _ = _s5_pass(pcl, ycb_l, st, a0, bd0, cd0, dsk, gw, gb, 0)

        tile0 = 0 if with_ctx else n_ctx_rows // ROW_TILE
        xo, h2, idx, gates, rank, tile_base, counts = _outproj(
            xs, ya, yb, yc_c, yc_l, mod[l, :, 2:5], mw, wo, norm2_w[l][None], router_w[l].T,
            router_b[l][:, None], tile0, n_ctx_rows)

        xs_new = _moe(h2, idx, gates, rank, tile_base, counts, xo, mod[l, :, 5], final_norm_w[None], l,
                      moe_w_gate_up.reshape(-1, dm, 2 * D_EXPERT), moe_b_gate_up.reshape(-1, 2 * D_EXPERT),
                      moe_w_down.reshape(-1, D_EXPERT, dm), moe_b_down.reshape(-1, dm),
                      n_ctx_rows if with_ctx else 0, l == depth - 1)
        xs = xs_new

    out = xs.reshape(seq, bsz, dm).transpose(1, 0, 2)
    return out
```

```python
import functools

import jax
import jax.numpy as jnp
from jax import lax
from jax.experimental import pallas as pl
from jax.experimental.pallas import tpu as pltpu

F32 = jnp.float32
BF16 = jnp.bfloat16
I32 = jnp.int32

D_MODEL = 1024
BATCH = 8
DEPTH = 4
GRID_W = 64
ML_HEADS = 4
ML_HEAD_DIM = 96
ML_CHUNK = 128
ML_M_INIT = -1e30
RG_BLOCKS = 6
RG_BLOCK_DIM = 64
RG_WIDTH = RG_BLOCKS * RG_BLOCK_DIM
RG_CONV = 4
RG_C = 8.0
S5_GROUPS = 16
S5_GROUP = 16
S5_WIDTH = S5_GROUPS * S5_GROUP
S5_STATE = 64
S5_NSTATE = S5_GROUPS * S5_STATE
N_EXPERTS = 32
TOP_K = 4
D_EXPERT = D_MODEL
SWIGLU_LIMIT = 7.0
SWIGLU_ALPHA = 1.702
EPS = 1e-6

LANES = 128
SUBLANES = 8

ML_K, ML_V, ML_Q, ML_O = 0, ML_HEADS, 2 * ML_HEADS, 3 * ML_HEADS
ML_IG = 4 * ML_HEADS
ML_FG = ML_IG + 1
ML_SLABS = ML_FG + 1
ML_STATE_SLABS = 2 * ML_HEADS
A_PAD = ML_SLABS * LANES
MERGED = ML_HEADS * LANES + RG_WIDTH + S5_WIDTH

ROW_TILE = 512
T_CHUNK = 128
CHUNK_ROWS = T_CHUNK * BATCH
MOE_BM = 512
DMA_PRIORITIES = 2
RUN_BITS = ROW_TILE.bit_length()
VMEM_LIMIT = 56 * 1024 * 1024


def _cparams(*sem):
    return pltpu.CompilerParams(dimension_semantics=sem, vmem_limit_bytes=VMEM_LIMIT)


def _sigmoid(x):
    return 0.5 * (jnp.tanh(0.5 * x) + 1.0)


def _log_sigmoid(x):
    return jnp.minimum(x, 0.0) - jnp.log1p(jnp.exp(-jnp.abs(x)))


def _softplus(x):
    return jnp.maximum(x, 0.0) + jnp.log1p(jnp.exp(-jnp.abs(x)))


def _gelu(x):
    return 0.5 * x * (1.0 + jnp.tanh(0.7978845608028654 * (x + 0.044715 * (x * x * x))))


def _dot(a, b):
    return jnp.dot(a, b, preferred_element_type=F32)


def _dot_nt(a, b):
    return lax.dot_general(a, b, (((1,), (1,)), ((), ())), preferred_element_type=F32)


def _split(a):
    hi = a.astype(BF16)
    lo = (a - hi.astype(F32)).astype(BF16)
    return hi, lo


def _dot_lx(a_exact, b):
    hi, lo = _split(b)
    return _dot(a_exact, hi) + _dot(a_exact, lo)


def _dot_xr(a, b_exact):
    hi, lo = _split(a)
    return _dot(hi, b_exact) + _dot(lo, b_exact)


def _dot3(a, b, nt=False):
    ah, al = _split(a)
    bh, bl = _split(b)
    d = _dot_nt if nt else _dot
    return d(ah, bh) + (d(ah, bl) + d(al, bh))


def _group_ones(n, shift):
    r = lax.broadcasted_iota(I32, (n, n), 0)
    c = lax.broadcasted_iota(I32, (n, n), 1)
    return (lax.shift_right_logical(r, shift) == lax.shift_right_logical(c, shift)).astype(BF16)


def _store_token_tiles(ref, tok0, val):
    n, width = val.shape
    rows = width // LANES
    for s in range(rows):
        ref[pl.ds(tok0 * rows + s, n, stride=rows), :] = val[:, s * LANES:(s + 1) * LANES]


def _load_token_tiles(ref, tok0, n, rows=SUBLANES):
    return jnp.concatenate([ref[pl.ds(tok0 * rows + s, n, stride=rows), :] for s in range(rows)], axis=1)


def _pack_bf16_pairs(x):
    w = x.shape[1] // 2
    bits = lax.bitcast_convert_type(x.astype(BF16).astype(F32), I32)
    return (bits[:, :w] & jnp.int32(-65536)) | lax.shift_right_logical(bits[:, w:], 16)


def _unpack_bf16_pairs(words):
    hi = lax.bitcast_convert_type(words & jnp.int32(-65536), F32)
    lo = lax.bitcast_convert_type(lax.shift_left(words, 16), F32)
    return jnp.concatenate([hi, lo], axis=1).astype(BF16)


def _mod_kernel(c_ref, w_ref, b_ref, o_ref):
    c = c_ref[...]
    s = c * _sigmoid(c)
    o_ref[0] = _dot3(s, w_ref[0]) + b_ref[0]


def _modulation(c_rows, ada_w, ada_b):
    depth, d, n = ada_w.shape
    tn = 1536
    return pl.pallas_call(
        _mod_kernel,
        out_shape=jax.ShapeDtypeStruct((depth, 16, n), F32),
        grid=(depth, n // tn),
        in_specs=[
            pl.BlockSpec((16, d), lambda l, j: (0, 0)),
            pl.BlockSpec((1, d, tn), lambda l, j: (l, 0, j)),
            pl.BlockSpec((1, 1, tn), lambda l, j: (l, 0, j)),
        ],
        out_specs=pl.BlockSpec((1, 16, tn), lambda l, j: (l, 0, j)),
        compiler_params=_cparams("arbitrary", "arbitrary"),
        name="adaln_mod",
    )(c_rows, ada_w, ada_b.reshape(depth, 1, n))


def _inproj_kernel(x_ref, mod_ref, nw_ref, w_ref, b_ref, pa_ref, pb_ref, pcc_ref, pcl_ref, *, n_ctx_tiles):
    i = pl.program_id(0)
    x = x_ref[...]
    tm = x.shape[0]
    ms = jnp.mean(x * x, axis=-1, keepdims=True)
    xn = (x * lax.rsqrt(ms + EPS)) * nw_ref[...]
    xn = xn.reshape(tm // BATCH, BATCH, D_MODEL)
    h = xn * (1.0 + mod_ref[0, 1][None]) + mod_ref[0, 0][None]
    h = h.reshape(tm, D_MODEL).astype(BF16)
    for j in range(0, ML_SLABS, 2):
        p = _dot(h, w_ref[:, j * LANES:(j + 2) * LANES]) + b_ref[:, j * LANES:(j + 2) * LANES]
        pa_ref[j] = p[:, :LANES]
        pa_ref[j + 1] = p[:, LANES:]
    c0 = A_PAD
    pb_ref[...] = _dot(h, w_ref[:, c0:c0 + 2 * RG_WIDTH]) + b_ref[:, c0:c0 + 2 * RG_WIDTH]
    c0 = A_PAD + 2 * RG_WIDTH
    pc = _dot(h, w_ref[:, c0:c0 + S5_WIDTH]) + b_ref[:, c0:c0 + S5_WIDTH]

    @pl.when(i < n_ctx_tiles)
    def _():
        pcc_ref[...] = pc

    @pl.when(i >= n_ctx_tiles)
    def _():
        pcl_ref[...] = pc.reshape(GRID_W, 1, BATCH, S5_WIDTH)


def _inproj(x, mod, nw, w, b, n_ctx_rows):
    nt = x.shape[0]
    tm = ROW_TILE
    nct = n_ctx_rows // tm
    n_lat_rows = nt - n_ctx_rows
    rows = n_lat_rows // tm
    ncols = w.shape[1]
    return pl.pallas_call(
        functools.partial(_inproj_kernel, n_ctx_tiles=nct),
        out_shape=(
            jax.ShapeDtypeStruct((ML_SLABS, nt, LANES), F32),
            jax.ShapeDtypeStruct((nt, 2 * RG_WIDTH), F32),
            jax.ShapeDtypeStruct((n_ctx_rows, S5_WIDTH), F32),
            jax.ShapeDtypeStruct((GRID_W, rows, BATCH, S5_WIDTH), F32),
        ),
        grid=(nt // tm,),
        in_specs=[
            pl.BlockSpec((tm, D_MODEL), lambda i: (i, 0)),
            pl.BlockSpec((1, 2, BATCH, D_MODEL), lambda i: (jnp.where(i < nct, 0, 1), 0, 0, 0)),
            pl.BlockSpec((1, D_MODEL), lambda i: (0, 0)),
            pl.BlockSpec((D_MODEL, ncols), lambda i: (0, 0)),
            pl.BlockSpec((1, ncols), lambda i: (0, 0)),
        ],
        out_specs=(
            pl.BlockSpec((ML_SLABS, tm, LANES), lambda i: (0, i, 0)),
            pl.BlockSpec((tm, 2 * RG_WIDTH), lambda i: (i, 0)),
            pl.BlockSpec((tm, S5_WIDTH), lambda i: (jnp.minimum(i, nct - 1), 0)),
            pl.BlockSpec((GRID_W, 1, BATCH, S5_WIDTH), lambda i: (0, jnp.maximum(i - nct, 0), 0, 0)),
        ),
        compiler_params=_cparams("arbitrary"),
        name="inproj",
    )(x, mod, nw, w, b)


def _mlstm_gates(ig, fg):
    L = ML_CHUNK
    row = lax.broadcasted_iota(I32, (L, L), 0)
    col = lax.broadcasted_iota(I32, (L, L), 1)
    lf = _log_sigmoid(fg)
    b_fwd = _dot_lx((row >= col).astype(BF16), lf)
    b_tot = _dot_lx(jnp.ones((L, L), BF16), lf)
    b_bwd = b_tot - b_fwd + lf
    bcol = jnp.where(col < ML_HEADS, b_fwd, b_bwd)
    return ig - bcol, bcol, b_tot


def _mlstm_state_kernel(kv_ref, g_ref, c0_ref, m0_ref, c_sc, m_sc, *, direction):
    i = pl.program_id(0)
    L = ML_CHUNK

    @pl.when(i == 0)
    def _():
        c_sc[...] = jnp.zeros(c_sc.shape, F32)
        m_sc[...] = jnp.full(m_sc.shape, ML_M_INIT, F32)

    def per_batch(b, carry):
        rows = pl.ds(b, L, stride=BATCH)
        z, _, b_tot = _mlstm_gates(g_ref[0, rows, :], g_ref[1, rows, :])
        z_t = z.T
        for h in range(ML_HEADS):
            j = ML_HEADS * direction + h
            idx = b * ML_HEADS + h
            k = kv_ref[ML_K + h, rows, :].astype(BF16)
            v_t = kv_ref[ML_V + h, rows, :].T
            c0 = c_sc[idx]
            m0_tile = m_sc[idx]
            c0_ref[0, idx] = c0.astype(BF16)
            m0_ref[0, idx] = m0_tile
            m0 = m0_tile[0:1, :]
            bt = jnp.broadcast_to(b_tot[0:1, j:j + 1], (1, LANES))
            w = bt + z_t[j:j + 1, :]
            mloc = jnp.broadcast_to(jnp.max(w, axis=1, keepdims=True), (1, LANES))
            cloc = _dot((v_t * jnp.exp(w - mloc)).astype(BF16), k)
            mnew = jnp.maximum(bt + m0, mloc)
            a = jnp.exp(bt + m0 - mnew)
            sc = jnp.exp(mloc - mnew)
            c_sc[idx] = a * c0 + sc * cloc
            m_sc[idx] = jnp.broadcast_to(mnew, (SUBLANES, LANES))
        return carry

    lax.fori_loop(0, BATCH, per_batch, 0, unroll=4)


def _mlstm_out_kernel(pa_ref, c0f_ref, m0f_ref, c0b_ref, m0b_ref, out_ref):
    L = ML_CHUNK
    row = lax.broadcasted_iota(I32, (L, L), 0)
    col = lax.broadcasted_iota(I32, (L, L), 1)
    masks = (row <= col, row >= col)
    head_row = row < ML_HEAD_DIM
    m0_refs = (m0f_ref, m0b_ref)

    def per_batch(b, carry):
        rows = pl.ds(b, L, stride=BATCH)
        z, bcol, _ = _mlstm_gates(pa_ref[ML_IG, rows, :], pa_ref[ML_FG, rows, :])
        bcol_t = bcol.T
        for h in range(ML_HEADS):
            idx = b * ML_HEADS + h
            q = pa_ref[ML_Q + h, rows, :].astype(BF16)
            k = pa_ref[ML_K + h, rows, :].astype(BF16)
            v_t = pa_ref[ML_V + h, rows, :].T.astype(BF16)
            s_t = _dot_nt(k, q)
            nc_t = _dot_nt(jnp.concatenate([c0f_ref[0, idx], c0b_ref[0, idx]], axis=0), q)
            w_t = None
            h_t = None
            for d in range(2):
                j = ML_HEADS * d + h
                m0 = m0_refs[d][0, idx][0:1, :]
                r = jnp.where(masks[d], z[:, j:j + 1], -jnp.inf)
                u = jnp.maximum(jnp.max(r, axis=0, keepdims=True), m0)
                sqk = s_t * jnp.exp(r - u)
                inter = jnp.exp(m0 - u)
                ncd = nc_t[d * LANES:(d + 1) * LANES, :]
                den = jnp.sum(sqk, axis=0, keepdims=True) + inter * ncd[ML_HEAD_DIM:ML_HEAD_DIM + 1, :]
                inv = 1.0 / jnp.maximum(jnp.abs(den), jnp.exp(-(bcol_t[j:j + 1, :] + u)))
                w_t = sqk * inv if w_t is None else w_t + sqk * inv
                h_t = (inter * inv) * ncd if h_t is None else h_t + (inter * inv) * ncd
            h_t = jnp.where(head_row, h_t + _dot(v_t, w_t.astype(BF16)), 0.0)
            ms = jnp.sum(h_t * h_t, axis=0, keepdims=True) * (1.0 / ML_HEAD_DIM)
            o_t = pa_ref[ML_O + h, rows, :].T
            out_ref[h, rows, :] = (_sigmoid(o_t) * (h_t * lax.rsqrt(ms + EPS))).T
        return carry

    lax.fori_loop(0, BATCH, per_batch, 0, unroll=4)


def _chunk_order(i, direction, n_ctx, n_all):
    if direction == 0:
        return i
    return jnp.where(i < n_ctx, n_ctx - 1 - i, n_all - 1 + n_ctx - i)


def _mlstm_states(pa, direction, n_ctx):
    nt = pa.shape[1]
    n_all = nt // CHUNK_ROWS
    nbh = BATCH * ML_HEADS
    order = lambda i: _chunk_order(i, direction, n_ctx, n_all)
    return pl.pallas_call(
        functools.partial(_mlstm_state_kernel, direction=direction),
        out_shape=(jax.ShapeDtypeStruct((n_all, nbh, ML_CHUNK, LANES), BF16),
                   jax.ShapeDtypeStruct((n_all, nbh, SUBLANES, LANES), F32)),
        grid=(n_all,),
        in_specs=[
            pl.BlockSpec((ML_STATE_SLABS, CHUNK_ROWS, LANES), lambda i: (0, order(i), 0)),
            pl.BlockSpec((2, CHUNK_ROWS, LANES), lambda i: (ML_IG // 2, order(i), 0)),
        ],
        out_specs=(pl.BlockSpec((1, nbh, ML_CHUNK, LANES), lambda i: (order(i), 0, 0, 0)),
                   pl.BlockSpec((1, nbh, SUBLANES, LANES), lambda i: (order(i), 0, 0, 0))),
        scratch_shapes=[
            pltpu.VMEM((nbh, ML_CHUNK, LANES), F32),
            pltpu.VMEM((nbh, SUBLANES, LANES), F32),
        ],
        compiler_params=_cparams("arbitrary"),
        name="mlstm_state_fwd" if direction == 0 else "mlstm_state_bwd",
    )(pa, pa)


def _mlstm_outputs(pa, c0f, m0f, c0b, m0b):
    nt = pa.shape[1]
    n_all = nt // CHUNK_ROWS
    nbh = BATCH * ML_HEADS
    cspec = pl.BlockSpec((1, nbh, ML_CHUNK, LANES), lambda i: (i, 0, 0, 0))
    mspec = pl.BlockSpec((1, nbh, SUBLANES, LANES), lambda i: (i, 0, 0, 0))
    return pl.pallas_call(
        _mlstm_out_kernel,
        out_shape=jax.ShapeDtypeStruct((ML_HEADS, nt, LANES), F32),
        grid=(n_all,),
        in_specs=[pl.BlockSpec((ML_SLABS, CHUNK_ROWS, LANES), lambda i: (0, i, 0)), cspec, mspec, cspec, mspec],
        out_specs=pl.BlockSpec((ML_HEADS, CHUNK_ROWS, LANES), lambda i: (0, i, 0)),
        compiler_params=_cparams("arbitrary"),
        name="mlstm_out",
    )(pa, c0f, m0f, c0b, m0b)


def _rglru_kernel(*refs, direction, final, n_ctx, n_all):
    if final:
        cur_ref, prev_ref, next_ref, hb_ref, cw_ref, cb_ref, wg_ref, bg_ref, lam_ref, out_ref, a_sc, b_sc, h_sc = refs
    else:
        cur_ref, prev_ref, next_ref, cw_ref, cb_ref, wg_ref, bg_ref, lam_ref, out_ref, a_sc, b_sc, h_sc = refs
        hb_ref = None
    i = pl.program_id(0)
    c = _chunk_order(i, direction, n_ctx, n_all)
    w = RG_WIDTH

    @pl.when(i == 0)
    def _():
        h_sc[...] = jnp.zeros(h_sc.shape, F32)

    seg_first = jnp.logical_or(c == 0, c == n_ctx)
    seg_last = jnp.logical_or(c == n_ctx - 1, c == n_all - 1)
    cur = cur_ref[:, :w]
    prev = jnp.where(seg_first, 0.0, prev_ref[:, :w])
    nxt = jnp.where(seg_last, 0.0, next_ref[:, :w])
    xc = jnp.concatenate([prev, cur, nxt], axis=0)
    n = CHUNK_ROWS
    xconv = cb_ref[...] + cw_ref[0:1, :] * xc[0:n]
    for j in range(1, RG_CONV):
        xconv = xconv + cw_ref[j:j + 1, :] * xc[j * BATCH:j * BATCH + n]
    z = _dot(xconv.astype(BF16), wg_ref[...]) + bg_ref[...]
    r = _sigmoid(z[:, :w])
    ig = _sigmoid(z[:, w:])
    log_a = (-RG_C * _softplus(-lam_ref[...])) * r
    a = jnp.exp(log_a)
    a_sc[...] = a
    b_sc[...] = jnp.sqrt(-jnp.tanh(log_a) * (a * a + 1.0)) * (ig * xconv)

    def step(t, h):
        tt = t if direction == 0 else T_CHUNK - 1 - t
        rows = pl.ds(pl.multiple_of(tt * BATCH, BATCH), BATCH)
        h = a_sc[rows, :] * h + b_sc[rows, :]
        b_sc[rows, :] = h
        return h

    h_sc[...] = lax.fori_loop(0, T_CHUNK, step, h_sc[...], unroll=8)
    hs = b_sc[...]
    if final:
        ht = hs + hb_ref[...]
        ss = _dot_xr(ht * ht, _group_ones(w, 6)) * (1.0 / RG_BLOCK_DIM)
        hs = _gelu(cur_ref[:, w:]) * (ht * lax.rsqrt(ss + EPS))
    out_ref[...] = hs


def _rglru_pass(pb, hb, cw, cb, wg, bg, lam, direction, n_ctx):
    nt = pb.shape[0]
    n_all = nt // CHUNK_ROWS
    final = hb is not None
    w2 = 2 * RG_WIDTH
    order = lambda i: _chunk_order(i, direction, n_ctx, n_all)
    halo_p = CHUNK_ROWS // (2 * BATCH)
    halo_n = CHUNK_ROWS // BATCH
    in_specs = [
        pl.BlockSpec((CHUNK_ROWS, w2), lambda i: (order(i), 0)),
        pl.BlockSpec((2 * BATCH, w2), lambda i: (jnp.maximum(order(i) * halo_p - 1, 0), 0)),
        pl.BlockSpec((BATCH, w2), lambda i: (jnp.minimum((order(i) + 1) * halo_n, nt // BATCH - 1), 0)),
    ]
    args = [pb, pb, pb]
    if final:
        in_specs.append(pl.BlockSpec((CHUNK_ROWS, RG_WIDTH), lambda i: (order(i), 0)))
        args.append(hb)
    full = lambda s: pl.BlockSpec(s, lambda i: tuple(0 for _ in s))
    in_specs += [full((RG_CONV, RG_WIDTH)), full((1, RG_WIDTH)), full((RG_WIDTH, w2)), full((1, w2)), full((1, RG_WIDTH))]
    args += [cw, cb, wg, bg, lam]
    return pl.pallas_call(
        functools.partial(_rglru_kernel, direction=direction, final=final, n_ctx=n_ctx, n_all=n_all),
        out_shape=jax.ShapeDtypeStruct((nt, RG_WIDTH), F32),
        grid=(n_all,),
        in_specs=in_specs,
        out_specs=pl.BlockSpec((CHUNK_ROWS, RG_WIDTH), lambda i: (order(i), 0)),
        scratch_shapes=[
            pltpu.VMEM((CHUNK_ROWS, RG_WIDTH), F32),
            pltpu.VMEM((CHUNK_ROWS, RG_WIDTH), F32),
            pltpu.VMEM((BATCH, RG_WIDTH), F32),
        ],
        compiler_params=_cparams("arbitrary"),
        name="rglru_fwd" if final else "rglru_bwd",
    )(*args)


def _s5_kernel(*refs, direction, final):
    if final:
        (u_ref, yb_ref, h0_ref, a_ref, bd_ref, cd_ref, dsk_ref, gw_ref, gb_ref,
         out_ref, hN_ref, v_sc, h_sc) = refs
    else:
        u_ref, h0_ref, a_ref, bd_ref, cd_ref, out_ref, hN_ref, v_sc, h_sc = refs
    i = pl.program_id(0)
    n = S5_NSTATE

    @pl.when(i == 0)
    def _():
        h_sc[...] = h0_ref[...]

    u = u_ref[...]
    v_sc[...] = _dot(u.astype(BF16), bd_ref[...])
    ar = jnp.broadcast_to(a_ref[0:1, :], (BATCH, n))
    ai = jnp.broadcast_to(a_ref[1:2, :], (BATCH, n))

    def step(t, carry):
        hr, hi = carry
        tt = t if direction == 0 else T_CHUNK - 1 - t
        rows = pl.ds(pl.multiple_of(tt * BATCH, BATCH), BATCH)
        nr = ar * hr - ai * hi + v_sc[rows, :n]
        ni = ar * hi + ai * hr + v_sc[rows, n:]
        v_sc[rows, :n] = nr
        v_sc[rows, n:] = ni
        return nr, ni

    hr, hi = lax.fori_loop(0, T_CHUNK, step, (h_sc[:, :n], h_sc[:, n:]))
    h_sc[:, :n] = hr
    h_sc[:, n:] = hi
    hN_ref[...] = h_sc[...]
    y = _dot(v_sc[...].astype(BF16), cd_ref[...])
    if final:
        y = y + yb_ref[...] + dsk_ref[...] * u
        g = _gelu(y)
        o = g * _sigmoid(_dot(g.astype(BF16), gw_ref[...]) + gb_ref[...])
        ss = _dot_xr(o * o, _group_ones(S5_WIDTH, 4)) * (1.0 / S5_GROUP)
        y = o * lax.rsqrt(ss + EPS)
    out_ref[...] = y


def _s5_pass(u, yb, h0, a, bd, cd, dsk, gw, gb, direction):
    nt = u.shape[0]
    n_chunks = nt // CHUNK_ROWS
    final = yb is not None
    order = (lambda i: i) if direction == 0 else (lambda i: n_chunks - 1 - i)
    full = lambda s: pl.BlockSpec(s, lambda i: tuple(0 for _ in s))
    chunk = pl.BlockSpec((CHUNK_ROWS, S5_WIDTH), lambda i: (order(i), 0))
    in_specs = [chunk]
    args = [u]
    if final:
        in_specs.append(chunk)
        args.append(yb)
    in_specs += [full((BATCH, 2 * S5_NSTATE)), full((2, S5_NSTATE)), full((S5_WIDTH, 2 * S5_NSTATE)),
                 full((2 * S5_NSTATE, S5_WIDTH))]
    args += [h0, a, bd, cd]
    if final:
        in_specs += [full((1, S5_WIDTH)), full((S5_WIDTH, S5_WIDTH)), full((1, S5_WIDTH))]
        args += [dsk, gw, gb]
    return pl.pallas_call(
        functools.partial(_s5_kernel, direction=direction, final=final),
        out_shape=(jax.ShapeDtypeStruct((nt, S5_WIDTH), F32), jax.ShapeDtypeStruct((BATCH, 2 * S5_NSTATE), F32)),
        grid=(n_chunks,),
        in_specs=in_specs,
        out_specs=(chunk, full((BATCH, 2 * S5_NSTATE))),
        scratch_shapes=[
            pltpu.VMEM((CHUNK_ROWS, 2 * S5_NSTATE), F32),
            pltpu.VMEM((BATCH, 2 * S5_NSTATE), F32),
        ],
        compiler_params=_cparams("arbitrary"),
        name="s5_fwd" if final else "s5_bwd",
    )(*args)


def _outproj_kernel(x_ref, ya_ref, yb_ref, ycc_ref, ycl_ref, mod_ref, mw_ref, wo_ref, n2_ref, rw_ref, rb_ref,
                    xo_ref, h2_ref, idx_ref, gate_ref, rank_ref, base_ref, cnt_ref, cnt_sc, *, tile0, n_ctx_tiles):
    i = pl.program_id(0)
    tile = i + tile0
    tm = x_ref.shape[0]

    @pl.when(i == 0)
    def _():
        cnt_sc[...] = jnp.zeros(cnt_sc.shape, F32)

    yc = jnp.where(tile < n_ctx_tiles, ycc_ref[...], ycl_ref[...].reshape(tm, S5_WIDTH))
    merged = jnp.concatenate([ya_ref[h] for h in range(ML_HEADS)] + [yb_ref[...], yc], axis=1)
    merged = (merged * mw_ref[...]).astype(BF16)
    proj = _dot(merged, wo_ref[...])
    g1 = mod_ref[0, 0]
    x = x_ref[...].reshape(tm // BATCH, BATCH, D_MODEL) + g1[None] * proj.reshape(tm // BATCH, BATCH, D_MODEL)
    xo_ref[...] = x.reshape(tm, D_MODEL)
    ms = jnp.mean(x * x, axis=-1, keepdims=True)
    xn = (x * lax.rsqrt(ms + EPS)) * n2_ref[...][None]
    h2 = (xn * (1.0 + mod_ref[0, 2][None]) + mod_ref[0, 1][None]).reshape(tm, D_MODEL)
    _store_token_tiles(h2_ref, 0, _pack_bf16_pairs(h2))
    logits = _dot3(rw_ref[...], h2, nt=True) + rb_ref[...]
    eidx = lax.broadcasted_iota(I32, (N_EXPERTS, tm), 0)
    vals, ids, hots = [], [], []
    for _ in range(TOP_K):
        m = jnp.max(logits, axis=0, keepdims=True)
        sel = jnp.min(jnp.where(logits == m, eidx, N_EXPERTS), axis=0, keepdims=True)
        hot = eidx == sel
        logits = jnp.where(hot, -jnp.inf, logits)
        vals.append(m)
        ids.append(sel)
        hots.append(hot)
    ex = [jnp.exp(v - vals[0]) for v in vals]
    tot = ex[0] + ex[1] + ex[2] + ex[3]
    idx_ref[...] = jnp.concatenate(ids, axis=0)
    gate_ref[...] = jnp.concatenate([e / tot for e in ex], axis=0)
    selm = hots[0].astype(F32) + hots[1].astype(F32) + hots[2].astype(F32) + hots[3].astype(F32)
    r = lax.broadcasted_iota(I32, (tm, tm), 0)
    c = lax.broadcasted_iota(I32, (tm, tm), 1)
    before = _dot(selm.astype(BF16), (r < c).astype(BF16)) + cnt_sc[:, 0:1]
    ranks = [jnp.sum(jnp.where(hot, before, 0.0), axis=0, keepdims=True) for hot in hots]
    rank_ref[...] = jnp.concatenate(ranks, axis=0).astype(I32)
    base_ref[0] = cnt_sc[...]
    cnt_sc[...] = cnt_sc[...] + jnp.sum(selm, axis=1, keepdims=True)
    cnt_ref[...] = cnt_sc[...]


def _outproj(x, ya, yb, ycc, ycl, mod, mw, wo, n2, rw, rb, tile0, n_ctx_rows):
    nt = x.shape[0]
    tm = ROW_TILE
    nct = n_ctx_rows // tm
    n_tiles = nt // tm - tile0
    n_out = n_tiles * tm
    rows = ycl.shape[0] // tm
    ycl4 = ycl.reshape(GRID_W, rows, BATCH, S5_WIDTH)
    full = lambda s: pl.BlockSpec(s, lambda i: tuple(0 for _ in s))
    return pl.pallas_call(
        functools.partial(_outproj_kernel, tile0=tile0, n_ctx_tiles=nct),
        out_shape=(
            jax.ShapeDtypeStruct((n_out, D_MODEL), F32),
            jax.ShapeDtypeStruct((n_out * PACKED_ROWS, LANES), I32),
            jax.ShapeDtypeStruct((TOP_K, n_out), I32),
            jax.ShapeDtypeStruct((TOP_K, n_out), F32),
            jax.ShapeDtypeStruct((TOP_K, n_out), I32),
            jax.ShapeDtypeStruct((n_tiles, N_EXPERTS, LANES), F32),
            jax.ShapeDtypeStruct((N_EXPERTS, LANES), F32),
        ),
        grid=(n_tiles,),
        in_specs=[
            pl.BlockSpec((tm, D_MODEL), lambda i: (i + tile0, 0)),
            pl.BlockSpec((ML_HEADS, tm, LANES), lambda i: (0, i + tile0, 0)),
            pl.BlockSpec((tm, RG_WIDTH), lambda i: (i + tile0, 0)),
            pl.BlockSpec((tm, S5_WIDTH), lambda i: (jnp.minimum(i + tile0, nct - 1), 0)),
            pl.BlockSpec((GRID_W, 1, BATCH, S5_WIDTH), lambda i: (0, jnp.maximum(i + tile0 - nct, 0), 0, 0)),
            pl.BlockSpec((1, 3, BATCH, D_MODEL), lambda i: (jnp.where(i + tile0 < nct, 0, 1), 0, 0, 0)),
            full((1, MERGED)),
            full((MERGED, D_MODEL)),
            full((1, D_MODEL)),
            full((N_EXPERTS, D_MODEL)),
            full((N_EXPERTS, 1)),
        ],
        out_specs=(
            pl.BlockSpec((tm, D_MODEL), lambda i: (i, 0)),
            pl.BlockSpec((tm * PACKED_ROWS, LANES), lambda i: (i, 0)),
            pl.BlockSpec((TOP_K, tm), lambda i: (0, i)),
            pl.BlockSpec((TOP_K, tm), lambda i: (0, i)),
            pl.BlockSpec((TOP_K, tm), lambda i: (0, i)),
            pl.BlockSpec((1, N_EXPERTS, LANES), lambda i: (i, 0, 0)),
            full((N_EXPERTS, LANES)),
        ),
        scratch_shapes=[pltpu.VMEM((N_EXPERTS, LANES), F32)],
        compiler_params=_cparams("arbitrary"),
        name="outproj_router",
    )(x, ya, yb, ycc, ycl4, mod, mw, wo, n2, rw, rb)


STAGE_TOKENS = TOP_K * ROW_TILE
PACKED_ROWS = D_MODEL // (2 * LANES)


def _start_run_copies(hbm, stage, runs_ref, sem, to_hbm, tok_rows):
    def per_expert(e, carry):
        row = runs_ref[0, 0, e]
        length = runs_ref[0, 0, N_EXPERTS + e]
        off = runs_ref[0, 0, 2 * N_EXPERTS + e]
        for bit in range(RUN_BITS):
            size = (1 << bit) * tok_rows

            @pl.when((lax.shift_right_logical(length, bit) & 1) == 1)
            def _():
                done = lax.shift_left(lax.shift_right_logical(length, bit + 1), bit + 1)
                h = hbm.at[pl.ds(pl.multiple_of((row + done) * tok_rows, tok_rows), size), :]
                s = stage.at[pl.ds(pl.multiple_of((off + done) * tok_rows, tok_rows), size), :]
                cp = pltpu.make_async_copy(s, h, sem) if to_hbm else pltpu.make_async_copy(h, s, sem)
                cp.start(priority=bit % DMA_PRIORITIES)
        return carry

    lax.fori_loop(0, N_EXPERTS, per_expert, 0)


def _wait_run_copies(hbm, stage, sem, to_hbm, tok_rows):
    h = hbm.at[pl.ds(0, STAGE_TOKENS * tok_rows), :]
    cp = pltpu.make_async_copy(stage, h, sem) if to_hbm else pltpu.make_async_copy(h, stage, sem)
    cp.wait()


def _dispatch_kernel(pe_ref, pd_ref, nu_ref, runs_ref, loc_ref, h_ref, xs_hbm, stage, zero_sc, sem, zsem, *,
                     n_blocks, n_tiles):
    i = pl.program_id(0)
    tm = ROW_TILE
    tr = PACKED_ROWS

    def zero_block(blk):
        start = pl.multiple_of(blk * (MOE_BM * tr), SUBLANES)
        return pltpu.make_async_copy(zero_sc, xs_hbm.at[pl.ds(start, MOE_BM * tr), :], zsem)

    @pl.when(i == 0)
    def _():
        zero_sc[...] = jnp.zeros(zero_sc.shape, I32)
        for e in range(N_EXPERTS):
            @pl.when(pd_ref[e] > 0)
            def _():
                zero_block(pe_ref[e] // MOE_BM - 1).start()
        for e in range(N_EXPERTS):
            @pl.when(pd_ref[e] > 0)
            def _():
                zero_block(0).wait()

        def fill(blk, carry):
            cp = zero_block(blk)
            cp.start()
            cp.wait()
            return carry

        lax.fori_loop(nu_ref[0], n_blocks, fill, 0)

    buf = lax.rem(i, 2)

    def place(r, carry):
        t = h_ref[pl.ds(pl.multiple_of(r * tr, tr), tr), :]
        for k in range(TOP_K):
            slot = loc_ref[0, 0, k * tm + r]
            stage[buf, pl.ds(pl.multiple_of(slot * tr, tr), tr), :] = t
        return carry

    lax.fori_loop(0, tm, place, 0, unroll=4)
    _start_run_copies(xs_hbm, stage.at[buf], runs_ref, sem.at[buf], to_hbm=True, tok_rows=tr)

    @pl.when(i > 0)
    def _():
        _wait_run_copies(xs_hbm, stage.at[1 - buf], sem.at[1 - buf], to_hbm=True, tok_rows=tr)

    @pl.when(i == n_tiles - 1)
    def _():
        _wait_run_copies(xs_hbm, stage.at[buf], sem.at[buf], to_hbm=True, tok_rows=tr)


def _moe_dispatch(pad_ends, padded, n_used, runs, loc, h2t, n_blocks):
    tm = ROW_TILE
    n_tiles = runs.shape[0]
    smem = lambda n: pl.BlockSpec((1, 1, n), lambda i, pe, pd, nu: (i, 0, 0), memory_space=pltpu.SMEM)
    grid_spec = pltpu.PrefetchScalarGridSpec(
        num_scalar_prefetch=3,
        grid=(n_tiles,),
        in_specs=[
            smem(3 * N_EXPERTS),
            smem(TOP_K * tm),
            pl.BlockSpec((tm * PACKED_ROWS, LANES), lambda i, pe, pd, nu: (i, 0)),
        ],
        out_specs=pl.BlockSpec(memory_space=pl.ANY),
        scratch_shapes=[
            pltpu.VMEM((2, STAGE_TOKENS * PACKED_ROWS, LANES), I32),
            pltpu.VMEM((MOE_BM * PACKED_ROWS, LANES), I32),
            pltpu.SemaphoreType.DMA((2,)),
            pltpu.SemaphoreType.DMA(()),
        ],
    )
    return pl.pallas_call(
        functools.partial(_dispatch_kernel, n_blocks=n_blocks, n_tiles=n_tiles),
        out_shape=jax.ShapeDtypeStruct((n_blocks * MOE_BM * PACKED_ROWS, LANES), I32),
        grid_spec=grid_spec,
        compiler_params=_cparams("arbitrary"),
        name="moe_dispatch",
    )(pad_ends, padded, n_used, runs, loc, h2t)


def _ffn_kernel(be_ref, nu_ref, x_ref, wgu_ref, bgu_ref, wd_ref, bd_ref, y_ref, wgu_sc, wd_sc):
    i = pl.program_id(0)
    changed = jnp.logical_or(i == 0, be_ref[i] != be_ref[jnp.maximum(i - 1, 0)])

    @pl.when(changed)
    def _():
        wgu_sc[...] = wgu_ref[0].astype(BF16)
        wd_sc[...] = wd_ref[0].astype(BF16)

    @pl.when(i < nu_ref[0])
    def _():
        x = _unpack_bf16_pairs(_load_token_tiles(x_ref, 0, MOE_BM, PACKED_ROWS))
        gu = _dot(x, wgu_sc[...]) + bgu_ref[0]
        gate = jnp.minimum(gu[:, :D_EXPERT], SWIGLU_LIMIT)
        up = jnp.clip(gu[:, D_EXPERT:], -SWIGLU_LIMIT, SWIGLU_LIMIT)
        glu = gate * _sigmoid(SWIGLU_ALPHA * gate)
        act = ((up + 1.0) * glu).astype(BF16)
        _store_token_tiles(y_ref, 0, _dot(act, wd_sc[...]) + bd_ref[0])

    @pl.when(i >= nu_ref[0])
    def _():
        y_ref[...] = jnp.zeros(y_ref.shape, F32)


def _moe_ffn(block_e, n_used, xs, wgu, bgu, wd, bd):
    rows = MOE_BM * SUBLANES
    in_rows = MOE_BM * PACKED_ROWS
    n_blocks = xs.shape[0] // in_rows
    ne = wgu.shape[0]
    grid_spec = pltpu.PrefetchScalarGridSpec(
        num_scalar_prefetch=2,
        grid=(n_blocks,),
        in_specs=[
            pl.BlockSpec((in_rows, LANES), lambda i, be, nu: (jnp.minimum(i, nu[0] - 1), 0)),
            pl.BlockSpec((1, D_MODEL, 2 * D_EXPERT), lambda i, be, nu: (be[i], 0, 0)),
            pl.BlockSpec((1, 1, 2 * D_EXPERT), lambda i, be, nu: (be[i], 0, 0)),
            pl.BlockSpec((1, D_EXPERT, D_MODEL), lambda i, be, nu: (be[i], 0, 0)),
            pl.BlockSpec((1, 1, D_MODEL), lambda i, be, nu: (be[i], 0, 0)),
        ],
        out_specs=pl.BlockSpec((rows, LANES), lambda i, be, nu: (i, 0)),
        scratch_shapes=[
            pltpu.VMEM((D_MODEL, 2 * D_EXPERT), BF16),
            pltpu.VMEM((D_EXPERT, D_MODEL), BF16),
        ],
    )
    return pl.pallas_call(
        _ffn_kernel,
        out_shape=jax.ShapeDtypeStruct((n_blocks * rows, LANES), F32),
        grid_spec=grid_spec,
        compiler_params=_cparams("arbitrary"),
        name="moe_ffn",
    )(block_e, n_used, xs, wgu, bgu.reshape(ne, 1, -1), wd, bd.reshape(ne, 1, -1))


def _combine_kernel(runs_ref, next_runs_ref, loc_ref, gate_ref, x_ref, mod_ref, fw_ref, ys_hbm, out_ref, stage, comb,
                    sem, *, final_norm, n_tiles):
    i = pl.program_id(0)
    tm = x_ref.shape[0]
    buf = lax.rem(i, 2)

    @pl.when(i == 0)
    def _():
        _start_run_copies(ys_hbm, stage.at[0], runs_ref, sem.at[0], to_hbm=False, tok_rows=SUBLANES)

    @pl.when(i + 1 < n_tiles)
    def _():
        _start_run_copies(ys_hbm, stage.at[1 - buf], next_runs_ref, sem.at[1 - buf], to_hbm=False,
                          tok_rows=SUBLANES)

    _wait_run_copies(ys_hbm, stage.at[buf], sem.at[buf], to_hbm=False, tok_rows=SUBLANES)

    def gather(r, carry):
        acc = None
        for k in range(TOP_K):
            slot = loc_ref[0, 0, k * tm + r]
            t = gate_ref[0, 0, k * tm + r] * stage[buf, pl.ds(pl.multiple_of(slot * SUBLANES, SUBLANES), SUBLANES), :]
            acc = t if acc is None else acc + t
        comb[pl.ds(pl.multiple_of(r * SUBLANES, SUBLANES), SUBLANES), :] = acc
        return carry

    lax.fori_loop(0, tm, gather, 0, unroll=4)
    f = _load_token_tiles(comb, 0, tm)
    x = x_ref[...].reshape(tm // BATCH, BATCH, D_MODEL) + mod_ref[0][None] * f.reshape(tm // BATCH, BATCH, D_MODEL)
    if final_norm:
        ms = jnp.mean(x * x, axis=-1, keepdims=True)
        x = (x * lax.rsqrt(ms + EPS)) * fw_ref[...][None]
    out_ref[...] = x.reshape(tm, D_MODEL)


def _moe_combine(runs, loc, gates_r, x, mod, fw, ys, n_ctx_rows, final_norm):
    nt = x.shape[0]
    tm = ROW_TILE
    n_tiles = nt // tm
    nct = n_ctx_rows // tm
    smem = lambda n: pl.BlockSpec((1, 1, n), lambda i: (i, 0, 0), memory_space=pltpu.SMEM)
    return pl.pallas_call(
        functools.partial(_combine_kernel, final_norm=final_norm, n_tiles=n_tiles),
        out_shape=jax.ShapeDtypeStruct((nt, D_MODEL), F32),
        grid=(n_tiles,),
        in_specs=[
            smem(3 * N_EXPERTS),
            pl.BlockSpec((1, 1, 3 * N_EXPERTS), lambda i: (jnp.minimum(i + 1, n_tiles - 1), 0, 0),
                         memory_space=pltpu.SMEM),
            smem(TOP_K * tm),
            smem(TOP_K * tm),
            pl.BlockSpec((tm, D_MODEL), lambda i: (i, 0)),
            pl.BlockSpec((1, BATCH, D_MODEL), lambda i: (jnp.where(i < nct, 0, 1), 0, 0)),
            pl.BlockSpec((1, D_MODEL), lambda i: (0, 0)),
            pl.BlockSpec(memory_space=pl.ANY),
        ],
        out_specs=pl.BlockSpec((tm, D_MODEL), lambda i: (i, 0)),
        scratch_shapes=[
            pltpu.VMEM((2, STAGE_TOKENS * SUBLANES, LANES), F32),
            pltpu.VMEM((tm * SUBLANES, LANES), F32),
            pltpu.SemaphoreType.DMA((2,)),
        ],
        compiler_params=_cparams("arbitrary"),
        name="moe_combine",
    )(runs, runs, loc, gates_r, x, mod, fw, ys)


def _moe(h2t, idx, gates, rank, tile_base, counts, x, g2, fw, layer, wgu, bgu, wd, bd, n_ctx_rows, final_norm):
    nt = idx.shape[1]
    bm = MOE_BM
    tm = ROW_TILE
    n_tiles = nt // tm
    n_blocks = (nt * TOP_K) // bm + N_EXPERTS
    cnt = counts[:, 0].astype(I32)
    padded = (cnt + bm - 1) // bm * bm
    pad_ends = jnp.cumsum(padded)
    pad_starts = pad_ends - padded
    experts = jnp.arange(N_EXPERTS, dtype=I32)
    base = tile_base[:, :, 0].astype(I32)
    run_len = jnp.concatenate([base[1:], cnt[None]], axis=0) - base
    run_off = jnp.cumsum(run_len, axis=1) - run_len
    run_row = pad_starts[None, :] + base
    runs = jnp.concatenate([run_row, run_len, run_off], axis=1).reshape(n_tiles, 1, 3 * N_EXPERTS)
    delta = (run_off - base).T
    idx3 = idx.reshape(TOP_K, n_tiles, tm)
    loc = rank.reshape(TOP_K, n_tiles, tm) + jnp.sum(
        jnp.where(idx3[None] == experts[:, None, None, None], delta[:, None, :, None], 0), axis=0)
    by_tile = lambda a: a.transpose(1, 0, 2).reshape(n_tiles, 1, TOP_K * tm)
    loc = by_tile(loc)
    block_start = jnp.arange(n_blocks, dtype=I32) * bm
    block_e = jnp.minimum(jnp.sum((pad_ends[None, :] <= block_start[:, None]).astype(I32), axis=1), N_EXPERTS - 1)
    n_used = (pad_ends[-1] // bm).astype(I32).reshape(1)
    xs = _moe_dispatch(pad_ends.astype(I32), padded, n_used, runs, loc, h2t, n_blocks)
    ys = _moe_ffn(block_e + layer * N_EXPERTS, n_used, xs, wgu, bgu, wd, bd)
    return _moe_combine(runs, loc, by_tile(gates.reshape(TOP_K, n_tiles, tm)), x, g2, fw, ys, n_ctx_rows, final_norm)


def _pad_heads(a):
    lead = a.shape[:-1]
    a = a.reshape(*lead, 4 * ML_HEADS, ML_HEAD_DIM)
    a = jnp.pad(a, [(0, 0)] * len(lead) + [(0, 0), (0, LANES - ML_HEAD_DIM)])
    return a.reshape(*lead, 4 * ML_HEADS * LANES)


def _kvqo(a):
    w = ML_HEADS * ML_HEAD_DIM
    return jnp.concatenate([a[..., w:2 * w], a[..., 2 * w:3 * w], a[..., :w], a[..., 3 * w:4 * w]], axis=-1)


def _block_diag(blocks):
    return jax.scipy.linalg.block_diag(*[blocks[g] for g in range(blocks.shape[0])])


def _s5_discretise(lam_re, lam_im, log_dt, b_re, b_im):
    dt = jnp.exp(log_dt)[:, None]
    mag = jnp.exp(lam_re * dt)
    ar, ai = mag * jnp.cos(lam_im * dt), mag * jnp.sin(lam_im * dt)
    den = lam_re * lam_re + lam_im * lam_im
    cr = ((ar - 1.0) * lam_re + ai * lam_im) / den
    ci = (ai * lam_re - (ar - 1.0) * lam_im) / den
    bbr = cr[..., None] * b_re - ci[..., None] * b_im
    bbi = cr[..., None] * b_im + ci[..., None] * b_re
    return ar, ai, bbr, bbi


def _layer_layouts(w_in, b_in, rg_wa, rg_ba, rg_wx, rg_bx, s5_lambda_re, s5_lambda_im, s5_log_dt, s5_b_re, s5_b_im,
                   s5_c_re, s5_c_im, mix_norm_w, w_out):
    dm = w_in.shape[0]
    g0 = 4 * ML_HEADS * ML_HEAD_DIM
    ng = 2 * ML_HEADS
    kscale = jnp.ones((4, ML_HEADS * LANES), F32).at[0].set(ML_HEAD_DIM ** -0.5).reshape(-1)
    ones_col = jnp.zeros((4, ML_HEADS, LANES), F32).at[1, :, ML_HEAD_DIM].set(1.0).reshape(-1)
    w_cat = jnp.concatenate([
        _pad_heads(_kvqo(w_in[:, :g0])) * kscale,
        jnp.pad(w_in[:, g0:g0 + ng], ((0, 0), (0, LANES - ng))),
        jnp.pad(w_in[:, g0 + ng:g0 + 2 * ng], ((0, 0), (0, LANES - ng))),
        w_in[:, g0 + 2 * ng:]], axis=1).astype(BF16)
    b_cat = jnp.concatenate([
        _pad_heads(_kvqo(b_in[:g0])) * kscale + ones_col,
        jnp.pad(b_in[g0:g0 + ng], (0, LANES - ng)),
        jnp.pad(b_in[g0 + ng:g0 + 2 * ng], (0, LANES - ng)),
        b_in[g0 + 2 * ng:]])[None]
    wg = [jnp.concatenate([_block_diag(rg_wa[d]), _block_diag(rg_wx[d])], axis=1).astype(BF16) for d in range(2)]
    bg = [jnp.concatenate([rg_ba[d], rg_bx[d]])[None] for d in range(2)]
    s5p = []
    for d in range(2):
        ar, ai, bbr, bbi = _s5_discretise(s5_lambda_re[d], s5_lambda_im[d], s5_log_dt[d], s5_b_re, s5_b_im)
        a = jnp.stack([ar.reshape(-1), ai.reshape(-1)])
        bd = jnp.concatenate([_block_diag(bbr.transpose(0, 2, 1)), _block_diag(bbi.transpose(0, 2, 1))],
                             axis=1).astype(BF16)
        cd = jnp.concatenate([_block_diag(s5_c_re.transpose(0, 2, 1)),
                              -_block_diag(s5_c_im.transpose(0, 2, 1))], axis=0).astype(BF16)
        s5p.append((a, bd, cd))
    wa = ML_HEADS * ML_HEAD_DIM
    mw_a = jnp.pad(mix_norm_w[:wa].reshape(ML_HEADS, ML_HEAD_DIM), ((0, 0), (0, LANES - ML_HEAD_DIM))).reshape(-1)
    mw = jnp.concatenate([mw_a, mix_norm_w[wa:]])[None]
    wo_a = jnp.pad(w_out[:wa].reshape(ML_HEADS, ML_HEAD_DIM, dm), ((0, 0), (0, LANES - ML_HEAD_DIM), (0, 0)))
    wo = jnp.concatenate([wo_a.reshape(ML_HEADS * LANES, dm), w_out[wa:]], axis=0).astype(BF16)
    return w_cat, b_cat, wg, bg, s5p, mw, wo


def kernel(x, c, ctx, c_ctx, ada_w, ada_b, norm1_w, w_in, b_in, rg_conv_w, rg_conv_b, rg_wa, rg_ba, rg_wx, rg_bx, rg_lambda, s5_lambda_re, s5_lambda_im, s5_log_dt, s5_b_re, s5_b_im, s5_c_re, s5_c_im, s5_d, s5_glu_w, s5_glu_b, mix_norm_w, w_out, norm2_w, router_w, router_b, moe_w_gate_up, moe_b_gate_up, moe_w_down, moe_b_down, final_norm_w):
    bsz, seq, dm = x.shape
    sc = ctx.shape[1]
    assert bsz == BATCH and dm == D_MODEL and seq % GRID_W == 0
    n_ctx_rows = sc * BATCH
    n_lat_rows = seq * BATCH
    assert n_ctx_rows % CHUNK_ROWS == 0 and n_lat_rows % CHUNK_ROWS == 0 and GRID_W * BATCH == ROW_TILE
    n_ctx_chunks = n_ctx_rows // CHUNK_ROWS
    depth = ada_w.shape[0]

    xs = jnp.concatenate([ctx.transpose(1, 0, 2).reshape(n_ctx_rows, dm),
                          x.transpose(1, 0, 2).reshape(n_lat_rows, dm)], axis=0)

    c_rows = jnp.concatenate([c, c_ctx[None], jnp.zeros((16 - bsz - 1, dm), F32)], axis=0)
    mods = _modulation(c_rows, ada_w, ada_b)
    mods = mods.reshape(depth, 16, 6, dm)
    mod_lat = mods[:, :bsz].transpose(0, 2, 1, 3)
    mod_ctx = jnp.broadcast_to(mods[:, bsz][:, :, None, :], mod_lat.shape)
    mod = jnp.stack([mod_ctx, mod_lat], axis=1)

    layouts = jax.vmap(_layer_layouts)(w_in, b_in, rg_wa, rg_ba, rg_wx, rg_bx, s5_lambda_re, s5_lambda_im, s5_log_dt,
                                       s5_b_re, s5_b_im, s5_c_re, s5_c_im, mix_norm_w, w_out)

    for l in range(depth):
        with_ctx = l < depth - 1
        w_cat, b_cat, wg, bg, s5p, mw, wo = jax.tree.map(lambda a: a[l], layouts)

        pa, pb, pcc, pcl = _inproj(xs, mod[l, :, 0:2], norm1_w[l][None], w_cat, b_cat, n_ctx_rows)
        pcl = pcl.reshape(n_lat_rows, S5_WIDTH)
        c0f, m0f = _mlstm_states(pa, 0, n_ctx_chunks)
        c0b, m0b = _mlstm_states(pa, 1, n_ctx_chunks)
        ya = _mlstm_outputs(pa, c0f, m0f, c0b, m0b)
        rgb = _rglru_pass(pb, None, rg_conv_w[l], rg_conv_b[l][None], wg[1], bg[1], rg_lambda[l, 1][None], 1, n_ctx_chunks)
        yb = _rglru_pass(pb, rgb, rg_conv_w[l], rg_conv_b[l][None], wg[0], bg[0], rg_lambda[l, 0][None], 0, n_ctx_chunks)
        zero_state = jnp.zeros((BATCH, 2 * S5_NSTATE), F32)
        a1, bd1, cd1 = s5p[1]
        a0, bd0, cd0 = s5p[0]
        dsk, gw, gb = s5_d[l][None], s5_glu_w[l].astype(BF16), s5_glu_b[l][None]
        ycb_c, st = _s5_pass(pcc, None, zero_state, a1, bd1, cd1, None, None, None, 1)
        ycb_l, _ = _s5_pass(pcl, None, st, a1, bd1, cd1, None, None, None, 1)
        yc_c, st = _s5_pass(pcc, ycb_c, zero_state, a0, bd0, cd0, dsk, gw, gb, 0)
        yc_l, _ = _s5_pass(pcl, ycb_l, st, a0, bd0, cd0, dsk, gw, gb, 0)

        tile0 = 0 if with_ctx else n_ctx_rows // ROW_TILE
        xo, h2, idx, gates, rank, tile_base, counts = _outproj(
            xs, ya, yb, yc_c, yc_l, mod[l, :, 2:5], mw, wo, norm2_w[l][None], router_w[l].T,
            router_b[l][:, None], tile0, n_ctx_rows)

        xs_new = _moe(h2, idx, gates, rank, tile_base, counts, xo, mod[l, :, 5], final_norm_w[None], l,
                      moe_w_gate_up.reshape(-1, dm, 2 * D_EXPERT), moe_b_gate_up.reshape(-1, 2 * D_EXPERT),
                      moe_w_down.reshape(-1, D_EXPERT, dm), moe_b_down.reshape(-1, dm),
                      n_ctx_rows if with_ctx else 0, l == depth - 1)
        xs = xs_new

    out = xs.reshape(seq, bsz, dm).transpose(1, 0, 2)
    return out
```

```python
import functools

import jax
import jax.numpy as jnp
from jax import lax
from jax.experimental import pallas as pl
from jax.experimental.pallas import tpu as pltpu

F32 = jnp.float32
BF16 = jnp.bfloat16
I32 = jnp.int32

D_MODEL = 1024
BATCH = 8
DEPTH = 4
GRID_W = 64
ML_HEADS = 4
ML_HEAD_DIM = 96
ML_CHUNK = 128
ML_M_INIT = -1e30
RG_BLOCKS = 6
RG_BLOCK_DIM = 64
RG_WIDTH = RG_BLOCKS * RG_BLOCK_DIM
RG_CONV = 4
RG_C = 8.0
S5_GROUPS = 16
S5_GROUP = 16
S5_WIDTH = S5_GROUPS * S5_GROUP
S5_STATE = 64
S5_NSTATE = S5_GROUPS * S5_STATE
N_EXPERTS = 32
TOP_K = 4
D_EXPERT = D_MODEL
SWIGLU_LIMIT = 7.0
SWIGLU_ALPHA = 1.702
EPS = 1e-6

LANES = 128
SUBLANES = 8

ML_K, ML_V, ML_Q, ML_O = 0, ML_HEADS, 2 * ML_HEADS, 3 * ML_HEADS
ML_IG = 4 * ML_HEADS
ML_FG = ML_IG + 1
ML_SLABS = ML_FG + 1
ML_STATE_SLABS = 2 * ML_HEADS
A_PAD = ML_SLABS * LANES
MERGED = ML_HEADS * LANES + RG_WIDTH + S5_WIDTH

ROW_TILE = 512
T_CHUNK = 128
CHUNK_ROWS = T_CHUNK * BATCH
MOE_BM = 512
DMA_PRIORITIES = 2
RUN_BITS = ROW_TILE.bit_length()
VMEM_LIMIT = 56 * 1024 * 1024


def _cparams(*sem):
    return pltpu.CompilerParams(dimension_semantics=sem, vmem_limit_bytes=VMEM_LIMIT)


def _sigmoid(x):
    return 0.5 * (jnp.tanh(0.5 * x) + 1.0)


def _log_sigmoid(x):
    return jnp.minimum(x, 0.0) - jnp.log1p(jnp.exp(-jnp.abs(x)))


def _softplus(x):
    return jnp.maximum(x, 0.0) + jnp.log1p(jnp.exp(-jnp.abs(x)))


def _gelu(x):
    return 0.5 * x * (1.0 + jnp.tanh(0.7978845608028654 * (x + 0.044715 * (x * x * x))))


def _dot(a, b):
    return jnp.dot(a, b, preferred_element_type=F32)


def _dot_nt(a, b):
    return lax.dot_general(a, b, (((1,), (1,)), ((), ())), preferred_element_type=F32)


def _split(a):
    hi = a.astype(BF16)
    lo = (a - hi.astype(F32)).astype(BF16)
    return hi, lo


def _dot_lx(a_exact, b):
    hi, lo = _split(b)
    return _dot(a_exact, hi) + _dot(a_exact, lo)


def _dot_xr(a, b_exact):
    hi, lo = _split(a)
    return _dot(hi, b_exact) + _dot(lo, b_exact)


def _dot3(a, b, nt=False):
    ah, al = _split(a)
    bh, bl = _split(b)
    d = _dot_nt if nt else _dot
    return d(ah, bh) + (d(ah, bl) + d(al, bh))


def _group_ones(n, shift):
    r = lax.broadcasted_iota(I32, (n, n), 0)
    c = lax.broadcasted_iota(I32, (n, n), 1)
    return (lax.shift_right_logical(r, shift) == lax.shift_right_logical(c, shift)).astype(BF16)


def _store_token_tiles(ref, tok0, val):
    n, width = val.shape
    rows = width // LANES
    for s in range(rows):
        ref[pl.ds(tok0 * rows + s, n, stride=rows), :] = val[:, s * LANES:(s + 1) * LANES]


def _load_token_tiles(ref, tok0, n, rows=SUBLANES):
    return jnp.concatenate([ref[pl.ds(tok0 * rows + s, n, stride=rows), :] for s in range(rows)], axis=1)


def _pack_bf16_pairs(x):
    w = x.shape[1] // 2
    bits = lax.bitcast_convert_type(x.astype(BF16).astype(F32), I32)
    return (bits[:, :w] & jnp.int32(-65536)) | lax.shift_right_logical(bits[:, w:], 16)


def _unpack_bf16_pairs(words):
    hi = lax.bitcast_convert_type(words & jnp.int32(-65536), F32)
    lo = lax.bitcast_convert_type(lax.shift_left(words, 16), F32)
    return jnp.concatenate([hi, lo], axis=1).astype(BF16)


def _mod_kernel(c_ref, w_ref, b_ref, o_ref):
    c = c_ref[...]
    s = c * _sigmoid(c)
    o_ref[0] = _dot3(s, w_ref[0]) + b_ref[0]


def _modulation(c_rows, ada_w, ada_b):
    depth, d, n = ada_w.shape
    tn = 1536
    return pl.pallas_call(
        _mod_kernel,
        out_shape=jax.ShapeDtypeStruct((depth, 16, n), F32),
        grid=(depth, n // tn),
        in_specs=[
            pl.BlockSpec((16, d), lambda l, j: (0, 0)),
            pl.BlockSpec((1, d, tn), lambda l, j: (l, 0, j)),
            pl.BlockSpec((1, 1, tn), lambda l, j: (l, 0, j)),
        ],
        out_specs=pl.BlockSpec((1, 16, tn), lambda l, j: (l, 0, j)),
        compiler_params=_cparams("arbitrary", "arbitrary"),
        name="adaln_mod",
    )(c_rows, ada_w, ada_b.reshape(depth, 1, n))


def _inproj_kernel(x_ref, mod_ref, nw_ref, w_ref, b_ref, pa_ref, pb_ref, pcc_ref, pcl_ref, *, n_ctx_tiles):
    i = pl.program_id(0)
    x = x_ref[...]
    tm = x.shape[0]
    ms = jnp.mean(x * x, axis=-1, keepdims=True)
    xn = (x * lax.rsqrt(ms + EPS)) * nw_ref[...]
    xn = xn.reshape(tm // BATCH, BATCH, D_MODEL)
    h = xn * (1.0 + mod_ref[0, 1][None]) + mod_ref[0, 0][None]
    h = h.reshape(tm, D_MODEL).astype(BF16)
    for j in range(0, ML_SLABS, 2):
        p = _dot(h, w_ref[:, j * LANES:(j + 2) * LANES]) + b_ref[:, j * LANES:(j + 2) * LANES]
        pa_ref[j] = p[:, :LANES]
        pa_ref[j + 1] = p[:, LANES:]
    c0 = A_PAD
    pb_ref[...] = _dot(h, w_ref[:, c0:c0 + 2 * RG_WIDTH]) + b_ref[:, c0:c0 + 2 * RG_WIDTH]
    c0 = A_PAD + 2 * RG_WIDTH
    pc = _dot(h, w_ref[:, c0:c0 + S5_WIDTH]) + b_ref[:, c0:c0 + S5_WIDTH]

    @pl.when(i < n_ctx_tiles)
    def _():
        pcc_ref[...] = pc

    @pl.when(i >= n_ctx_tiles)
    def _():
        pcl_ref[...] = pc.reshape(GRID_W, 1, BATCH, S5_WIDTH)


def _inproj(x, mod, nw, w, b, n_ctx_rows):
    nt = x.shape[0]
    tm = ROW_TILE
    nct = n_ctx_rows // tm
    n_lat_rows = nt - n_ctx_rows
    rows = n_lat_rows // tm
    ncols = w.shape[1]
    return pl.pallas_call(
        functools.partial(_inproj_kernel, n_ctx_tiles=nct),
        out_shape=(
            jax.ShapeDtypeStruct((ML_SLABS, nt, LANES), F32),
            jax.ShapeDtypeStruct((nt, 2 * RG_WIDTH), F32),
            jax.ShapeDtypeStruct((n_ctx_rows, S5_WIDTH), F32),
            jax.ShapeDtypeStruct((GRID_W, rows, BATCH, S5_WIDTH), F32),
        ),
        grid=(nt // tm,),
        in_specs=[
            pl.BlockSpec((tm, D_MODEL), lambda i: (i, 0)),
            pl.BlockSpec((1, 2, BATCH, D_MODEL), lambda i: (jnp.where(i < nct, 0, 1), 0, 0, 0)),
            pl.BlockSpec((1, D_MODEL), lambda i: (0, 0)),
            pl.BlockSpec((D_MODEL, ncols), lambda i: (0, 0)),
            pl.BlockSpec((1, ncols), lambda i: (0, 0)),
        ],
        out_specs=(
            pl.BlockSpec((ML_SLABS, tm, LANES), lambda i: (0, i, 0)),
            pl.BlockSpec((tm, 2 * RG_WIDTH), lambda i: (i, 0)),
            pl.BlockSpec((tm, S5_WIDTH), lambda i: (jnp.minimum(i, nct - 1), 0)),
            pl.BlockSpec((GRID_W, 1, BATCH, S5_WIDTH), lambda i: (0, jnp.maximum(i - nct, 0), 0, 0)),
        ),
        compiler_params=_cparams("arbitrary"),
        name="inproj",
    )(x, mod, nw, w, b)


def _mlstm_gates(ig, fg):
    L = ML_CHUNK
    row = lax.broadcasted_iota(I32, (L, L), 0)
    col = lax.broadcasted_iota(I32, (L, L), 1)
    lf = _log_sigmoid(fg)
    b_fwd = _dot_lx((row >= col).astype(BF16), lf)
    b_tot = _dot_lx(jnp.ones((L, L), BF16), lf)
    b_bwd = b_tot - b_fwd + lf
    bcol = jnp.where(col < ML_HEADS, b_fwd, b_bwd)
    return ig - bcol, bcol, b_tot


def _mlstm_state_kernel(kv_ref, g_ref, c0_ref, m0_ref, c_sc, m_sc, *, direction):
    i = pl.program_id(0)
    L = ML_CHUNK

    @pl.when(i == 0)
    def _():
        c_sc[...] = jnp.zeros(c_sc.shape, F32)
        m_sc[...] = jnp.full(m_sc.shape, ML_M_INIT, F32)

    def per_batch(b, carry):
        rows = pl.ds(b, L, stride=BATCH)
        z, _, b_tot = _mlstm_gates(g_ref[0, rows, :], g_ref[1, rows, :])
        z_t = z.T
        for h in range(ML_HEADS):
            j = ML_HEADS * direction + h
            idx = b * ML_HEADS + h
            k = kv_ref[ML_K + h, rows, :].astype(BF16)
            v_t = kv_ref[ML_V + h, rows, :].T
            c0 = c_sc[idx]
            m0_tile = m_sc[idx]
            c0_ref[0, idx] = c0.astype(BF16)
            m0_ref[0, idx] = m0_tile
            m0 = m0_tile[0:1, :]
            bt = jnp.broadcast_to(b_tot[0:1, j:j + 1], (1, LANES))
            w = bt + z_t[j:j + 1, :]
            mloc = jnp.broadcast_to(jnp.max(w, axis=1, keepdims=True), (1, LANES))
            cloc = _dot((v_t * jnp.exp(w - mloc)).astype(BF16), k)
            mnew = jnp.maximum(bt + m0, mloc)
            a = jnp.exp(bt + m0 - mnew)
            sc = jnp.exp(mloc - mnew)
            c_sc[idx] = a * c0 + sc * cloc
            m_sc[idx] = jnp.broadcast_to(mnew, (SUBLANES, LANES))
        return carry

    lax.fori_loop(0, BATCH, per_batch, 0, unroll=4)


def _mlstm_out_kernel(pa_ref, c0f_ref, m0f_ref, c0b_ref, m0b_ref, out_ref):
    L = ML_CHUNK
    row = lax.broadcasted_iota(I32, (L, L), 0)
    col = lax.broadcasted_iota(I32, (L, L), 1)
    masks = (row <= col, row >= col)
    head_row = row < ML_HEAD_DIM
    m0_refs = (m0f_ref, m0b_ref)

    def per_batch(b, carry):
        rows = pl.ds(b, L, stride=BATCH)
        z, bcol, _ = _mlstm_gates(pa_ref[ML_IG, rows, :], pa_ref[ML_FG, rows, :])
        bcol_t = bcol.T
        for h in range(ML_HEADS):
            idx = b * ML_HEADS + h
            q = pa_ref[ML_Q + h, rows, :].astype(BF16)
            k = pa_ref[ML_K + h, rows, :].astype(BF16)
            v_t = pa_ref[ML_V + h, rows, :].T.astype(BF16)
            s_t = _dot_nt(k, q)
            nc_t = _dot_nt(jnp.concatenate([c0f_ref[0, idx], c0b_ref[0, idx]], axis=0), q)
            w_t = None
            h_t = None
            for d in range(2):
                j = ML_HEADS * d + h
                m0 = m0_refs[d][0, idx][0:1, :]
                r = jnp.where(masks[d], z[:, j:j + 1], -jnp.inf)
                u = jnp.maximum(jnp.max(r, axis=0, keepdims=True), m0)
                sqk = s_t * jnp.exp(r - u)
                inter = jnp.exp(m0 - u)
                ncd = nc_t[d * LANES:(d + 1) * LANES, :]
                den = jnp.sum(sqk, axis=0, keepdims=True) + inter * ncd[ML_HEAD_DIM:ML_HEAD_DIM + 1, :]
                inv = 1.0 / jnp.maximum(jnp.abs(den), jnp.exp(-(bcol_t[j:j + 1, :] + u)))
                w_t = sqk * inv if w_t is None else w_t + sqk * inv
                h_t = (inter * inv) * ncd if h_t is None else h_t + (inter * inv) * ncd
            h_t = jnp.where(head_row, h_t + _dot(v_t, w_t.astype(BF16)), 0.0)
            ms = jnp.sum(h_t * h_t, axis=0, keepdims=True) * (1.0 / ML_HEAD_DIM)
            o_t = pa_ref[ML_O + h, rows, :].T
            out_ref[h, rows, :] = (_sigmoid(o_t) * (h_t * lax.rsqrt(ms + EPS))).T
        return carry

    lax.fori_loop(0, BATCH, per_batch, 0, unroll=4)


def _chunk_order(i, direction, n_ctx, n_all):
    if direction == 0:
        return i
    return jnp.where(i < n_ctx, n_ctx - 1 - i, n_all - 1 + n_ctx - i)


def _mlstm_states(pa, direction, n_ctx):
    nt = pa.shape[1]
    n_all = nt // CHUNK_ROWS
    nbh = BATCH * ML_HEADS
    order = lambda i: _chunk_order(i, direction, n_ctx, n_all)
    return pl.pallas_call(
        functools.partial(_mlstm_state_kernel, direction=direction),
        out_shape=(jax.ShapeDtypeStruct((n_all, nbh, ML_CHUNK, LANES), BF16),
                   jax.ShapeDtypeStruct((n_all, nbh, SUBLANES, LANES), F32)),
        grid=(n_all,),
        in_specs=[
            pl.BlockSpec((ML_STATE_SLABS, CHUNK_ROWS, LANES), lambda i: (0, order(i), 0)),
            pl.BlockSpec((2, CHUNK_ROWS, LANES), lambda i: (ML_IG // 2, order(i), 0)),
        ],
        out_specs=(pl.BlockSpec((1, nbh, ML_CHUNK, LANES), lambda i: (order(i), 0, 0, 0)),
                   pl.BlockSpec((1, nbh, SUBLANES, LANES), lambda i: (order(i), 0, 0, 0))),
        scratch_shapes=[
            pltpu.VMEM((nbh, ML_CHUNK, LANES), F32),
            pltpu.VMEM((nbh, SUBLANES, LANES), F32),
        ],
        compiler_params=_cparams("arbitrary"),
        name="mlstm_state_fwd" if direction == 0 else "mlstm_state_bwd",
    )(pa, pa)


def _mlstm_outputs(pa, c0f, m0f, c0b, m0b):
    nt = pa.shape[1]
    n_all = nt // CHUNK_ROWS
    nbh = BATCH * ML_HEADS
    cspec = pl.BlockSpec((1, nbh, ML_CHUNK, LANES), lambda i: (i, 0, 0, 0))
    mspec = pl.BlockSpec((1, nbh, SUBLANES, LANES), lambda i: (i, 0, 0, 0))
    return pl.pallas_call(
        _mlstm_out_kernel,
        out_shape=jax.ShapeDtypeStruct((ML_HEADS, nt, LANES), F32),
        grid=(n_all,),
        in_specs=[pl.BlockSpec((ML_SLABS, CHUNK_ROWS, LANES), lambda i: (0, i, 0)), cspec, mspec, cspec, mspec],
        out_specs=pl.BlockSpec((ML_HEADS, CHUNK_ROWS, LANES), lambda i: (0, i, 0)),
        compiler_params=_cparams("arbitrary"),
        name="mlstm_out",
    )(pa, c0f, m0f, c0b, m0b)


def _rglru_kernel(*refs, direction, final, n_ctx, n_all):
    if final:
        cur_ref, prev_ref, next_ref, hb_ref, cw_ref, cb_ref, wg_ref, bg_ref, lam_ref, out_ref, a_sc, b_sc, h_sc = refs
    else:
        cur_ref, prev_ref, next_ref, cw_ref, cb_ref, wg_ref, bg_ref, lam_ref, out_ref, a_sc, b_sc, h_sc = refs
        hb_ref = None
    i = pl.program_id(0)
    c = _chunk_order(i, direction, n_ctx, n_all)
    w = RG_WIDTH

    @pl.when(i == 0)
    def _():
        h_sc[...] = jnp.zeros(h_sc.shape, F32)

    seg_first = jnp.logical_or(c == 0, c == n_ctx)
    seg_last = jnp.logical_or(c == n_ctx - 1, c == n_all - 1)
    cur = cur_ref[:, :w]
    prev = jnp.where(seg_first, 0.0, prev_ref[:, :w])
    nxt = jnp.where(seg_last, 0.0, next_ref[:, :w])
    xc = jnp.concatenate([prev, cur, nxt], axis=0)
    n = CHUNK_ROWS
    xconv = cb_ref[...] + cw_ref[0:1, :] * xc[0:n]
    for j in range(1, RG_CONV):
        xconv = xconv + cw_ref[j:j + 1, :] * xc[j * BATCH:j * BATCH + n]
    z = _dot(xconv.astype(BF16), wg_ref[...]) + bg_ref[...]
    r = _sigmoid(z[:, :w])
    ig = _sigmoid(z[:, w:])
    log_a = (-RG_C * _softplus(-lam_ref[...])) * r
    a = jnp.exp(log_a)
    a_sc[...] = a
    b_sc[...] = jnp.sqrt(-jnp.tanh(log_a) * (a * a + 1.0)) * (ig * xconv)

    def step(t, h):
        tt = t if direction == 0 else T_CHUNK - 1 - t
        rows = pl.ds(pl.multiple_of(tt * BATCH, BATCH), BATCH)
        h = a_sc[rows, :] * h + b_sc[rows, :]
        b_sc[rows, :] = h
        return h

    h_sc[...] = lax.fori_loop(0, T_CHUNK, step, h_sc[...], unroll=8)
    hs = b_sc[...]
    if final:
        ht = hs + hb_ref[...]
        ss = _dot_xr(ht * ht, _group_ones(w, 6)) * (1.0 / RG_BLOCK_DIM)
        hs = _gelu(cur_ref[:, w:]) * (ht * lax.rsqrt(ss + EPS))
    out_ref[...] = hs


def _rglru_pass(pb, hb, cw, cb, wg, bg, lam, direction, n_ctx):
    nt = pb.shape[0]
    n_all = nt // CHUNK_ROWS
    final = hb is not None
    w2 = 2 * RG_WIDTH
    order = lambda i: _chunk_order(i, direction, n_ctx, n_all)
    halo_p = CHUNK_ROWS // (2 * BATCH)
    halo_n = CHUNK_ROWS // BATCH
    in_specs = [
        pl.BlockSpec((CHUNK_ROWS, w2), lambda i: (order(i), 0)),
        pl.BlockSpec((2 * BATCH, w2), lambda i: (jnp.maximum(order(i) * halo_p - 1, 0), 0)),
        pl.BlockSpec((BATCH, w2), lambda i: (jnp.minimum((order(i) + 1) * halo_n, nt // BATCH - 1), 0)),
    ]
    args = [pb, pb, pb]
    if final:
        in_specs.append(pl.BlockSpec((CHUNK_ROWS, RG_WIDTH), lambda i: (order(i), 0)))
        args.append(hb)
    full = lambda s: pl.BlockSpec(s, lambda i: tuple(0 for _ in s))
    in_specs += [full((RG_CONV, RG_WIDTH)), full((1, RG_WIDTH)), full((RG_WIDTH, w2)), full((1, w2)), full((1, RG_WIDTH))]
    args += [cw, cb, wg, bg, lam]
    return pl.pallas_call(
        functools.partial(_rglru_kernel, direction=direction, final=final, n_ctx=n_ctx, n_all=n_all),
        out_shape=jax.ShapeDtypeStruct((nt, RG_WIDTH), F32),
        grid=(n_all,),
        in_specs=in_specs,
        out_specs=pl.BlockSpec((CHUNK_ROWS, RG_WIDTH), lambda i: (order(i), 0)),
        scratch_shapes=[
            pltpu.VMEM((CHUNK_ROWS, RG_WIDTH), F32),
            pltpu.VMEM((CHUNK_ROWS, RG_WIDTH), F32),
            pltpu.VMEM((BATCH, RG_WIDTH), F32),
        ],
        compiler_params=_cparams("arbitrary"),
        name="rglru_fwd" if final else "rglru_bwd",
    )(*args)


def _s5_kernel(*refs, direction, final):
    if final:
        (u_ref, yb_ref, h0_ref, a_ref, bd_ref, cd_ref, dsk_ref, gw_ref, gb_ref,
         out_ref, hN_ref, v_sc, h_sc) = refs
    else:
        u_ref, h0_ref, a_ref, bd_ref, cd_ref, out_ref, hN_ref, v_sc, h_sc = refs
    i = pl.program_id(0)
    n = S5_NSTATE

    @pl.when(i == 0)
    def _():
        h_sc[...] = h0_ref[...]

    u = u_ref[...]
    v_sc[...] = _dot(u.astype(BF16), bd_ref[...])
    ar = jnp.broadcast_to(a_ref[0:1, :], (BATCH, n))
    ai = jnp.broadcast_to(a_ref[1:2, :], (BATCH, n))

    def step(t, carry):
        hr, hi = carry
        tt = t if direction == 0 else T_CHUNK - 1 - t
        rows = pl.ds(pl.multiple_of(tt * BATCH, BATCH), BATCH)
        nr = ar * hr - ai * hi + v_sc[rows, :n]
        ni = ar * hi + ai * hr + v_sc[rows, n:]
        v_sc[rows, :n] = nr
        v_sc[rows, n:] = ni
        return nr, ni

    hr, hi = lax.fori_loop(0, T_CHUNK, step, (h_sc[:, :n], h_sc[:, n:]))
    h_sc[:, :n] = hr
    h_sc[:, n:] = hi
    hN_ref[...] = h_sc[...]
    y = _dot(v_sc[...].astype(BF16), cd_ref[...])
    if final:
        y = y + yb_ref[...] + dsk_ref[...] * u
        g = _gelu(y)
        o = g * _sigmoid(_dot(g.astype(BF16), gw_ref[...]) + gb_ref[...])
        ss = _dot_xr(o * o, _group_ones(S5_WIDTH, 4)) * (1.0 / S5_GROUP)
        y = o * lax.rsqrt(ss + EPS)
    out_ref[...] = y


def _s5_pass(u, yb, h0, a, bd, cd, dsk, gw, gb, direction):
    nt = u.shape[0]
    n_chunks = nt // CHUNK_ROWS
    final = yb is not None
    order = (lambda i: i) if direction == 0 else (lambda i: n_chunks - 1 - i)
    full = lambda s: pl.BlockSpec(s, lambda i: tuple(0 for _ in s))
    chunk = pl.BlockSpec((CHUNK_ROWS, S5_WIDTH), lambda i: (order(i), 0))
    in_specs = [chunk]
    args = [u]
    if final:
        in_specs.append(chunk)
        args.append(yb)
    in_specs += [full((BATCH, 2 * S5_NSTATE)), full((2, S5_NSTATE)), full((S5_WIDTH, 2 * S5_NSTATE)),
                 full((2 * S5_NSTATE, S5_WIDTH))]
    args += [h0, a, bd, cd]
    if final:
        in_specs += [full((1, S5_WIDTH)), full((S5_WIDTH, S5_WIDTH)), full((1, S5_WIDTH))]
        args += [dsk, gw, gb]
    return pl.pallas_call(
        functools.partial(_s5_kernel, direction=direction, final=final),
        out_shape=(jax.ShapeDtypeStruct((nt, S5_WIDTH), F32), jax.ShapeDtypeStruct((BATCH, 2 * S5_NSTATE), F32)),
        grid=(n_chunks,),
        in_specs=in_specs,
        out_specs=(chunk, full((BATCH, 2 * S5_NSTATE))),
        scratch_shapes=[
            pltpu.VMEM((CHUNK_ROWS, 2 * S5_NSTATE), F32),
            pltpu.VMEM((BATCH, 2 * S5_NSTATE), F32),
        ],
        compiler_params=_cparams("arbitrary"),
        name="s5_fwd" if final else "s5_bwd",
    )(*args)


def _outproj_kernel(x_ref, ya_ref, yb_ref, ycc_ref, ycl_ref, mod_ref, mw_ref, wo_ref, n2_ref, rw_ref, rb_ref,
                    xo_ref, h2_ref, idx_ref, gate_ref, rank_ref, base_ref, cnt_ref, cnt_sc, *, tile0, n_ctx_tiles):
    i = pl.program_id(0)
    tile = i + tile0
    tm = x_ref.shape[0]

    @pl.when(i == 0)
    def _():
        cnt_sc[...] = jnp.zeros(cnt_sc.shape, F32)

    yc = jnp.where(tile < n_ctx_tiles, ycc_ref[...], ycl_ref[...].reshape(tm, S5_WIDTH))
    merged = jnp.concatenate([ya_ref[h] for h in range(ML_HEADS)] + [yb_ref[...], yc], axis=1)
    merged = (merged * mw_ref[...]).astype(BF16)
    proj = _dot(merged, wo_ref[...])
    g1 = mod_ref[0, 0]
    x = x_ref[...].reshape(tm // BATCH, BATCH, D_MODEL) + g1[None] * proj.reshape(tm // BATCH, BATCH, D_MODEL)
    xo_ref[...] = x.reshape(tm, D_MODEL)
    ms = jnp.mean(x * x, axis=-1, keepdims=True)
    xn = (x * lax.rsqrt(ms + EPS)) * n2_ref[...][None]
    h2 = (xn * (1.0 + mod_ref[0, 2][None]) + mod_ref[0, 1][None]).reshape(tm, D_MODEL)
    _store_token_tiles(h2_ref, 0, _pack_bf16_pairs(h2))
    logits = _dot3(rw_ref[...], h2, nt=True) + rb_ref[...]
    eidx = lax.broadcasted_iota(I32, (N_EXPERTS, tm), 0)
    vals, ids, hots = [], [], []
    for _ in range(TOP_K):
        m = jnp.max(logits, axis=0, keepdims=True)
        sel = jnp.min(jnp.where(logits == m, eidx, N_EXPERTS), axis=0, keepdims=True)
        hot = eidx == sel
        logits = jnp.where(hot, -jnp.inf, logits)
        vals.append(m)
        ids.append(sel)
        hots.append(hot)
    ex = [jnp.exp(v - vals[0]) for v in vals]
    tot = ex[0] + ex[1] + ex[2] + ex[3]
    idx_ref[...] = jnp.concatenate(ids, axis=0)
    gate_ref[...] = jnp.concatenate([e / tot for e in ex], axis=0)
    selm = hots[0].astype(F32) + hots[1].astype(F32) + hots[2].astype(F32) + hots[3].astype(F32)
    r = lax.broadcasted_iota(I32, (tm, tm), 0)
    c = lax.broadcasted_iota(I32, (tm, tm), 1)
    before = _dot(selm.astype(BF16), (r < c).astype(BF16)) + cnt_sc[:, 0:1]
    ranks = [jnp.sum(jnp.where(hot, before, 0.0), axis=0, keepdims=True) for hot in hots]
    rank_ref[...] = jnp.concatenate(ranks, axis=0).astype(I32)
    base_ref[0] = cnt_sc[...]
    cnt_sc[...] = cnt_sc[...] + jnp.sum(selm, axis=1, keepdims=True)
    cnt_ref[...] = cnt_sc[...]


def _outproj(x, ya, yb, ycc, ycl, mod, mw, wo, n2, rw, rb, tile0, n_ctx_rows):
    nt = x.shape[0]
    tm = ROW_TILE
    nct = n_ctx_rows // tm
    n_tiles = nt // tm - tile0
    n_out = n_tiles * tm
    rows = ycl.shape[0] // tm
    ycl4 = ycl.reshape(GRID_W, rows, BATCH, S5_WIDTH)
    full = lambda s: pl.BlockSpec(s, lambda i: tuple(0 for _ in s))
    return pl.pallas_call(
        functools.partial(_outproj_kernel, tile0=tile0, n_ctx_tiles=nct),
        out_shape=(
            jax.ShapeDtypeStruct((n_out, D_MODEL), F32),
            jax.ShapeDtypeStruct((n_out * PACKED_ROWS, LANES), I32),
            jax.ShapeDtypeStruct((TOP_K, n_out), I32),
            jax.ShapeDtypeStruct((TOP_K, n_out), F32),
            jax.ShapeDtypeStruct((TOP_K, n_out), I32),
            jax.ShapeDtypeStruct((n_tiles, N_EXPERTS, LANES), F32),
            jax.ShapeDtypeStruct((N_EXPERTS, LANES), F32),
        ),
        grid=(n_tiles,),
        in_specs=[
            pl.BlockSpec((tm, D_MODEL), lambda i: (i + tile0, 0)),
            pl.BlockSpec((ML_HEADS, tm, LANES), lambda i: (0, i + tile0, 0)),
            pl.BlockSpec((tm, RG_WIDTH), lambda i: (i + tile0, 0)),
            pl.BlockSpec((tm, S5_WIDTH), lambda i: (jnp.minimum(i + tile0, nct - 1), 0)),
            pl.BlockSpec((GRID_W, 1, BATCH, S5_WIDTH), lambda i: (0, jnp.maximum(i + tile0 - nct, 0), 0, 0)),
            pl.BlockSpec((1, 3, BATCH, D_MODEL), lambda i: (jnp.where(i + tile0 < nct, 0, 1), 0, 0, 0)),
            full((1, MERGED)),
            full((MERGED, D_MODEL)),
            full((1, D_MODEL)),
            full((N_EXPERTS, D_MODEL)),
            full((N_EXPERTS, 1)),
        ],
        out_specs=(
            pl.BlockSpec((tm, D_MODEL), lambda i: (i, 0)),
            pl.BlockSpec((tm * PACKED_ROWS, LANES), lambda i: (i, 0)),
            pl.BlockSpec((TOP_K, tm), lambda i: (0, i)),
            pl.BlockSpec((TOP_K, tm), lambda i: (0, i)),
            pl.BlockSpec((TOP_K, tm), lambda i: (0, i)),
            pl.BlockSpec((1, N_EXPERTS, LANES), lambda i: (i, 0, 0)),
            full((N_EXPERTS, LANES)),
        ),
        scratch_shapes=[pltpu.VMEM((N_EXPERTS, LANES), F32)],
        compiler_params=_cparams("arbitrary"),
        name="outproj_router",
    )(x, ya, yb, ycc, ycl4, mod, mw, wo, n2, rw, rb)


STAGE_TOKENS = TOP_K * ROW_TILE
COMBINE_AHEAD = 2
PACKED_ROWS = D_MODEL // (2 * LANES)


def _start_run_copies(hbm, stage, runs_ref, sem, to_hbm, tok_rows):
    def per_expert(e, carry):
        row = runs_ref[0, 0, e]
        length = runs_ref[0, 0, N_EXPERTS + e]
        off = runs_ref[0, 0, 2 * N_EXPERTS + e]
        for bit in range(RUN_BITS):
            size = (1 << bit) * tok_rows

            @pl.when((lax.shift_right_logical(length, bit) & 1) == 1)
            def _():
                done = lax.shift_left(lax.shift_right_logical(length, bit + 1), bit + 1)
                h = hbm.at[pl.ds(pl.multiple_of((row + done) * tok_rows, tok_rows), size), :]
                s = stage.at[pl.ds(pl.multiple_of((off + done) * tok_rows, tok_rows), size), :]
                cp = pltpu.make_async_copy(s, h, sem) if to_hbm else pltpu.make_async_copy(h, s, sem)
                cp.start(priority=bit % DMA_PRIORITIES)
        return carry

    lax.fori_loop(0, N_EXPERTS, per_expert, 0)


def _wait_run_copies(hbm, stage, sem, to_hbm, tok_rows):
    h = hbm.at[pl.ds(0, STAGE_TOKENS * tok_rows), :]
    cp = pltpu.make_async_copy(stage, h, sem) if to_hbm else pltpu.make_async_copy(h, stage, sem)
    cp.wait()


def _dispatch_kernel(pe_ref, pd_ref, nu_ref, runs_ref, loc_ref, h_ref, xs_hbm, stage, zero_sc, sem, zsem, *,
                     n_blocks, n_tiles):
    i = pl.program_id(0)
    tm = ROW_TILE
    tr = PACKED_ROWS

    def zero_block(blk):
        start = pl.multiple_of(blk * (MOE_BM * tr), SUBLANES)
        return pltpu.make_async_copy(zero_sc, xs_hbm.at[pl.ds(start, MOE_BM * tr), :], zsem)

    @pl.when(i == 0)
    def _():
        zero_sc[...] = jnp.zeros(zero_sc.shape, I32)
        for e in range(N_EXPERTS):
            @pl.when(pd_ref[e] > 0)
            def _():
                zero_block(pe_ref[e] // MOE_BM - 1).start()
        for e in range(N_EXPERTS):
            @pl.when(pd_ref[e] > 0)
            def _():
                zero_block(0).wait()

        def fill(blk, carry):
            cp = zero_block(blk)
            cp.start()
            cp.wait()
            return carry

        lax.fori_loop(nu_ref[0], n_blocks, fill, 0)

    buf = lax.rem(i, 2)

    def place(r, carry):
        t = h_ref[pl.ds(pl.multiple_of(r * tr, tr), tr), :]
        for k in range(TOP_K):
            stage[buf, pl.ds(pl.multiple_of(loc_ref[0, 0, k * tm + r], tr), tr), :] = t
        return carry

    lax.fori_loop(0, tm, place, 0, unroll=4)
    _start_run_copies(xs_hbm, stage.at[buf], runs_ref, sem.at[buf], to_hbm=True, tok_rows=tr)

    @pl.when(i > 0)
    def _():
        _wait_run_copies(xs_hbm, stage.at[1 - buf], sem.at[1 - buf], to_hbm=True, tok_rows=tr)

    @pl.when(i == n_tiles - 1)
    def _():
        _wait_run_copies(xs_hbm, stage.at[buf], sem.at[buf], to_hbm=True, tok_rows=tr)


def _moe_dispatch(pad_ends, padded, n_used, runs, loc, h2t, n_blocks):
    tm = ROW_TILE
    n_tiles = runs.shape[0]
    smem = lambda n: pl.BlockSpec((1, 1, n), lambda i, pe, pd, nu: (i, 0, 0), memory_space=pltpu.SMEM)
    grid_spec = pltpu.PrefetchScalarGridSpec(
        num_scalar_prefetch=3,
        grid=(n_tiles,),
        in_specs=[
            smem(3 * N_EXPERTS),
            smem(TOP_K * tm),
            pl.BlockSpec((tm * PACKED_ROWS, LANES), lambda i, pe, pd, nu: (i, 0)),
        ],
        out_specs=pl.BlockSpec(memory_space=pl.ANY),
        scratch_shapes=[
            pltpu.VMEM((2, STAGE_TOKENS * PACKED_ROWS, LANES), I32),
            pltpu.VMEM((MOE_BM * PACKED_ROWS, LANES), I32),
            pltpu.SemaphoreType.DMA((2,)),
            pltpu.SemaphoreType.DMA(()),
        ],
    )
    return pl.pallas_call(
        functools.partial(_dispatch_kernel, n_blocks=n_blocks, n_tiles=n_tiles),
        out_shape=jax.ShapeDtypeStruct((n_blocks * MOE_BM * PACKED_ROWS, LANES), I32),
        grid_spec=grid_spec,
        compiler_params=_cparams("arbitrary"),
        name="moe_dispatch",
    )(pad_ends, padded, n_used, runs, loc, h2t)


def _ffn_kernel(be_ref, nu_ref, x_ref, wgu_ref, bgu_ref, wd_ref, bd_ref, y_ref, wgu_sc, wd_sc):
    i = pl.program_id(0)
    changed = jnp.logical_or(i == 0, be_ref[i] != be_ref[jnp.maximum(i - 1, 0)])

    @pl.when(changed)
    def _():
        wgu_sc[...] = wgu_ref[0].astype(BF16)
        wd_sc[...] = wd_ref[0].astype(BF16)

    @pl.when(i < nu_ref[0])
    def _():
        x = _unpack_bf16_pairs(_load_token_tiles(x_ref, 0, MOE_BM, PACKED_ROWS))
        gu = _dot(x, wgu_sc[...]) + bgu_ref[0]
        gate = jnp.minimum(gu[:, :D_EXPERT], SWIGLU_LIMIT)
        up = jnp.clip(gu[:, D_EXPERT:], -SWIGLU_LIMIT, SWIGLU_LIMIT)
        glu = gate * _sigmoid(SWIGLU_ALPHA * gate)
        act = ((up + 1.0) * glu).astype(BF16)
        _store_token_tiles(y_ref, 0, _dot(act, wd_sc[...]) + bd_ref[0])

    @pl.when(i >= nu_ref[0])
    def _():
        y_ref[...] = jnp.zeros(y_ref.shape, F32)


def _moe_ffn(block_e, n_used, xs, wgu, bgu, wd, bd):
    rows = MOE_BM * SUBLANES
    in_rows = MOE_BM * PACKED_ROWS
    n_blocks = xs.shape[0] // in_rows
    ne = wgu.shape[0]
    grid_spec = pltpu.PrefetchScalarGridSpec(
        num_scalar_prefetch=2,
        grid=(n_blocks,),
        in_specs=[
            pl.BlockSpec((in_rows, LANES), lambda i, be, nu: (jnp.minimum(i, nu[0] - 1), 0)),
            pl.BlockSpec((1, D_MODEL, 2 * D_EXPERT), lambda i, be, nu: (be[i], 0, 0)),
            pl.BlockSpec((1, 1, 2 * D_EXPERT), lambda i, be, nu: (be[i], 0, 0)),
            pl.BlockSpec((1, D_EXPERT, D_MODEL), lambda i, be, nu: (be[i], 0, 0)),
            pl.BlockSpec((1, 1, D_MODEL), lambda i, be, nu: (be[i], 0, 0)),
        ],
        out_specs=pl.BlockSpec((rows, LANES), lambda i, be, nu: (i, 0)),
        scratch_shapes=[
            pltpu.VMEM((D_MODEL, 2 * D_EXPERT), BF16),
            pltpu.VMEM((D_EXPERT, D_MODEL), BF16),
        ],
    )
    return pl.pallas_call(
        _ffn_kernel,
        out_shape=jax.ShapeDtypeStruct((n_blocks * rows, LANES), F32),
        grid_spec=grid_spec,
        compiler_params=_cparams("arbitrary"),
        name="moe_ffn",
    )(block_e, n_used, xs, wgu, bgu.reshape(ne, 1, -1), wd, bd.reshape(ne, 1, -1))


def _combine_kernel(*refs, final_norm, n_tiles):
    runs_refs = refs[:COMBINE_AHEAD + 1]
    loc_ref, gate_ref, x_ref, mod_ref, fw_ref, ys_hbm, out_ref, stage, comb, sem = refs[COMBINE_AHEAD + 1:]
    i = pl.program_id(0)
    tm = x_ref.shape[0]
    n_buf = COMBINE_AHEAD + 1
    buf = lax.rem(i, n_buf)

    def fetch(slot, runs_ref):
        _start_run_copies(ys_hbm, stage.at[slot], runs_ref, sem.at[slot], to_hbm=False, tok_rows=SUBLANES)

    @pl.when(i == 0)
    def _():
        for a in range(min(COMBINE_AHEAD, n_tiles)):
            fetch(a, runs_refs[a])

    @pl.when(i + COMBINE_AHEAD < n_tiles)
    def _():
        fetch(lax.rem(i + COMBINE_AHEAD, n_buf), runs_refs[COMBINE_AHEAD])

    _wait_run_copies(ys_hbm, stage.at[buf], sem.at[buf], to_hbm=False, tok_rows=SUBLANES)

    def gather(r, carry):
        acc = None
        for k in range(TOP_K):
            row = pl.multiple_of(loc_ref[0, 0, k * tm + r], SUBLANES)
            t = gate_ref[0, 0, k * tm + r] * stage[buf, pl.ds(row, SUBLANES), :]
            acc = t if acc is None else acc + t
        comb[pl.ds(pl.multiple_of(r * SUBLANES, SUBLANES), SUBLANES), :] = acc
        return carry

    lax.fori_loop(0, tm, gather, 0, unroll=4)
    f = _load_token_tiles(comb, 0, tm)
    x = x_ref[...].reshape(tm // BATCH, BATCH, D_MODEL) + mod_ref[0][None] * f.reshape(tm // BATCH, BATCH, D_MODEL)
    if final_norm:
        ms = jnp.mean(x * x, axis=-1, keepdims=True)
        x = (x * lax.rsqrt(ms + EPS)) * fw_ref[...][None]
    out_ref[...] = x.reshape(tm, D_MODEL)


def _moe_combine(runs, loc, gates_r, x, mod, fw, ys, n_ctx_rows, final_norm):
    nt = x.shape[0]
    tm = ROW_TILE
    n_tiles = nt // tm
    nct = n_ctx_rows // tm
    smem = lambda n: pl.BlockSpec((1, 1, n), lambda i: (i, 0, 0), memory_space=pltpu.SMEM)
    ahead = lambda a, i: (jnp.minimum(i + a, n_tiles - 1), 0, 0)
    return pl.pallas_call(
        functools.partial(_combine_kernel, final_norm=final_norm, n_tiles=n_tiles),
        out_shape=jax.ShapeDtypeStruct((nt, D_MODEL), F32),
        grid=(n_tiles,),
        in_specs=[
            pl.BlockSpec((1, 1, 3 * N_EXPERTS), functools.partial(ahead, a), memory_space=pltpu.SMEM)
            for a in range(COMBINE_AHEAD + 1)
        ] + [
            smem(TOP_K * tm),
            smem(TOP_K * tm),
            pl.BlockSpec((tm, D_MODEL), lambda i: (i, 0)),
            pl.BlockSpec((1, BATCH, D_MODEL), lambda i: (jnp.where(i < nct, 0, 1), 0, 0)),
            pl.BlockSpec((1, D_MODEL), lambda i: (0, 0)),
            pl.BlockSpec(memory_space=pl.ANY),
        ],
        out_specs=pl.BlockSpec((tm, D_MODEL), lambda i: (i, 0)),
        scratch_shapes=[
            pltpu.VMEM((COMBINE_AHEAD + 1, STAGE_TOKENS * SUBLANES, LANES), F32),
            pltpu.VMEM((tm * SUBLANES, LANES), F32),
            pltpu.SemaphoreType.DMA((COMBINE_AHEAD + 1,)),
        ],
        compiler_params=_cparams("arbitrary"),
        name="moe_combine",
    )(*([runs] * (COMBINE_AHEAD + 1)), loc, gates_r, x, mod, fw, ys)


def _moe(h2t, idx, gates, rank, tile_base, counts, x, g2, fw, layer, wgu, bgu, wd, bd, n_ctx_rows, final_norm):
    nt = idx.shape[1]
    bm = MOE_BM
    tm = ROW_TILE
    n_tiles = nt // tm
    n_blocks = (nt * TOP_K) // bm + N_EXPERTS
    cnt = counts[:, 0].astype(I32)
    padded = (cnt + bm - 1) // bm * bm
    pad_ends = jnp.cumsum(padded)
    pad_starts = pad_ends - padded
    experts = jnp.arange(N_EXPERTS, dtype=I32)
    base = tile_base[:, :, 0].astype(I32)
    run_len = jnp.concatenate([base[1:], cnt[None]], axis=0) - base
    run_off = jnp.cumsum(run_len, axis=1) - run_len
    run_row = pad_starts[None, :] + base
    runs = jnp.concatenate([run_row, run_len, run_off], axis=1).reshape(n_tiles, 1, 3 * N_EXPERTS)
    delta = (run_off - base).T
    idx3 = idx.reshape(TOP_K, n_tiles, tm)
    loc = rank.reshape(TOP_K, n_tiles, tm) + jnp.sum(
        jnp.where(idx3[None] == experts[:, None, None, None], delta[:, None, :, None], 0), axis=0)
    by_tile = lambda a: a.transpose(1, 0, 2).reshape(n_tiles, 1, TOP_K * tm)
    loc = by_tile(loc)
    block_start = jnp.arange(n_blocks, dtype=I32) * bm
    block_e = jnp.minimum(jnp.sum((pad_ends[None, :] <= block_start[:, None]).astype(I32), axis=1), N_EXPERTS - 1)
    n_used = (pad_ends[-1] // bm).astype(I32).reshape(1)
    xs = _moe_dispatch(pad_ends.astype(I32), padded, n_used, runs, loc * PACKED_ROWS, h2t, n_blocks)
    ys = _moe_ffn(block_e + layer * N_EXPERTS, n_used, xs, wgu, bgu, wd, bd)
    return _moe_combine(runs, loc * SUBLANES, by_tile(gates.reshape(TOP_K, n_tiles, tm)), x, g2, fw, ys, n_ctx_rows,
                        final_norm)


def _pad_heads(a):
    lead = a.shape[:-1]
    a = a.reshape(*lead, 4 * ML_HEADS, ML_HEAD_DIM)
    a = jnp.pad(a, [(0, 0)] * len(lead) + [(0, 0), (0, LANES - ML_HEAD_DIM)])
    return a.reshape(*lead, 4 * ML_HEADS * LANES)


def _kvqo(a):
    w = ML_HEADS * ML_HEAD_DIM
    return jnp.concatenate([a[..., w:2 * w], a[..., 2 * w:3 * w], a[..., :w], a[..., 3 * w:4 * w]], axis=-1)


def _block_diag(blocks):
    return jax.scipy.linalg.block_diag(*[blocks[g] for g in range(blocks.shape[0])])


def _s5_discretise(lam_re, lam_im, log_dt, b_re, b_im):
    dt = jnp.exp(log_dt)[:, None]
    mag = jnp.exp(lam_re * dt)
    ar, ai = mag * jnp.cos(lam_im * dt), mag * jnp.sin(lam_im * dt)
    den = lam_re * lam_re + lam_im * lam_im
    cr = ((ar - 1.0) * lam_re + ai * lam_im) / den
    ci = (ai * lam_re - (ar - 1.0) * lam_im) / den
    bbr = cr[..., None] * b_re - ci[..., None] * b_im
    bbi = cr[..., None] * b_im + ci[..., None] * b_re
    return ar, ai, bbr, bbi


def _layer_layouts(w_in, b_in, rg_wa, rg_ba, rg_wx, rg_bx, s5_lambda_re, s5_lambda_im, s5_log_dt, s5_b_re, s5_b_im,
                   s5_c_re, s5_c_im, mix_norm_w, w_out):
    dm = w_in.shape[0]
    g0 = 4 * ML_HEADS * ML_HEAD_DIM
    ng = 2 * ML_HEADS
    kscale = jnp.ones((4, ML_HEADS * LANES), F32).at[0].set(ML_HEAD_DIM ** -0.5).reshape(-1)
    ones_col = jnp.zeros((4, ML_HEADS, LANES), F32).at[1, :, ML_HEAD_DIM].set(1.0).reshape(-1)
    w_cat = jnp.concatenate([
        _pad_heads(_kvqo(w_in[:, :g0])) * kscale,
        jnp.pad(w_in[:, g0:g0 + ng], ((0, 0), (0, LANES - ng))),
        jnp.pad(w_in[:, g0 + ng:g0 + 2 * ng], ((0, 0), (0, LANES - ng))),
        w_in[:, g0 + 2 * ng:]], axis=1).astype(BF16)
    b_cat = jnp.concatenate([
        _pad_heads(_kvqo(b_in[:g0])) * kscale + ones_col,
        jnp.pad(b_in[g0:g0 + ng], (0, LANES - ng)),
        jnp.pad(b_in[g0 + ng:g0 + 2 * ng], (0, LANES - ng)),
        b_in[g0 + 2 * ng:]])[None]
    wg = [jnp.concatenate([_block_diag(rg_wa[d]), _block_diag(rg_wx[d])], axis=1).astype(BF16) for d in range(2)]
    bg = [jnp.concatenate([rg_ba[d], rg_bx[d]])[None] for d in range(2)]
    s5p = []
    for d in range(2):
        ar, ai, bbr, bbi = _s5_discretise(s5_lambda_re[d], s5_lambda_im[d], s5_log_dt[d], s5_b_re, s5_b_im)
        a = jnp.stack([ar.reshape(-1), ai.reshape(-1)])
        bd = jnp.concatenate([_block_diag(bbr.transpose(0, 2, 1)), _block_diag(bbi.transpose(0, 2, 1))],
                             axis=1).astype(BF16)
        cd = jnp.concatenate([_block_diag(s5_c_re.transpose(0, 2, 1)),
                              -_block_diag(s5_c_im.transpose(0, 2, 1))], axis=0).astype(BF16)
        s5p.append((a, bd, cd))
    wa = ML_HEADS * ML_HEAD_DIM
    mw_a = jnp.pad(mix_norm_w[:wa].reshape(ML_HEADS, ML_HEAD_DIM), ((0, 0), (0, LANES - ML_HEAD_DIM))).reshape(-1)
    mw = jnp.concatenate([mw_a, mix_norm_w[wa:]])[None]
    wo_a = jnp.pad(w_out[:wa].reshape(ML_HEADS, ML_HEAD_DIM, dm), ((0, 0), (0, LANES - ML_HEAD_DIM), (0, 0)))
    wo = jnp.concatenate([wo_a.reshape(ML_HEADS * LANES, dm), w_out[wa:]], axis=0).astype(BF16)
    return w_cat, b_cat, wg, bg, s5p, mw, wo


def kernel(x, c, ctx, c_ctx, ada_w, ada_b, norm1_w, w_in, b_in, rg_conv_w, rg_conv_b, rg_wa, rg_ba, rg_wx, rg_bx, rg_lambda, s5_lambda_re, s5_lambda_im, s5_log_dt, s5_b_re, s5_b_im, s5_c_re, s5_c_im, s5_d, s5_glu_w, s5_glu_b, mix_norm_w, w_out, norm2_w, router_w, router_b, moe_w_gate_up, moe_b_gate_up, moe_w_down, moe_b_down, final_norm_w):
    bsz, seq, dm = x.shape
    sc = ctx.shape[1]
    assert bsz == BATCH and dm == D_MODEL and seq % GRID_W == 0
    n_ctx_rows = sc * BATCH
    n_lat_rows = seq * BATCH
    assert n_ctx_rows % CHUNK_ROWS == 0 and n_lat_rows % CHUNK_ROWS == 0 and GRID_W * BATCH == ROW_TILE
    n_ctx_chunks = n_ctx_rows // CHUNK_ROWS
    depth = ada_w.shape[0]

    xs = jnp.concatenate([ctx.transpose(1, 0, 2).reshape(n_ctx_rows, dm),
                          x.transpose(1, 0, 2).reshape(n_lat_rows, dm)], axis=0)

    c_rows = jnp.concatenate([c, c_ctx[None], jnp.zeros((16 - bsz - 1, dm), F32)], axis=0)
    mods = _modulation(c_rows, ada_w, ada_b)
    mods = mods.reshape(depth, 16, 6, dm)
    mod_lat = mods[:, :bsz].transpose(0, 2, 1, 3)
    mod_ctx = jnp.broadcast_to(mods[:, bsz][:, :, None, :], mod_lat.shape)
    mod = jnp.stack([mod_ctx, mod_lat], axis=1)

    layouts = jax.vmap(_layer_layouts)(w_in, b_in, rg_wa, rg_ba, rg_wx, rg_bx, s5_lambda_re, s5_lambda_im, s5_log_dt,
                                       s5_b_re, s5_b_im, s5_c_re, s5_c_im, mix_norm_w, w_out)

    for l in range(depth):
        with_ctx = l < depth - 1
        w_cat, b_cat, wg, bg, s5p, mw, wo = jax.tree.map(lambda a: a[l], layouts)

        pa, pb, pcc, pcl = _inproj(xs, mod[l, :, 0:2], norm1_w[l][None], w_cat, b_cat, n_ctx_rows)
        pcl = pcl.reshape(n_lat_rows, S5_WIDTH)
        c0f, m0f = _mlstm_states(pa, 0, n_ctx_chunks)
        c0b, m0b = _mlstm_states(pa, 1, n_ctx_chunks)
        ya = _mlstm_outputs(pa, c0f, m0f, c0b, m0b)
        rgb = _rglru_pass(pb, None, rg_conv_w[l], rg_conv_b[l][None], wg[1], bg[1], rg_lambda[l, 1][None], 1, n_ctx_chunks)
        yb = _rglru_pass(pb, rgb, rg_conv_w[l], rg_conv_b[l][None], wg[0], bg[0], rg_lambda[l, 0][None], 0, n_ctx_chunks)
        zero_state = jnp.zeros((BATCH, 2 * S5_NSTATE), F32)
        a1, bd1, cd1 = s5p[1]
        a0, bd0, cd0 = s5p[0]
        dsk, gw, gb = s5_d[l][None], s5_glu_w[l].astype(BF16), s5_glu_b[l][None]
        ycb_c, st = _s5_pass(pcc, None, zero_state, a1, bd1, cd1, None, None, None, 1)
        ycb_l, _ = _s5_pass(pcl, None, st, a1, bd1, cd1, None, None, None, 1)
        yc_c, st = _s5_pass(pcc, ycb_c, zero_state, a0, bd0, cd0, dsk, gw, gb, 0)
        yc_l, _ = _s5_pass(pcl, ycb_l, st, a0, bd0, cd0, dsk, gw, gb, 0)

        tile0 = 0 if with_ctx else n_ctx_rows // ROW_TILE
        xo, h2, idx, gates, rank, tile_base, counts = _outproj(
            xs, ya, yb, yc_c, yc_l, mod[l, :, 2:5], mw, wo, norm2_w[l][None], router_w[l].T,
            router_b[l][:, None], tile0, n_ctx_rows)

        xs_new = _moe(h2, idx, gates, rank, tile_base, counts, xo, mod[l, :, 5], final_norm_w[None], l,
                      moe_w_gate_up.reshape(-1, dm, 2 * D_EXPERT), moe_b_gate_up.reshape(-1, 2 * D_EXPERT),
                      moe_w_down.reshape(-1, D_EXPERT, dm), moe_b_down.reshape(-1, dm),
                      n_ctx_rows if with_ctx else 0, l == depth - 1)
        xs = xs_new

    out = xs.reshape(seq, bsz, dm).transpose(1, 0, 2)
    return out
```

```python
import functools

import jax
import jax.numpy as jnp
from jax import lax
from jax.experimental import pallas as pl
from jax.experimental.pallas import tpu as pltpu

F32 = jnp.float32
BF16 = jnp.bfloat16
I32 = jnp.int32

D_MODEL = 1024
BATCH = 8
DEPTH = 4
GRID_W = 64
ML_HEADS = 4
ML_HEAD_DIM = 96
ML_CHUNK = 128
ML_M_INIT = -1e30
RG_BLOCKS = 6
RG_BLOCK_DIM = 64
RG_WIDTH = RG_BLOCKS * RG_BLOCK_DIM
RG_CONV = 4
RG_C = 8.0
S5_GROUPS = 16
S5_GROUP = 16
S5_WIDTH = S5_GROUPS * S5_GROUP
S5_STATE = 64
S5_NSTATE = S5_GROUPS * S5_STATE
N_EXPERTS = 32
TOP_K = 4
D_EXPERT = D_MODEL
SWIGLU_LIMIT = 7.0
SWIGLU_ALPHA = 1.702
EPS = 1e-6

LANES = 128
SUBLANES = 8

ML_K, ML_V, ML_Q, ML_O = 0, ML_HEADS, 2 * ML_HEADS, 3 * ML_HEADS
ML_IG = 4 * ML_HEADS
ML_FG = ML_IG + 1
ML_SLABS = ML_FG + 1
ML_STATE_SLABS = 2 * ML_HEADS
A_PAD = ML_SLABS * LANES
MERGED = ML_HEADS * LANES + RG_WIDTH + S5_WIDTH

ROW_TILE = 512
T_CHUNK = 128
CHUNK_ROWS = T_CHUNK * BATCH
MOE_BM = 512
DMA_PRIORITIES = 2
RUN_BITS = ROW_TILE.bit_length()
VMEM_LIMIT = 56 * 1024 * 1024


def _cparams(*sem):
    return pltpu.CompilerParams(dimension_semantics=sem, vmem_limit_bytes=VMEM_LIMIT)


def _sigmoid(x):
    return 0.5 * (jnp.tanh(0.5 * x) + 1.0)


def _log_sigmoid(x):
    return jnp.minimum(x, 0.0) - jnp.log1p(jnp.exp(-jnp.abs(x)))


def _softplus(x):
    return jnp.maximum(x, 0.0) + jnp.log1p(jnp.exp(-jnp.abs(x)))


def _gelu(x):
    return 0.5 * x * (1.0 + jnp.tanh(0.7978845608028654 * (x + 0.044715 * (x * x * x))))


def _dot(a, b):
    return jnp.dot(a, b, preferred_element_type=F32)


def _dot_nt(a, b):
    return lax.dot_general(a, b, (((1,), (1,)), ((), ())), preferred_element_type=F32)


def _split(a):
    hi = a.astype(BF16)
    lo = (a - hi.astype(F32)).astype(BF16)
    return hi, lo


def _dot_lx(a_exact, b):
    hi, lo = _split(b)
    return _dot(a_exact, hi) + _dot(a_exact, lo)


def _dot_xr(a, b_exact):
    hi, lo = _split(a)
    return _dot(hi, b_exact) + _dot(lo, b_exact)


def _dot3(a, b, nt=False):
    ah, al = _split(a)
    bh, bl = _split(b)
    d = _dot_nt if nt else _dot
    return d(ah, bh) + (d(ah, bl) + d(al, bh))


def _group_ones(n, shift):
    r = lax.broadcasted_iota(I32, (n, n), 0)
    c = lax.broadcasted_iota(I32, (n, n), 1)
    return (lax.shift_right_logical(r, shift) == lax.shift_right_logical(c, shift)).astype(BF16)


def _store_token_tiles(ref, tok0, val):
    n, width = val.shape
    rows = width // LANES
    for s in range(rows):
        ref[pl.ds(tok0 * rows + s, n, stride=rows), :] = val[:, s * LANES:(s + 1) * LANES]


def _load_token_tiles(ref, tok0, n, rows=SUBLANES):
    return jnp.concatenate([ref[pl.ds(tok0 * rows + s, n, stride=rows), :] for s in range(rows)], axis=1)


def _pack_bf16_pairs(x):
    w = x.shape[1] // 2
    bits = lax.bitcast_convert_type(x.astype(BF16).astype(F32), I32)
    return (bits[:, :w] & jnp.int32(-65536)) | lax.shift_right_logical(bits[:, w:], 16)


def _unpack_bf16_pairs(words):
    hi = lax.bitcast_convert_type(words & jnp.int32(-65536), F32)
    lo = lax.bitcast_convert_type(lax.shift_left(words, 16), F32)
    return jnp.concatenate([hi, lo], axis=1).astype(BF16)


def _mod_kernel(c_ref, w_ref, b_ref, o_ref):
    c = c_ref[...]
    s = c * _sigmoid(c)
    o_ref[0] = _dot3(s, w_ref[0]) + b_ref[0]


def _modulation(c_rows, ada_w, ada_b):
    depth, d, n = ada_w.shape
    tn = 1536
    return pl.pallas_call(
        _mod_kernel,
        out_shape=jax.ShapeDtypeStruct((depth, 16, n), F32),
        grid=(depth, n // tn),
        in_specs=[
            pl.BlockSpec((16, d), lambda l, j: (0, 0)),
            pl.BlockSpec((1, d, tn), lambda l, j: (l, 0, j)),
            pl.BlockSpec((1, 1, tn), lambda l, j: (l, 0, j)),
        ],
        out_specs=pl.BlockSpec((1, 16, tn), lambda l, j: (l, 0, j)),
        compiler_params=_cparams("arbitrary", "arbitrary"),
        name="adaln_mod",
    )(c_rows, ada_w, ada_b.reshape(depth, 1, n))


def _inproj_kernel(x_ref, mod_ref, nw_ref, w_ref, b_ref, pa_ref, pb_ref, pcc_ref, pcl_ref, *, n_ctx_tiles):
    i = pl.program_id(0)
    x = x_ref[...]
    tm = x.shape[0]
    ms = jnp.mean(x * x, axis=-1, keepdims=True)
    xn = (x * lax.rsqrt(ms + EPS)) * nw_ref[...]
    xn = xn.reshape(tm // BATCH, BATCH, D_MODEL)
    h = xn * (1.0 + mod_ref[0, 1][None]) + mod_ref[0, 0][None]
    h = h.reshape(tm, D_MODEL).astype(BF16)
    for j in range(0, ML_SLABS, 2):
        p = _dot(h, w_ref[:, j * LANES:(j + 2) * LANES]) + b_ref[:, j * LANES:(j + 2) * LANES]
        pa_ref[j] = p[:, :LANES]
        pa_ref[j + 1] = p[:, LANES:]
    c0 = A_PAD
    pb_ref[...] = _dot(h, w_ref[:, c0:c0 + 2 * RG_WIDTH]) + b_ref[:, c0:c0 + 2 * RG_WIDTH]
    c0 = A_PAD + 2 * RG_WIDTH
    pc = _dot(h, w_ref[:, c0:c0 + S5_WIDTH]) + b_ref[:, c0:c0 + S5_WIDTH]

    @pl.when(i < n_ctx_tiles)
    def _():
        pcc_ref[...] = pc

    @pl.when(i >= n_ctx_tiles)
    def _():
        pcl_ref[...] = pc.reshape(GRID_W, 1, BATCH, S5_WIDTH)


def _inproj(x, mod, nw, w, b, n_ctx_rows):
    nt = x.shape[0]
    tm = ROW_TILE
    nct = n_ctx_rows // tm
    n_lat_rows = nt - n_ctx_rows
    rows = n_lat_rows // tm
    ncols = w.shape[1]
    return pl.pallas_call(
        functools.partial(_inproj_kernel, n_ctx_tiles=nct),
        out_shape=(
            jax.ShapeDtypeStruct((ML_SLABS, nt, LANES), F32),
            jax.ShapeDtypeStruct((nt, 2 * RG_WIDTH), F32),
            jax.ShapeDtypeStruct((n_ctx_rows, S5_WIDTH), F32),
            jax.ShapeDtypeStruct((GRID_W, rows, BATCH, S5_WIDTH), F32),
        ),
        grid=(nt // tm,),
        in_specs=[
            pl.BlockSpec((tm, D_MODEL), lambda i: (i, 0)),
            pl.BlockSpec((1, 2, BATCH, D_MODEL), lambda i: (jnp.where(i < nct, 0, 1), 0, 0, 0)),
            pl.BlockSpec((1, D_MODEL), lambda i: (0, 0)),
            pl.BlockSpec((D_MODEL, ncols), lambda i: (0, 0)),
            pl.BlockSpec((1, ncols), lambda i: (0, 0)),
        ],
        out_specs=(
            pl.BlockSpec((ML_SLABS, tm, LANES), lambda i: (0, i, 0)),
            pl.BlockSpec((tm, 2 * RG_WIDTH), lambda i: (i, 0)),
            pl.BlockSpec((tm, S5_WIDTH), lambda i: (jnp.minimum(i, nct - 1), 0)),
            pl.BlockSpec((GRID_W, 1, BATCH, S5_WIDTH), lambda i: (0, jnp.maximum(i - nct, 0), 0, 0)),
        ),
        compiler_params=_cparams("arbitrary"),
        name="inproj",
    )(x, mod, nw, w, b)


def _mlstm_gates(ig, fg):
    L = ML_CHUNK
    row = lax.broadcasted_iota(I32, (L, L), 0)
    col = lax.broadcasted_iota(I32, (L, L), 1)
    lf = _log_sigmoid(fg)
    b_fwd = _dot_lx((row >= col).astype(BF16), lf)
    b_tot = _dot_lx(jnp.ones((L, L), BF16), lf)
    b_bwd = b_tot - b_fwd + lf
    bcol = jnp.where(col < ML_HEADS, b_fwd, b_bwd)
    return ig - bcol, bcol, b_tot


def _mlstm_state_kernel(kv_ref, g_ref, c0_ref, m0_ref, c_sc, m_sc, *, direction):
    i = pl.program_id(0)
    L = ML_CHUNK

    @pl.when(i == 0)
    def _():
        c_sc[...] = jnp.zeros(c_sc.shape, F32)
        m_sc[...] = jnp.full(m_sc.shape, ML_M_INIT, F32)

    def per_batch(b, carry):
        rows = pl.ds(b, L, stride=BATCH)
        z, _, b_tot = _mlstm_gates(g_ref[0, rows, :], g_ref[1, rows, :])
        z_t = z.T
        for h in range(ML_HEADS):
            j = ML_HEADS * direction + h
            idx = b * ML_HEADS + h
            k = kv_ref[ML_K + h, rows, :].astype(BF16)
            v_t = kv_ref[ML_V + h, rows, :].T
            c0 = c_sc[idx]
            m0_tile = m_sc[idx]
            c0_ref[0, idx] = c0.astype(BF16)
            m0_ref[0, idx] = m0_tile
            m0 = m0_tile[0:1, :]
            bt = jnp.broadcast_to(b_tot[0:1, j:j + 1], (1, LANES))
            w = bt + z_t[j:j + 1, :]
            mloc = jnp.broadcast_to(jnp.max(w, axis=1, keepdims=True), (1, LANES))
            cloc = _dot((v_t * jnp.exp(w - mloc)).astype(BF16), k)
            mnew = jnp.maximum(bt + m0, mloc)
            a = jnp.exp(bt + m0 - mnew)
            sc = jnp.exp(mloc - mnew)
            c_sc[idx] = a * c0 + sc * cloc
            m_sc[idx] = jnp.broadcast_to(mnew, (SUBLANES, LANES))
        return carry

    lax.fori_loop(0, BATCH, per_batch, 0, unroll=4)


def _mlstm_out_kernel(pa_ref, c0f_ref, m0f_ref, c0b_ref, m0b_ref, out_ref):
    L = ML_CHUNK
    row = lax.broadcasted_iota(I32, (L, L), 0)
    col = lax.broadcasted_iota(I32, (L, L), 1)
    masks = (row <= col, row >= col)
    head_row = row < ML_HEAD_DIM
    m0_refs = (m0f_ref, m0b_ref)

    def per_batch(b, carry):
        rows = pl.ds(b, L, stride=BATCH)
        z, bcol, _ = _mlstm_gates(pa_ref[ML_IG, rows, :], pa_ref[ML_FG, rows, :])
        bcol_t = bcol.T
        for h in range(ML_HEADS):
            idx = b * ML_HEADS + h
            q = pa_ref[ML_Q + h, rows, :].astype(BF16)
            k = pa_ref[ML_K + h, rows, :].astype(BF16)
            v_t = pa_ref[ML_V + h, rows, :].T.astype(BF16)
            s_t = _dot_nt(k, q)
            nc_t = _dot_nt(jnp.concatenate([c0f_ref[0, idx], c0b_ref[0, idx]], axis=0), q)
            w_t = None
            h_t = None
            for d in range(2):
                j = ML_HEADS * d + h
                m0 = m0_refs[d][0, idx][0:1, :]
                r = jnp.where(masks[d], z[:, j:j + 1], -jnp.inf)
                u = jnp.maximum(jnp.max(r, axis=0, keepdims=True), m0)
                sqk = s_t * jnp.exp(r - u)
                inter = jnp.exp(m0 - u)
                ncd = nc_t[d * LANES:(d + 1) * LANES, :]
                den = jnp.sum(sqk, axis=0, keepdims=True) + inter * ncd[ML_HEAD_DIM:ML_HEAD_DIM + 1, :]
                inv = 1.0 / jnp.maximum(jnp.abs(den), jnp.exp(-(bcol_t[j:j + 1, :] + u)))
                w_t = sqk * inv if w_t is None else w_t + sqk * inv
                h_t = (inter * inv) * ncd if h_t is None else h_t + (inter * inv) * ncd
            h_t = jnp.where(head_row, h_t + _dot(v_t, w_t.astype(BF16)), 0.0)
            ms = jnp.sum(h_t * h_t, axis=0, keepdims=True) * (1.0 / ML_HEAD_DIM)
            o_t = pa_ref[ML_O + h, rows, :].T
            out_ref[h, rows, :] = (_sigmoid(o_t) * (h_t * lax.rsqrt(ms + EPS))).T
        return carry

    lax.fori_loop(0, BATCH, per_batch, 0, unroll=4)


def _chunk_order(i, direction, n_ctx, n_all):
    if direction == 0:
        return i
    return jnp.where(i < n_ctx, n_ctx - 1 - i, n_all - 1 + n_ctx - i)


def _mlstm_states(pa, direction, n_ctx):
    nt = pa.shape[1]
    n_all = nt // CHUNK_ROWS
    nbh = BATCH * ML_HEADS
    order = lambda i: _chunk_order(i, direction, n_ctx, n_all)
    return pl.pallas_call(
        functools.partial(_mlstm_state_kernel, direction=direction),
        out_shape=(jax.ShapeDtypeStruct((n_all, nbh, ML_CHUNK, LANES), BF16),
                   jax.ShapeDtypeStruct((n_all, nbh, SUBLANES, LANES), F32)),
        grid=(n_all,),
        in_specs=[
            pl.BlockSpec((ML_STATE_SLABS, CHUNK_ROWS, LANES), lambda i: (0, order(i), 0)),
            pl.BlockSpec((2, CHUNK_ROWS, LANES), lambda i: (ML_IG // 2, order(i), 0)),
        ],
        out_specs=(pl.BlockSpec((1, nbh, ML_CHUNK, LANES), lambda i: (order(i), 0, 0, 0)),
                   pl.BlockSpec((1, nbh, SUBLANES, LANES), lambda i: (order(i), 0, 0, 0))),
        scratch_shapes=[
            pltpu.VMEM((nbh, ML_CHUNK, LANES), F32),
            pltpu.VMEM((nbh, SUBLANES, LANES), F32),
        ],
        compiler_params=_cparams("arbitrary"),
        name="mlstm_state_fwd" if direction == 0 else "mlstm_state_bwd",
    )(pa, pa)


def _mlstm_outputs(pa, c0f, m0f, c0b, m0b):
    nt = pa.shape[1]
    n_all = nt // CHUNK_ROWS
    nbh = BATCH * ML_HEADS
    cspec = pl.BlockSpec((1, nbh, ML_CHUNK, LANES), lambda i: (i, 0, 0, 0))
    mspec = pl.BlockSpec((1, nbh, SUBLANES, LANES), lambda i: (i, 0, 0, 0))
    return pl.pallas_call(
        _mlstm_out_kernel,
        out_shape=jax.ShapeDtypeStruct((ML_HEADS, nt, LANES), F32),
        grid=(n_all,),
        in_specs=[pl.BlockSpec((ML_SLABS, CHUNK_ROWS, LANES), lambda i: (0, i, 0)), cspec, mspec, cspec, mspec],
        out_specs=pl.BlockSpec((ML_HEADS, CHUNK_ROWS, LANES), lambda i: (0, i, 0)),
        compiler_params=_cparams("arbitrary"),
        name="mlstm_out",
    )(pa, c0f, m0f, c0b, m0b)


def _rglru_kernel(*refs, direction, final, n_ctx, n_all):
    if final:
        cur_ref, prev_ref, next_ref, hb_ref, cw_ref, cb_ref, wg_ref, bg_ref, lam_ref, out_ref, a_sc, b_sc, h_sc = refs
    else:
        cur_ref, prev_ref, next_ref, cw_ref, cb_ref, wg_ref, bg_ref, lam_ref, out_ref, a_sc, b_sc, h_sc = refs
        hb_ref = None
    i = pl.program_id(0)
    c = _chunk_order(i, direction, n_ctx, n_all)
    w = RG_WIDTH

    @pl.when(i == 0)
    def _():
        h_sc[...] = jnp.zeros(h_sc.shape, F32)

    seg_first = jnp.logical_or(c == 0, c == n_ctx)
    seg_last = jnp.logical_or(c == n_ctx - 1, c == n_all - 1)
    cur = cur_ref[:, :w]
    prev = jnp.where(seg_first, 0.0, prev_ref[:, :w])
    nxt = jnp.where(seg_last, 0.0, next_ref[:, :w])
    xc = jnp.concatenate([prev, cur, nxt], axis=0)
    n = CHUNK_ROWS
    xconv = cb_ref[...] + cw_ref[0:1, :] * xc[0:n]
    for j in range(1, RG_CONV):
        xconv = xconv + cw_ref[j:j + 1, :] * xc[j * BATCH:j * BATCH + n]
    z = _dot(xconv.astype(BF16), wg_ref[...]) + bg_ref[...]
    r = _sigmoid(z[:, :w])
    ig = _sigmoid(z[:, w:])
    log_a = (-RG_C * _softplus(-lam_ref[...])) * r
    a = jnp.exp(log_a)
    a_sc[...] = a
    b_sc[...] = jnp.sqrt(-jnp.tanh(log_a) * (a * a + 1.0)) * (ig * xconv)

    def step(t, h):
        tt = t if direction == 0 else T_CHUNK - 1 - t
        rows = pl.ds(pl.multiple_of(tt * BATCH, BATCH), BATCH)
        h = a_sc[rows, :] * h + b_sc[rows, :]
        b_sc[rows, :] = h
        return h

    h_sc[...] = lax.fori_loop(0, T_CHUNK, step, h_sc[...], unroll=8)
    hs = b_sc[...]
    if final:
        ht = hs + hb_ref[...]
        ss = _dot_xr(ht * ht, _group_ones(w, 6)) * (1.0 / RG_BLOCK_DIM)
        hs = _gelu(cur_ref[:, w:]) * (ht * lax.rsqrt(ss + EPS))
    out_ref[...] = hs


def _rglru_pass(pb, hb, cw, cb, wg, bg, lam, direction, n_ctx):
    nt = pb.shape[0]
    n_all = nt // CHUNK_ROWS
    final = hb is not None
    w2 = 2 * RG_WIDTH
    order = lambda i: _chunk_order(i, direction, n_ctx, n_all)
    halo_p = CHUNK_ROWS // (2 * BATCH)
    halo_n = CHUNK_ROWS // BATCH
    in_specs = [
        pl.BlockSpec((CHUNK_ROWS, w2), lambda i: (order(i), 0)),
        pl.BlockSpec((2 * BATCH, w2), lambda i: (jnp.maximum(order(i) * halo_p - 1, 0), 0)),
        pl.BlockSpec((BATCH, w2), lambda i: (jnp.minimum((order(i) + 1) * halo_n, nt // BATCH - 1), 0)),
    ]
    args = [pb, pb, pb]
    if final:
        in_specs.append(pl.BlockSpec((CHUNK_ROWS, RG_WIDTH), lambda i: (order(i), 0)))
        args.append(hb)
    full = lambda s: pl.BlockSpec(s, lambda i: tuple(0 for _ in s))
    in_specs += [full((RG_CONV, RG_WIDTH)), full((1, RG_WIDTH)), full((RG_WIDTH, w2)), full((1, w2)), full((1, RG_WIDTH))]
    args += [cw, cb, wg, bg, lam]
    return pl.pallas_call(
        functools.partial(_rglru_kernel, direction=direction, final=final, n_ctx=n_ctx, n_all=n_all),
        out_shape=jax.ShapeDtypeStruct((nt, RG_WIDTH), F32),
        grid=(n_all,),
        in_specs=in_specs,
        out_specs=pl.BlockSpec((CHUNK_ROWS, RG_WIDTH), lambda i: (order(i), 0)),
        scratch_shapes=[
            pltpu.VMEM((CHUNK_ROWS, RG_WIDTH), F32),
            pltpu.VMEM((CHUNK_ROWS, RG_WIDTH), F32),
            pltpu.VMEM((BATCH, RG_WIDTH), F32),
        ],
        compiler_params=_cparams("arbitrary"),
        name="rglru_fwd" if final else "rglru_bwd",
    )(*args)


def _s5_kernel(*refs, direction, final):
    if final:
        (u_ref, yb_ref, h0_ref, a_ref, bd_ref, cd_ref, dsk_ref, gw_ref, gb_ref,
         out_ref, hN_ref, v_sc, h_sc) = refs
    else:
        u_ref, h0_ref, a_ref, bd_ref, cd_ref, out_ref, hN_ref, v_sc, h_sc = refs
    i = pl.program_id(0)
    n = S5_NSTATE

    @pl.when(i == 0)
    def _():
        h_sc[...] = h0_ref[...]

    u = u_ref[...]
    v_sc[...] = _dot(u.astype(BF16), bd_ref[...])
    ar = jnp.broadcast_to(a_ref[0:1, :], (BATCH, n))
    ai = jnp.broadcast_to(a_ref[1:2, :], (BATCH, n))

    def step(t, carry):
        hr, hi = carry
        tt = t if direction == 0 else T_CHUNK - 1 - t
        rows = pl.ds(pl.multiple_of(tt * BATCH, BATCH), BATCH)
        nr = ar * hr - ai * hi + v_sc[rows, :n]
        ni = ar * hi + ai * hr + v_sc[rows, n:]
        v_sc[rows, :n] = nr
        v_sc[rows, n:] = ni
        return nr, ni

    hr, hi = lax.fori_loop(0, T_CHUNK, step, (h_sc[:, :n], h_sc[:, n:]))
    h_sc[:, :n] = hr
    h_sc[:, n:] = hi
    hN_ref[...] = h_sc[...]
    y = _dot(v_sc[...].astype(BF16), cd_ref[...])
    if final:
        y = y + yb_ref[...] + dsk_ref[...] * u
        g = _gelu(y)
        o = g * _sigmoid(_dot(g.astype(BF16), gw_ref[...]) + gb_ref[...])
        ss = _dot_xr(o * o, _group_ones(S5_WIDTH, 4)) * (1.0 / S5_GROUP)
        y = o * lax.rsqrt(ss + EPS)
    out_ref[...] = y


def _s5_pass(u, yb, h0, a, bd, cd, dsk, gw, gb, direction):
    nt = u.shape[0]
    n_chunks = nt // CHUNK_ROWS
    final = yb is not None
    order = (lambda i: i) if direction == 0 else (lambda i: n_chunks - 1 - i)
    full = lambda s: pl.BlockSpec(s, lambda i: tuple(0 for _ in s))
    chunk = pl.BlockSpec((CHUNK_ROWS, S5_WIDTH), lambda i: (order(i), 0))
    in_specs = [chunk]
    args = [u]
    if final:
        in_specs.append(chunk)
        args.append(yb)
    in_specs += [full((BATCH, 2 * S5_NSTATE)), full((2, S5_NSTATE)), full((S5_WIDTH, 2 * S5_NSTATE)),
                 full((2 * S5_NSTATE, S5_WIDTH))]
    args += [h0, a, bd, cd]
    if final:
        in_specs += [full((1, S5_WIDTH)), full((S5_WIDTH, S5_WIDTH)), full((1, S5_WIDTH))]
        args += [dsk, gw, gb]
    return pl.pallas_call(
        functools.partial(_s5_kernel, direction=direction, final=final),
        out_shape=(jax.ShapeDtypeStruct((nt, S5_WIDTH), F32), jax.ShapeDtypeStruct((BATCH, 2 * S5_NSTATE), F32)),
        grid=(n_chunks,),
        in_specs=in_specs,
        out_specs=(chunk, full((BATCH, 2 * S5_NSTATE))),
        scratch_shapes=[
            pltpu.VMEM((CHUNK_ROWS, 2 * S5_NSTATE), F32),
            pltpu.VMEM((BATCH, 2 * S5_NSTATE), F32),
        ],
        compiler_params=_cparams("arbitrary"),
        name="s5_fwd" if final else "s5_bwd",
    )(*args)


def _outproj_kernel(x_ref, ya_ref, yb_ref, ycc_ref, ycl_ref, mod_ref, mw_ref, wo_ref, n2_ref, rw_ref, rb_ref,
                    xo_ref, h2_ref, idx_ref, gate_ref, rank_ref, base_ref, cnt_ref, cnt_sc, *, tile0, n_ctx_tiles):
    i = pl.program_id(0)
    tile = i + tile0
    tm = x_ref.shape[0]

    @pl.when(i == 0)
    def _():
        cnt_sc[...] = jnp.zeros(cnt_sc.shape, F32)

    yc = jnp.where(tile < n_ctx_tiles, ycc_ref[...], ycl_ref[...].reshape(tm, S5_WIDTH))
    merged = jnp.concatenate([ya_ref[h] for h in range(ML_HEADS)] + [yb_ref[...], yc], axis=1)
    merged = (merged * mw_ref[...]).astype(BF16)
    proj = _dot(merged, wo_ref[...])
    g1 = mod_ref[0, 0]
    x = x_ref[...].reshape(tm // BATCH, BATCH, D_MODEL) + g1[None] * proj.reshape(tm // BATCH, BATCH, D_MODEL)
    xo_ref[...] = x.reshape(tm, D_MODEL)
    ms = jnp.mean(x * x, axis=-1, keepdims=True)
    xn = (x * lax.rsqrt(ms + EPS)) * n2_ref[...][None]
    h2 = (xn * (1.0 + mod_ref[0, 2][None]) + mod_ref[0, 1][None]).reshape(tm, D_MODEL)
    _store_token_tiles(h2_ref, 0, _pack_bf16_pairs(h2))
    logits = _dot3(rw_ref[...], h2, nt=True) + rb_ref[...]
    eidx = lax.broadcasted_iota(I32, (N_EXPERTS, tm), 0)
    vals, ids, hots = [], [], []
    for _ in range(TOP_K):
        m = jnp.max(logits, axis=0, keepdims=True)
        sel = jnp.min(jnp.where(logits == m, eidx, N_EXPERTS), axis=0, keepdims=True)
        hot = eidx == sel
        logits = jnp.where(hot, -jnp.inf, logits)
        vals.append(m)
        ids.append(sel)
        hots.append(hot)
    ex = [jnp.exp(v - vals[0]) for v in vals]
    tot = ex[0] + ex[1] + ex[2] + ex[3]
    idx_ref[...] = jnp.concatenate(ids, axis=0)
    gate_ref[...] = jnp.concatenate([e / tot for e in ex], axis=0)
    selm = hots[0].astype(F32) + hots[1].astype(F32) + hots[2].astype(F32) + hots[3].astype(F32)
    r = lax.broadcasted_iota(I32, (tm, tm), 0)
    c = lax.broadcasted_iota(I32, (tm, tm), 1)
    before = _dot(selm.astype(BF16), (r < c).astype(BF16)) + cnt_sc[:, 0:1]
    ranks = [jnp.sum(jnp.where(hot, before, 0.0), axis=0, keepdims=True) for hot in hots]
    rank_ref[...] = jnp.concatenate(ranks, axis=0).astype(I32)
    base_ref[0] = cnt_sc[...]
    cnt_sc[...] = cnt_sc[...] + jnp.sum(selm, axis=1, keepdims=True)
    cnt_ref[...] = cnt_sc[...]


def _outproj(x, ya, yb, ycc, ycl, mod, mw, wo, n2, rw, rb, tile0, n_ctx_rows):
    nt = x.shape[0]
    tm = ROW_TILE
    nct = n_ctx_rows // tm
    n_tiles = nt // tm - tile0
    n_out = n_tiles * tm
    rows = ycl.shape[0] // tm
    ycl4 = ycl.reshape(GRID_W, rows, BATCH, S5_WIDTH)
    full = lambda s: pl.BlockSpec(s, lambda i: tuple(0 for _ in s))
    return pl.pallas_call(
        functools.partial(_outproj_kernel, tile0=tile0, n_ctx_tiles=nct),
        out_shape=(
            jax.ShapeDtypeStruct((n_out, D_MODEL), F32),
            jax.ShapeDtypeStruct((n_out * PACKED_ROWS, LANES), I32),
            jax.ShapeDtypeStruct((TOP_K, n_out), I32),
            jax.ShapeDtypeStruct((TOP_K, n_out), F32),
            jax.ShapeDtypeStruct((TOP_K, n_out), I32),
            jax.ShapeDtypeStruct((n_tiles, N_EXPERTS, LANES), F32),
            jax.ShapeDtypeStruct((N_EXPERTS, LANES), F32),
        ),
        grid=(n_tiles,),
        in_specs=[
            pl.BlockSpec((tm, D_MODEL), lambda i: (i + tile0, 0)),
            pl.BlockSpec((ML_HEADS, tm, LANES), lambda i: (0, i + tile0, 0)),
            pl.BlockSpec((tm, RG_WIDTH), lambda i: (i + tile0, 0)),
            pl.BlockSpec((tm, S5_WIDTH), lambda i: (jnp.minimum(i + tile0, nct - 1), 0)),
            pl.BlockSpec((GRID_W, 1, BATCH, S5_WIDTH), lambda i: (0, jnp.maximum(i + tile0 - nct, 0), 0, 0)),
            pl.BlockSpec((1, 3, BATCH, D_MODEL), lambda i: (jnp.where(i + tile0 < nct, 0, 1), 0, 0, 0)),
            full((1, MERGED)),
            full((MERGED, D_MODEL)),
            full((1, D_MODEL)),
            full((N_EXPERTS, D_MODEL)),
            full((N_EXPERTS, 1)),
        ],
        out_specs=(
            pl.BlockSpec((tm, D_MODEL), lambda i: (i, 0)),
            pl.BlockSpec((tm * PACKED_ROWS, LANES), lambda i: (i, 0)),
            pl.BlockSpec((TOP_K, tm), lambda i: (0, i)),
            pl.BlockSpec((TOP_K, tm), lambda i: (0, i)),
            pl.BlockSpec((TOP_K, tm), lambda i: (0, i)),
            pl.BlockSpec((1, N_EXPERTS, LANES), lambda i: (i, 0, 0)),
            full((N_EXPERTS, LANES)),
        ),
        scratch_shapes=[pltpu.VMEM((N_EXPERTS, LANES), F32)],
        compiler_params=_cparams("arbitrary"),
        name="outproj_router",
    )(x, ya, yb, ycc, ycl4, mod, mw, wo, n2, rw, rb)


STAGE_TOKENS = TOP_K * ROW_TILE
COMBINE_AHEAD = 2
DISPATCH_BUFFERS = 2
PACKED_ROWS = D_MODEL // (2 * LANES)


def _start_run_copies(hbm, stage, runs_ref, sem, to_hbm, tok_rows):
    def per_expert(e, carry):
        row = runs_ref[0, 0, e]
        length = runs_ref[0, 0, N_EXPERTS + e]
        off = runs_ref[0, 0, 2 * N_EXPERTS + e]
        for bit in range(RUN_BITS):
            size = (1 << bit) * tok_rows

            @pl.when((lax.shift_right_logical(length, bit) & 1) == 1)
            def _():
                done = lax.shift_left(lax.shift_right_logical(length, bit + 1), bit + 1)
                h = hbm.at[pl.ds(pl.multiple_of((row + done) * tok_rows, tok_rows), size), :]
                s = stage.at[pl.ds(pl.multiple_of((off + done) * tok_rows, tok_rows), size), :]
                cp = pltpu.make_async_copy(s, h, sem) if to_hbm else pltpu.make_async_copy(h, s, sem)
                cp.start(priority=bit % DMA_PRIORITIES)
        return carry

    lax.fori_loop(0, N_EXPERTS, per_expert, 0)


def _wait_run_copies(hbm, stage, sem, to_hbm, tok_rows):
    h = hbm.at[pl.ds(0, STAGE_TOKENS * tok_rows), :]
    cp = pltpu.make_async_copy(stage, h, sem) if to_hbm else pltpu.make_async_copy(h, stage, sem)
    cp.wait()


def _dispatch_kernel(pe_ref, pd_ref, nu_ref, runs_ref, loc_ref, h_ref, xs_hbm, stage, zero_sc, sem, zsem, *,
                     n_blocks, n_tiles):
    i = pl.program_id(0)
    tm = ROW_TILE
    tr = PACKED_ROWS

    def zero_block(blk):
        start = pl.multiple_of(blk * (MOE_BM * tr), SUBLANES)
        return pltpu.make_async_copy(zero_sc, xs_hbm.at[pl.ds(start, MOE_BM * tr), :], zsem)

    @pl.when(i == 0)
    def _():
        zero_sc[...] = jnp.zeros(zero_sc.shape, I32)
        for e in range(N_EXPERTS):
            @pl.when(pd_ref[e] > 0)
            def _():
                zero_block(pe_ref[e] // MOE_BM - 1).start()
        for e in range(N_EXPERTS):
            @pl.when(pd_ref[e] > 0)
            def _():
                zero_block(0).wait()

        def fill(blk, carry):
            cp = zero_block(blk)
            cp.start()
            cp.wait()
            return carry

        lax.fori_loop(nu_ref[0], n_blocks, fill, 0)

    buf = lax.rem(i, DISPATCH_BUFFERS)

    def place(r, carry):
        t = h_ref[pl.ds(pl.multiple_of(r * tr, tr), tr), :]
        for k in range(TOP_K):
            stage[pl.ds(pl.multiple_of(loc_ref[0, 0, TOP_K * r + k], tr), tr), :] = t
        return carry

    lax.fori_loop(0, tm, place, 0, unroll=4)
    rows = STAGE_TOKENS * tr
    view = lambda b: stage.at[pl.ds(pl.multiple_of(b * rows, SUBLANES), rows), :]
    _start_run_copies(xs_hbm, view(buf), runs_ref, sem.at[buf], to_hbm=True, tok_rows=tr)

    @pl.when(i > 0)
    def _():
        _wait_run_copies(xs_hbm, view(1 - buf), sem.at[1 - buf], to_hbm=True, tok_rows=tr)

    @pl.when(i == n_tiles - 1)
    def _():
        _wait_run_copies(xs_hbm, view(buf), sem.at[buf], to_hbm=True, tok_rows=tr)


def _moe_dispatch(pad_ends, padded, n_used, runs, loc, h2t, n_blocks):
    tm = ROW_TILE
    n_tiles = runs.shape[0]
    smem = lambda n: pl.BlockSpec((1, 1, n), lambda i, pe, pd, nu: (i, 0, 0), memory_space=pltpu.SMEM)
    grid_spec = pltpu.PrefetchScalarGridSpec(
        num_scalar_prefetch=3,
        grid=(n_tiles,),
        in_specs=[
            smem(3 * N_EXPERTS),
            smem(TOP_K * tm),
            pl.BlockSpec((tm * PACKED_ROWS, LANES), lambda i, pe, pd, nu: (i, 0)),
        ],
        out_specs=pl.BlockSpec(memory_space=pl.ANY),
        scratch_shapes=[
            pltpu.VMEM((DISPATCH_BUFFERS * STAGE_TOKENS * PACKED_ROWS, LANES), I32),
            pltpu.VMEM((MOE_BM * PACKED_ROWS, LANES), I32),
            pltpu.SemaphoreType.DMA((2,)),
            pltpu.SemaphoreType.DMA(()),
        ],
    )
    return pl.pallas_call(
        functools.partial(_dispatch_kernel, n_blocks=n_blocks, n_tiles=n_tiles),
        out_shape=jax.ShapeDtypeStruct((n_blocks * MOE_BM * PACKED_ROWS, LANES), I32),
        grid_spec=grid_spec,
        compiler_params=_cparams("arbitrary"),
        name="moe_dispatch",
    )(pad_ends, padded, n_used, runs, loc, h2t)


def _ffn_kernel(be_ref, nu_ref, x_ref, wgu_ref, bgu_ref, wd_ref, bd_ref, y_ref, wgu_sc, wd_sc):
    i = pl.program_id(0)
    changed = jnp.logical_or(i == 0, be_ref[i] != be_ref[jnp.maximum(i - 1, 0)])

    @pl.when(changed)
    def _():
        wgu_sc[...] = wgu_ref[0].astype(BF16)
        wd_sc[...] = wd_ref[0].astype(BF16)

    @pl.when(i < nu_ref[0])
    def _():
        x = _unpack_bf16_pairs(_load_token_tiles(x_ref, 0, MOE_BM, PACKED_ROWS))
        gu = _dot(x, wgu_sc[...]) + bgu_ref[0]
        gate = jnp.minimum(gu[:, :D_EXPERT], SWIGLU_LIMIT)
        up = jnp.clip(gu[:, D_EXPERT:], -SWIGLU_LIMIT, SWIGLU_LIMIT)
        glu = gate * _sigmoid(SWIGLU_ALPHA * gate)
        act = ((up + 1.0) * glu).astype(BF16)
        _store_token_tiles(y_ref, 0, _dot(act, wd_sc[...]) + bd_ref[0])

    @pl.when(i >= nu_ref[0])
    def _():
        y_ref[...] = jnp.zeros(y_ref.shape, F32)


def _moe_ffn(block_e, n_used, xs, wgu, bgu, wd, bd):
    rows = MOE_BM * SUBLANES
    in_rows = MOE_BM * PACKED_ROWS
    n_blocks = xs.shape[0] // in_rows
    ne = wgu.shape[0]
    grid_spec = pltpu.PrefetchScalarGridSpec(
        num_scalar_prefetch=2,
        grid=(n_blocks,),
        in_specs=[
            pl.BlockSpec((in_rows, LANES), lambda i, be, nu: (jnp.minimum(i, nu[0] - 1), 0)),
            pl.BlockSpec((1, D_MODEL, 2 * D_EXPERT), lambda i, be, nu: (be[i], 0, 0)),
            pl.BlockSpec((1, 1, 2 * D_EXPERT), lambda i, be, nu: (be[i], 0, 0)),
            pl.BlockSpec((1, D_EXPERT, D_MODEL), lambda i, be, nu: (be[i], 0, 0)),
            pl.BlockSpec((1, 1, D_MODEL), lambda i, be, nu: (be[i], 0, 0)),
        ],
        out_specs=pl.BlockSpec((rows, LANES), lambda i, be, nu: (i, 0)),
        scratch_shapes=[
            pltpu.VMEM((D_MODEL, 2 * D_EXPERT), BF16),
            pltpu.VMEM((D_EXPERT, D_MODEL), BF16),
        ],
    )
    return pl.pallas_call(
        _ffn_kernel,
        out_shape=jax.ShapeDtypeStruct((n_blocks * rows, LANES), F32),
        grid_spec=grid_spec,
        compiler_params=_cparams("arbitrary"),
        name="moe_ffn",
    )(block_e, n_used, xs, wgu, bgu.reshape(ne, 1, -1), wd, bd.reshape(ne, 1, -1))


def _combine_kernel(*refs, final_norm, n_tiles):
    runs_refs = refs[:COMBINE_AHEAD + 1]
    loc_ref, gate_ref, x_ref, mod_ref, fw_ref, ys_hbm, out_ref, stage, comb, sem = refs[COMBINE_AHEAD + 1:]
    i = pl.program_id(0)
    tm = x_ref.shape[0]
    n_buf = COMBINE_AHEAD + 1
    buf = lax.rem(i, n_buf)

    rows = STAGE_TOKENS * SUBLANES
    view = lambda b: stage.at[pl.ds(pl.multiple_of(b * rows, SUBLANES), rows), :]

    def fetch(slot, runs_ref):
        _start_run_copies(ys_hbm, view(slot), runs_ref, sem.at[slot], to_hbm=False, tok_rows=SUBLANES)

    @pl.when(i == 0)
    def _():
        for a in range(min(COMBINE_AHEAD, n_tiles)):
            fetch(a, runs_refs[a])

    @pl.when(i + COMBINE_AHEAD < n_tiles)
    def _():
        fetch(lax.rem(i + COMBINE_AHEAD, n_buf), runs_refs[COMBINE_AHEAD])

    _wait_run_copies(ys_hbm, view(buf), sem.at[buf], to_hbm=False, tok_rows=SUBLANES)

    def gather(r, carry):
        acc = None
        for k in range(TOP_K):
            row = pl.multiple_of(loc_ref[0, 0, TOP_K * r + k], SUBLANES)
            t = gate_ref[0, 0, TOP_K * r + k] * stage[pl.ds(row, SUBLANES), :]
            acc = t if acc is None else acc + t
        comb[pl.ds(pl.multiple_of(r * SUBLANES, SUBLANES), SUBLANES), :] = acc
        return carry

    lax.fori_loop(0, tm, gather, 0, unroll=4)
    f = _load_token_tiles(comb, 0, tm)
    x = x_ref[...].reshape(tm // BATCH, BATCH, D_MODEL) + mod_ref[0][None] * f.reshape(tm // BATCH, BATCH, D_MODEL)
    if final_norm:
        ms = jnp.mean(x * x, axis=-1, keepdims=True)
        x = (x * lax.rsqrt(ms + EPS)) * fw_ref[...][None]
    out_ref[...] = x.reshape(tm, D_MODEL)


def _moe_combine(runs, loc, gates_r, x, mod, fw, ys, n_ctx_rows, final_norm):
    nt = x.shape[0]
    tm = ROW_TILE
    n_tiles = nt // tm
    nct = n_ctx_rows // tm
    smem = lambda n: pl.BlockSpec((1, 1, n), lambda i: (i, 0, 0), memory_space=pltpu.SMEM)
    ahead = lambda a, i: (jnp.minimum(i + a, n_tiles - 1), 0, 0)
    return pl.pallas_call(
        functools.partial(_combine_kernel, final_norm=final_norm, n_tiles=n_tiles),
        out_shape=jax.ShapeDtypeStruct((nt, D_MODEL), F32),
        grid=(n_tiles,),
        in_specs=[
            pl.BlockSpec((1, 1, 3 * N_EXPERTS), functools.partial(ahead, a), memory_space=pltpu.SMEM)
            for a in range(COMBINE_AHEAD + 1)
        ] + [
            smem(TOP_K * tm),
            smem(TOP_K * tm),
            pl.BlockSpec((tm, D_MODEL), lambda i: (i, 0)),
            pl.BlockSpec((1, BATCH, D_MODEL), lambda i: (jnp.where(i < nct, 0, 1), 0, 0)),
            pl.BlockSpec((1, D_MODEL), lambda i: (0, 0)),
            pl.BlockSpec(memory_space=pl.ANY),
        ],
        out_specs=pl.BlockSpec((tm, D_MODEL), lambda i: (i, 0)),
        scratch_shapes=[
            pltpu.VMEM(((COMBINE_AHEAD + 1) * STAGE_TOKENS * SUBLANES, LANES), F32),
            pltpu.VMEM((tm * SUBLANES, LANES), F32),
            pltpu.SemaphoreType.DMA((COMBINE_AHEAD + 1,)),
        ],
        compiler_params=_cparams("arbitrary"),
        name="moe_combine",
    )(*([runs] * (COMBINE_AHEAD + 1)), loc, gates_r, x, mod, fw, ys)


def _moe(h2t, idx, gates, rank, tile_base, counts, x, g2, fw, layer, wgu, bgu, wd, bd, n_ctx_rows, final_norm):
    nt = idx.shape[1]
    bm = MOE_BM
    tm = ROW_TILE
    n_tiles = nt // tm
    n_blocks = (nt * TOP_K) // bm + N_EXPERTS
    cnt = counts[:, 0].astype(I32)
    padded = (cnt + bm - 1) // bm * bm
    pad_ends = jnp.cumsum(padded)
    pad_starts = pad_ends - padded
    experts = jnp.arange(N_EXPERTS, dtype=I32)
    base = tile_base[:, :, 0].astype(I32)
    run_len = jnp.concatenate([base[1:], cnt[None]], axis=0) - base
    run_off = jnp.cumsum(run_len, axis=1) - run_len
    run_row = pad_starts[None, :] + base
    runs = jnp.concatenate([run_row, run_len, run_off], axis=1).reshape(n_tiles, 1, 3 * N_EXPERTS)
    delta = (run_off - base).T
    idx3 = idx.reshape(TOP_K, n_tiles, tm)
    loc = rank.reshape(TOP_K, n_tiles, tm) + jnp.sum(
        jnp.where(idx3[None] == experts[:, None, None, None], delta[:, None, :, None], 0), axis=0)
    by_tile = lambda a: a.transpose(1, 2, 0).reshape(n_tiles, 1, tm * TOP_K)
    block_start = jnp.arange(n_blocks, dtype=I32) * bm
    block_e = jnp.minimum(jnp.sum((pad_ends[None, :] <= block_start[:, None]).astype(I32), axis=1), N_EXPERTS - 1)
    n_used = (pad_ends[-1] // bm).astype(I32).reshape(1)

    def stage_rows(tok_rows, n_buffers):
        buffer = (jnp.arange(n_tiles, dtype=I32) % n_buffers)[None, :, None]
        return by_tile((loc + buffer * STAGE_TOKENS) * tok_rows)

    xs = _moe_dispatch(pad_ends.astype(I32), padded, n_used, runs, stage_rows(PACKED_ROWS, DISPATCH_BUFFERS), h2t,
                       n_blocks)
    ys = _moe_ffn(block_e + layer * N_EXPERTS, n_used, xs, wgu, bgu, wd, bd)
    return _moe_combine(runs, stage_rows(SUBLANES, COMBINE_AHEAD + 1), by_tile(gates.reshape(TOP_K, n_tiles, tm)), x,
                        g2, fw, ys, n_ctx_rows, final_norm)


def _pad_heads(a):
    lead = a.shape[:-1]
    a = a.reshape(*lead, 4 * ML_HEADS, ML_HEAD_DIM)
    a = jnp.pad(a, [(0, 0)] * len(lead) + [(0, 0), (0, LANES - ML_HEAD_DIM)])
    return a.reshape(*lead, 4 * ML_HEADS * LANES)


def _kvqo(a):
    w = ML_HEADS * ML_HEAD_DIM
    return jnp.concatenate([a[..., w:2 * w], a[..., 2 * w:3 * w], a[..., :w], a[..., 3 * w:4 * w]], axis=-1)


def _block_diag(blocks):
    return jax.scipy.linalg.block_diag(*[blocks[g] for g in range(blocks.shape[0])])


def _s5_discretise(lam_re, lam_im, log_dt, b_re, b_im):
    dt = jnp.exp(log_dt)[:, None]
    mag = jnp.exp(lam_re * dt)
    ar, ai = mag * jnp.cos(lam_im * dt), mag * jnp.sin(lam_im * dt)
    den = lam_re * lam_re + lam_im * lam_im
    cr = ((ar - 1.0) * lam_re + ai * lam_im) / den
    ci = (ai * lam_re - (ar - 1.0) * lam_im) / den
    bbr = cr[..., None] * b_re - ci[..., None] * b_im
    bbi = cr[..., None] * b_im + ci[..., None] * b_re
    return ar, ai, bbr, bbi


def _layer_layouts(w_in, b_in, rg_wa, rg_ba, rg_wx, rg_bx, s5_lambda_re, s5_lambda_im, s5_log_dt, s5_b_re, s5_b_im,
                   s5_c_re, s5_c_im, mix_norm_w, w_out):
    dm = w_in.shape[0]
    g0 = 4 * ML_HEADS * ML_HEAD_DIM
    ng = 2 * ML_HEADS
    kscale = jnp.ones((4, ML_HEADS * LANES), F32).at[0].set(ML_HEAD_DIM ** -0.5).reshape(-1)
    ones_col = jnp.zeros((4, ML_HEADS, LANES), F32).at[1, :, ML_HEAD_DIM].set(1.0).reshape(-1)
    w_cat = jnp.concatenate([
        _pad_heads(_kvqo(w_in[:, :g0])) * kscale,
        jnp.pad(w_in[:, g0:g0 + ng], ((0, 0), (0, LANES - ng))),
        jnp.pad(w_in[:, g0 + ng:g0 + 2 * ng], ((0, 0), (0, LANES - ng))),
        w_in[:, g0 + 2 * ng:]], axis=1).astype(BF16)
    b_cat = jnp.concatenate([
        _pad_heads(_kvqo(b_in[:g0])) * kscale + ones_col,
        jnp.pad(b_in[g0:g0 + ng], (0, LANES - ng)),
        jnp.pad(b_in[g0 + ng:g0 + 2 * ng], (0, LANES - ng)),
        b_in[g0 + 2 * ng:]])[None]
    wg = [jnp.concatenate([_block_diag(rg_wa[d]), _block_diag(rg_wx[d])], axis=1).astype(BF16) for d in range(2)]
    bg = [jnp.concatenate([rg_ba[d], rg_bx[d]])[None] for d in range(2)]
    s5p = []
    for d in range(2):
        ar, ai, bbr, bbi = _s5_discretise(s5_lambda_re[d], s5_lambda_im[d], s5_log_dt[d], s5_b_re, s5_b_im)
        a = jnp.stack([ar.reshape(-1), ai.reshape(-1)])
        bd = jnp.concatenate([_block_diag(bbr.transpose(0, 2, 1)), _block_diag(bbi.transpose(0, 2, 1))],
                             axis=1).astype(BF16)
        cd = jnp.concatenate([_block_diag(s5_c_re.transpose(0, 2, 1)),
                              -_block_diag(s5_c_im.transpose(0, 2, 1))], axis=0).astype(BF16)
        s5p.append((a, bd, cd))
    wa = ML_HEADS * ML_HEAD_DIM
    mw_a = jnp.pad(mix_norm_w[:wa].reshape(ML_HEADS, ML_HEAD_DIM), ((0, 0), (0, LANES - ML_HEAD_DIM))).reshape(-1)
    mw = jnp.concatenate([mw_a, mix_norm_w[wa:]])[None]
    wo_a = jnp.pad(w_out[:wa].reshape(ML_HEADS, ML_HEAD_DIM, dm), ((0, 0), (0, LANES - ML_HEAD_DIM), (0, 0)))
    wo = jnp.concatenate([wo_a.reshape(ML_HEADS * LANES, dm), w_out[wa:]], axis=0).astype(BF16)
    return w_cat, b_cat, wg, bg, s5p, mw, wo


def kernel(x, c, ctx, c_ctx, ada_w, ada_b, norm1_w, w_in, b_in, rg_conv_w, rg_conv_b, rg_wa, rg_ba, rg_wx, rg_bx, rg_lambda, s5_lambda_re, s5_lambda_im, s5_log_dt, s5_b_re, s5_b_im, s5_c_re, s5_c_im, s5_d, s5_glu_w, s5_glu_b, mix_norm_w, w_out, norm2_w, router_w, router_b, moe_w_gate_up, moe_b_gate_up, moe_w_down, moe_b_down, final_norm_w):
    bsz, seq, dm = x.shape
    sc = ctx.shape[1]
    assert bsz == BATCH and dm == D_MODEL and seq % GRID_W == 0
    n_ctx_rows = sc * BATCH
    n_lat_rows = seq * BATCH
    assert n_ctx_rows % CHUNK_ROWS == 0 and n_lat_rows % CHUNK_ROWS == 0 and GRID_W * BATCH == ROW_TILE
    n_ctx_chunks = n_ctx_rows // CHUNK_ROWS
    depth = ada_w.shape[0]

    xs = jnp.concatenate([ctx.transpose(1, 0, 2).reshape(n_ctx_rows, dm),
                          x.transpose(1, 0, 2).reshape(n_lat_rows, dm)], axis=0)

    c_rows = jnp.concatenate([c, c_ctx[None], jnp.zeros((16 - bsz - 1, dm), F32)], axis=0)
    mods = _modulation(c_rows, ada_w, ada_b)
    mods = mods.reshape(depth, 16, 6, dm)
    mod_lat = mods[:, :bsz].transpose(0, 2, 1, 3)
    mod_ctx = jnp.broadcast_to(mods[:, bsz][:, :, None, :], mod_lat.shape)
    mod = jnp.stack([mod_ctx, mod_lat], axis=1)

    layouts = jax.vmap(_layer_layouts)(w_in, b_in, rg_wa, rg_ba, rg_wx, rg_bx, s5_lambda_re, s5_lambda_im, s5_log_dt,
                                       s5_b_re, s5_b_im, s5_c_re, s5_c_im, mix_norm_w, w_out)

    for l in range(depth):
        with_ctx = l < depth - 1
        w_cat, b_cat, wg, bg, s5p, mw, wo = jax.tree.map(lambda a: a[l], layouts)

        pa, pb, pcc, pcl = _inproj(xs, mod[l, :, 0:2], norm1_w[l][None], w_cat, b_cat, n_ctx_rows)
        pcl = pcl.reshape(n_lat_rows, S5_WIDTH)
        c0f, m0f = _mlstm_states(pa, 0, n_ctx_chunks)
        c0b, m0b = _mlstm_states(pa, 1, n_ctx_chunks)
        ya = _mlstm_outputs(pa, c0f, m0f, c0b, m0b)
        rgb = _rglru_pass(pb, None, rg_conv_w[l], rg_conv_b[l][None], wg[1], bg[1], rg_lambda[l, 1][None], 1, n_ctx_chunks)
        yb = _rglru_pass(pb, rgb, rg_conv_w[l], rg_conv_b[l][None], wg[0], bg[0], rg_lambda[l, 0][None], 0, n_ctx_chunks)
        zero_state = jnp.zeros((BATCH, 2 * S5_NSTATE), F32)
        a1, bd1, cd1 = s5p[1]
        a0, bd0, cd0 = s5p[0]
        dsk, gw, gb = s5_d[l][None], s5_glu_w[l].astype(BF16), s5_glu_b[l][None]
        ycb_c, st = _s5_pass(pcc, None, zero_state, a1, bd1, cd1, None, None, None, 1)
        ycb_l, _ = _s5_pass(pcl, None, st, a1, bd1, cd1, None, None, None, 1)
        yc_c, st = _s5_pass(pcc, ycb_c, zero_state, a0, bd0, cd0, dsk, gw, gb, 0)
        yc_l, _ = _s5_pass(pcl, ycb_l, st, a0, bd0, cd0, dsk, gw, gb, 0)

        tile0 = 0 if with_ctx else n_ctx_rows // ROW_TILE
        xo, h2, idx, gates, rank, tile_base, counts = _outproj(
            xs, ya, yb, yc_c, yc_l, mod[l, :, 2:5], mw, wo, norm2_w[l][None], router_w[l].T,
            router_b[l][:, None], tile0, n_ctx_rows)

        xs_new = _moe(h2, idx, gates, rank, tile_base, counts, xo, mod[l, :, 5], final_norm_w[None], l,
                      moe_w_gate_up.reshape(-1, dm, 2 * D_EXPERT), moe_b_gate_up.reshape(-1, 2 * D_EXPERT),
                      moe_w_down.reshape(-1, D_EXPERT, dm), moe_b_down.reshape(-1, dm),
                      n_ctx_rows if with_ctx else 0, l == depth - 1)
        xs = xs_new

    out = xs.reshape(seq, bsz, dm).transpose(1, 0, 2)
    return out
```

```python
import functools

import jax
import jax.numpy as jnp
from jax import lax
from jax.experimental import pallas as pl
from jax.experimental.pallas import tpu as pltpu

F32 = jnp.float32
BF16 = jnp.bfloat16
I32 = jnp.int32

D_MODEL = 1024
BATCH = 8
DEPTH = 4
GRID_W = 64
ML_HEADS = 4
ML_HEAD_DIM = 96
ML_CHUNK = 128
ML_M_INIT = -1e30
RG_BLOCKS = 6
RG_BLOCK_DIM = 64
RG_WIDTH = RG_BLOCKS * RG_BLOCK_DIM
RG_CONV = 4
RG_C = 8.0
S5_GROUPS = 16
S5_GROUP = 16
S5_WIDTH = S5_GROUPS * S5_GROUP
S5_STATE = 64
S5_NSTATE = S5_GROUPS * S5_STATE
N_EXPERTS = 32
TOP_K = 4
D_EXPERT = D_MODEL
SWIGLU_LIMIT = 7.0
SWIGLU_ALPHA = 1.702
EPS = 1e-6

LANES = 128
SUBLANES = 8

ML_K, ML_V, ML_Q, ML_O = 0, ML_HEADS, 2 * ML_HEADS, 3 * ML_HEADS
ML_IG = 4 * ML_HEADS
ML_FG = ML_IG + 1
ML_SLABS = ML_FG + 1
ML_STATE_SLABS = 2 * ML_HEADS
A_PAD = ML_SLABS * LANES
MERGED = ML_HEADS * LANES + RG_WIDTH + S5_WIDTH

ROW_TILE = 512
T_CHUNK = 128
CHUNK_ROWS = T_CHUNK * BATCH
MOE_BM = 512
DMA_PRIORITIES = 2
RUN_BITS = ROW_TILE.bit_length()
VMEM_LIMIT = 56 * 1024 * 1024


def _cparams(*sem):
    return pltpu.CompilerParams(dimension_semantics=sem, vmem_limit_bytes=VMEM_LIMIT)


def _sigmoid(x):
    return 0.5 * (jnp.tanh(0.5 * x) + 1.0)


def _log_sigmoid(x):
    return jnp.minimum(x, 0.0) - jnp.log1p(jnp.exp(-jnp.abs(x)))


def _softplus(x):
    return jnp.maximum(x, 0.0) + jnp.log1p(jnp.exp(-jnp.abs(x)))


def _gelu(x):
    return 0.5 * x * (1.0 + jnp.tanh(0.7978845608028654 * (x + 0.044715 * (x * x * x))))


def _dot(a, b):
    return jnp.dot(a, b, preferred_element_type=F32)


def _dot_nt(a, b):
    return lax.dot_general(a, b, (((1,), (1,)), ((), ())), preferred_element_type=F32)


def _split(a):
    hi = a.astype(BF16)
    lo = (a - hi.astype(F32)).astype(BF16)
    return hi, lo


def _dot_lx(a_exact, b):
    hi, lo = _split(b)
    return _dot(a_exact, hi) + _dot(a_exact, lo)


def _dot_xr(a, b_exact):
    hi, lo = _split(a)
    return _dot(hi, b_exact) + _dot(lo, b_exact)


def _dot3(a, b, nt=False):
    ah, al = _split(a)
    bh, bl = _split(b)
    d = _dot_nt if nt else _dot
    return d(ah, bh) + (d(ah, bl) + d(al, bh))


def _group_ones(n, shift):
    r = lax.broadcasted_iota(I32, (n, n), 0)
    c = lax.broadcasted_iota(I32, (n, n), 1)
    return (lax.shift_right_logical(r, shift) == lax.shift_right_logical(c, shift)).astype(BF16)


def _store_token_tiles(ref, tok0, val):
    n, width = val.shape
    rows = width // LANES
    for s in range(rows):
        ref[pl.ds(tok0 * rows + s, n, stride=rows), :] = val[:, s * LANES:(s + 1) * LANES]


def _load_token_tiles(ref, tok0, n, rows=SUBLANES):
    return jnp.concatenate([ref[pl.ds(tok0 * rows + s, n, stride=rows), :] for s in range(rows)], axis=1)


def _pack_bf16_pairs(x):
    w = x.shape[1] // 2
    bits = lax.bitcast_convert_type(x.astype(BF16).astype(F32), I32)
    return (bits[:, :w] & jnp.int32(-65536)) | lax.shift_right_logical(bits[:, w:], 16)


def _unpack_bf16_pairs(words):
    hi = lax.bitcast_convert_type(words & jnp.int32(-65536), F32)
    lo = lax.bitcast_convert_type(lax.shift_left(words, 16), F32)
    return jnp.concatenate([hi, lo], axis=1).astype(BF16)


def _mod_kernel(c_ref, w_ref, b_ref, o_ref):
    c = c_ref[...]
    s = c * _sigmoid(c)
    o_ref[0] = _dot3(s, w_ref[0]) + b_ref[0]


def _modulation(c_rows, ada_w, ada_b):
    depth, d, n = ada_w.shape
    tn = 1536
    return pl.pallas_call(
        _mod_kernel,
        out_shape=jax.ShapeDtypeStruct((depth, 16, n), F32),
        grid=(depth, n // tn),
        in_specs=[
            pl.BlockSpec((16, d), lambda l, j: (0, 0)),
            pl.BlockSpec((1, d, tn), lambda l, j: (l, 0, j)),
            pl.BlockSpec((1, 1, tn), lambda l, j: (l, 0, j)),
        ],
        out_specs=pl.BlockSpec((1, 16, tn), lambda l, j: (l, 0, j)),
        compiler_params=_cparams("arbitrary", "arbitrary"),
        name="adaln_mod",
    )(c_rows, ada_w, ada_b.reshape(depth, 1, n))


def _inproj_kernel(x_ref, mod_ref, nw_ref, w_ref, b_ref, pa_ref, pb_ref, pcc_ref, pcl_ref, *, n_ctx_tiles):
    i = pl.program_id(0)
    x = x_ref[...]
    tm = x.shape[0]
    ms = jnp.mean(x * x, axis=-1, keepdims=True)
    xn = (x * lax.rsqrt(ms + EPS)) * nw_ref[...]
    xn = xn.reshape(tm // BATCH, BATCH, D_MODEL)
    h = xn * (1.0 + mod_ref[0, 1][None]) + mod_ref[0, 0][None]
    h = h.reshape(tm, D_MODEL).astype(BF16)
    for j in range(0, ML_SLABS, 2):
        p = _dot(h, w_ref[:, j * LANES:(j + 2) * LANES]) + b_ref[:, j * LANES:(j + 2) * LANES]
        pa_ref[j] = p[:, :LANES]
        pa_ref[j + 1] = p[:, LANES:]
    c0 = A_PAD
    pb_ref[...] = _dot(h, w_ref[:, c0:c0 + 2 * RG_WIDTH]) + b_ref[:, c0:c0 + 2 * RG_WIDTH]
    c0 = A_PAD + 2 * RG_WIDTH
    pc = _dot(h, w_ref[:, c0:c0 + S5_WIDTH]) + b_ref[:, c0:c0 + S5_WIDTH]

    @pl.when(i < n_ctx_tiles)
    def _():
        pcc_ref[...] = pc

    @pl.when(i >= n_ctx_tiles)
    def _():
        pcl_ref[...] = pc.reshape(GRID_W, 1, BATCH, S5_WIDTH)


def _inproj(x, mod, nw, w, b, n_ctx_rows):
    nt = x.shape[0]
    tm = ROW_TILE
    nct = n_ctx_rows // tm
    n_lat_rows = nt - n_ctx_rows
    rows = n_lat_rows // tm
    ncols = w.shape[1]
    return pl.pallas_call(
        functools.partial(_inproj_kernel, n_ctx_tiles=nct),
        out_shape=(
            jax.ShapeDtypeStruct((ML_SLABS, nt, LANES), F32),
            jax.ShapeDtypeStruct((nt, 2 * RG_WIDTH), F32),
            jax.ShapeDtypeStruct((n_ctx_rows, S5_WIDTH), F32),
            jax.ShapeDtypeStruct((GRID_W, rows, BATCH, S5_WIDTH), F32),
        ),
        grid=(nt // tm,),
        in_specs=[
            pl.BlockSpec((tm, D_MODEL), lambda i: (i, 0)),
            pl.BlockSpec((1, 2, BATCH, D_MODEL), lambda i: (jnp.where(i < nct, 0, 1), 0, 0, 0)),
            pl.BlockSpec((1, D_MODEL), lambda i: (0, 0)),
            pl.BlockSpec((D_MODEL, ncols), lambda i: (0, 0)),
            pl.BlockSpec((1, ncols), lambda i: (0, 0)),
        ],
        out_specs=(
            pl.BlockSpec((ML_SLABS, tm, LANES), lambda i: (0, i, 0)),
            pl.BlockSpec((tm, 2 * RG_WIDTH), lambda i: (i, 0)),
            pl.BlockSpec((tm, S5_WIDTH), lambda i: (jnp.minimum(i, nct - 1), 0)),
            pl.BlockSpec((GRID_W, 1, BATCH, S5_WIDTH), lambda i: (0, jnp.maximum(i - nct, 0), 0, 0)),
        ),
        compiler_params=_cparams("arbitrary"),
        name="inproj",
    )(x, mod, nw, w, b)


def _mlstm_gates(ig, fg):
    L = ML_CHUNK
    row = lax.broadcasted_iota(I32, (L, L), 0)
    col = lax.broadcasted_iota(I32, (L, L), 1)
    lf = _log_sigmoid(fg)
    b_fwd = _dot_lx((row >= col).astype(BF16), lf)
    b_tot = _dot_lx(jnp.ones((L, L), BF16), lf)
    b_bwd = b_tot - b_fwd + lf
    bcol = jnp.where(col < ML_HEADS, b_fwd, b_bwd)
    return ig - bcol, bcol, b_tot


def _mlstm_state_kernel(kvf_ref, gf_ref, kvb_ref, gb_ref, c0f_ref, m0f_ref, c0b_ref, m0b_ref, c_sc, m_sc):
    i = pl.program_id(0)
    L = ML_CHUNK
    nbh = BATCH * ML_HEADS

    @pl.when(i == 0)
    def _():
        c_sc[...] = jnp.zeros(c_sc.shape, F32)
        m_sc[...] = jnp.full(m_sc.shape, ML_M_INIT, F32)

    def per_batch(b, carry):
        rows = pl.ds(b, L, stride=BATCH)
        for direction, kv_ref, g_ref, c0_ref, m0_ref in ((0, kvf_ref, gf_ref, c0f_ref, m0f_ref),
                                                         (1, kvb_ref, gb_ref, c0b_ref, m0b_ref)):
            z, _, b_tot = _mlstm_gates(g_ref[0, rows, :], g_ref[1, rows, :])
            z_t = z.T
            for h in range(ML_HEADS):
                j = ML_HEADS * direction + h
                idx = b * ML_HEADS + h
                sidx = direction * nbh + idx
                k = kv_ref[ML_K + h, rows, :].astype(BF16)
                v_t = kv_ref[ML_V + h, rows, :].T
                c0 = c_sc[sidx]
                m0_tile = m_sc[sidx]
                c0_ref[0, idx] = c0.astype(BF16)
                m0_ref[0, idx] = m0_tile
                m0 = m0_tile[0:1, :]
                bt = jnp.broadcast_to(b_tot[0:1, j:j + 1], (1, LANES))
                w = bt + z_t[j:j + 1, :]
                mloc = jnp.broadcast_to(jnp.max(w, axis=1, keepdims=True), (1, LANES))
                cloc = _dot((v_t * jnp.exp(w - mloc)).astype(BF16), k)
                mnew = jnp.maximum(bt + m0, mloc)
                a = jnp.exp(bt + m0 - mnew)
                sc = jnp.exp(mloc - mnew)
                c_sc[sidx] = a * c0 + sc * cloc
                m_sc[sidx] = jnp.broadcast_to(mnew, (SUBLANES, LANES))
        return carry

    lax.fori_loop(0, BATCH, per_batch, 0, unroll=2)


def _mlstm_out_kernel(pa_ref, c0f_ref, m0f_ref, c0b_ref, m0b_ref, out_ref):
    L = ML_CHUNK
    row = lax.broadcasted_iota(I32, (L, L), 0)
    col = lax.broadcasted_iota(I32, (L, L), 1)
    masks = (row <= col, row >= col)
    head_row = row < ML_HEAD_DIM
    m0_refs = (m0f_ref, m0b_ref)

    def per_batch(b, carry):
        rows = pl.ds(b, L, stride=BATCH)
        z, bcol, _ = _mlstm_gates(pa_ref[ML_IG, rows, :], pa_ref[ML_FG, rows, :])
        bcol_t = bcol.T
        for h in range(ML_HEADS):
            idx = b * ML_HEADS + h
            q = pa_ref[ML_Q + h, rows, :].astype(BF16)
            k = pa_ref[ML_K + h, rows, :].astype(BF16)
            v_t = pa_ref[ML_V + h, rows, :].T.astype(BF16)
            s_t = _dot_nt(k, q)
            nc_t = _dot_nt(jnp.concatenate([c0f_ref[0, idx], c0b_ref[0, idx]], axis=0), q)
            w_t = None
            h_t = None
            for d in range(2):
                j = ML_HEADS * d + h
                m0 = m0_refs[d][0, idx][0:1, :]
                r = jnp.where(masks[d], z[:, j:j + 1], -jnp.inf)
                u = jnp.maximum(jnp.max(r, axis=0, keepdims=True), m0)
                sqk = s_t * jnp.exp(r - u)
                inter = jnp.exp(m0 - u)
                ncd = nc_t[d * LANES:(d + 1) * LANES, :]
                den = jnp.sum(sqk, axis=0, keepdims=True) + inter * ncd[ML_HEAD_DIM:ML_HEAD_DIM + 1, :]
                inv = 1.0 / jnp.maximum(jnp.abs(den), jnp.exp(-(bcol_t[j:j + 1, :] + u)))
                w_t = sqk * inv if w_t is None else w_t + sqk * inv
                h_t = (inter * inv) * ncd if h_t is None else h_t + (inter * inv) * ncd
            h_t = jnp.where(head_row, h_t + _dot(v_t, w_t.astype(BF16)), 0.0)
            ms = jnp.sum(h_t * h_t, axis=0, keepdims=True) * (1.0 / ML_HEAD_DIM)
            out_ref[h, rows, :] = _sigmoid(pa_ref[ML_O + h, rows, :]) * (h_t * lax.rsqrt(ms + EPS)).T
        return carry

    lax.fori_loop(0, BATCH, per_batch, 0, unroll=4)


def _chunk_order(i, direction, n_ctx, n_all):
    if direction == 0:
        return i
    return jnp.where(i < n_ctx, n_ctx - 1 - i, n_all - 1 + n_ctx - i)


def _mlstm_states(pa, n_ctx):
    nt = pa.shape[1]
    n_all = nt // CHUNK_ROWS
    nbh = BATCH * ML_HEADS
    in_specs, out_specs = [], []
    for direction in range(2):
        order = functools.partial(_chunk_order, direction=direction, n_ctx=n_ctx, n_all=n_all)
        in_specs += [
            pl.BlockSpec((ML_STATE_SLABS, CHUNK_ROWS, LANES), lambda i, order=order: (0, order(i), 0)),
            pl.BlockSpec((2, CHUNK_ROWS, LANES), lambda i, order=order: (ML_IG // 2, order(i), 0)),
        ]
        out_specs += [
            pl.BlockSpec((1, nbh, ML_CHUNK, LANES), lambda i, order=order: (order(i), 0, 0, 0)),
            pl.BlockSpec((1, nbh, SUBLANES, LANES), lambda i, order=order: (order(i), 0, 0, 0)),
        ]
    cshape = jax.ShapeDtypeStruct((n_all, nbh, ML_CHUNK, LANES), BF16)
    mshape = jax.ShapeDtypeStruct((n_all, nbh, SUBLANES, LANES), F32)
    return pl.pallas_call(
        _mlstm_state_kernel,
        out_shape=(cshape, mshape, cshape, mshape),
        grid=(n_all,),
        in_specs=in_specs,
        out_specs=tuple(out_specs),
        scratch_shapes=[
            pltpu.VMEM((2 * nbh, ML_CHUNK, LANES), F32),
            pltpu.VMEM((2 * nbh, SUBLANES, LANES), F32),
        ],
        compiler_params=_cparams("arbitrary"),
        name="mlstm_states",
    )(pa, pa, pa, pa)


def _mlstm_outputs(pa, c0f, m0f, c0b, m0b):
    nt = pa.shape[1]
    n_all = nt // CHUNK_ROWS
    nbh = BATCH * ML_HEADS
    cspec = pl.BlockSpec((1, nbh, ML_CHUNK, LANES), lambda i: (i, 0, 0, 0))
    mspec = pl.BlockSpec((1, nbh, SUBLANES, LANES), lambda i: (i, 0, 0, 0))
    return pl.pallas_call(
        _mlstm_out_kernel,
        out_shape=jax.ShapeDtypeStruct((ML_HEADS, nt, LANES), F32),
        grid=(n_all,),
        in_specs=[pl.BlockSpec((ML_SLABS, CHUNK_ROWS, LANES), lambda i: (0, i, 0)), cspec, mspec, cspec, mspec],
        out_specs=pl.BlockSpec((ML_HEADS, CHUNK_ROWS, LANES), lambda i: (0, i, 0)),
        compiler_params=_cparams("arbitrary"),
        name="mlstm_out",
    )(pa, c0f, m0f, c0b, m0b)


def _rglru_kernel(*refs, direction, final, n_ctx, n_all):
    if final:
        cur_ref, prev_ref, next_ref, hb_ref, cw_ref, cb_ref, wg_ref, bg_ref, lam_ref, out_ref, a_sc, b_sc, h_sc = refs
    else:
        cur_ref, prev_ref, next_ref, cw_ref, cb_ref, wg_ref, bg_ref, lam_ref, out_ref, a_sc, b_sc, h_sc = refs
        hb_ref = None
    i = pl.program_id(0)
    c = _chunk_order(i, direction, n_ctx, n_all)
    w = RG_WIDTH

    @pl.when(i == 0)
    def _():
        h_sc[...] = jnp.zeros(h_sc.shape, F32)

    seg_first = jnp.logical_or(c == 0, c == n_ctx)
    seg_last = jnp.logical_or(c == n_ctx - 1, c == n_all - 1)
    cur = cur_ref[:, :w]
    prev = jnp.where(seg_first, 0.0, prev_ref[:, :w])
    nxt = jnp.where(seg_last, 0.0, next_ref[:, :w])
    xc = jnp.concatenate([prev, cur, nxt], axis=0)
    n = CHUNK_ROWS
    xconv = cb_ref[...] + cw_ref[0:1, :] * xc[0:n]
    for j in range(1, RG_CONV):
        xconv = xconv + cw_ref[j:j + 1, :] * xc[j * BATCH:j * BATCH + n]
    z = _dot(xconv.astype(BF16), wg_ref[...]) + bg_ref[...]
    r = _sigmoid(z[:, :w])
    ig = _sigmoid(z[:, w:])
    log_a = (-RG_C * _softplus(-lam_ref[...])) * r
    a = jnp.exp(log_a)
    a_sc[...] = a
    b_sc[...] = jnp.sqrt(-jnp.tanh(log_a) * (a * a + 1.0)) * (ig * xconv)

    def step(t, h):
        tt = t if direction == 0 else T_CHUNK - 1 - t
        rows = pl.ds(pl.multiple_of(tt * BATCH, BATCH), BATCH)
        h = a_sc[rows, :] * h + b_sc[rows, :]
        b_sc[rows, :] = h
        return h

    h_sc[...] = lax.fori_loop(0, T_CHUNK, step, h_sc[...], unroll=8)
    hs = b_sc[...]
    if final:
        ht = hs + hb_ref[...]
        ss = _dot_xr(ht * ht, _group_ones(w, 6)) * (1.0 / RG_BLOCK_DIM)
        hs = _gelu(cur_ref[:, w:]) * (ht * lax.rsqrt(ss + EPS))
    out_ref[...] = hs


def _rglru_pass(pb, hb, cw, cb, wg, bg, lam, direction, n_ctx):
    nt = pb.shape[0]
    n_all = nt // CHUNK_ROWS
    final = hb is not None
    w2 = 2 * RG_WIDTH
    order = lambda i: _chunk_order(i, direction, n_ctx, n_all)
    halo_p = CHUNK_ROWS // (2 * BATCH)
    halo_n = CHUNK_ROWS // BATCH
    in_specs = [
        pl.BlockSpec((CHUNK_ROWS, w2), lambda i: (order(i), 0)),
        pl.BlockSpec((2 * BATCH, w2), lambda i: (jnp.maximum(order(i) * halo_p - 1, 0), 0)),
        pl.BlockSpec((BATCH, w2), lambda i: (jnp.minimum((order(i) + 1) * halo_n, nt // BATCH - 1), 0)),
    ]
    args = [pb, pb, pb]
    if final:
        in_specs.append(pl.BlockSpec((CHUNK_ROWS, RG_WIDTH), lambda i: (order(i), 0)))
        args.append(hb)
    full = lambda s: pl.BlockSpec(s, lambda i: tuple(0 for _ in s))
    in_specs += [full((RG_CONV, RG_WIDTH)), full((1, RG_WIDTH)), full((RG_WIDTH, w2)), full((1, w2)), full((1, RG_WIDTH))]
    args += [cw, cb, wg, bg, lam]
    return pl.pallas_call(
        functools.partial(_rglru_kernel, direction=direction, final=final, n_ctx=n_ctx, n_all=n_all),
        out_shape=jax.ShapeDtypeStruct((nt, RG_WIDTH), F32),
        grid=(n_all,),
        in_specs=in_specs,
        out_specs=pl.BlockSpec((CHUNK_ROWS, RG_WIDTH), lambda i: (order(i), 0)),
        scratch_shapes=[
            pltpu.VMEM((CHUNK_ROWS, RG_WIDTH), F32),
            pltpu.VMEM((CHUNK_ROWS, RG_WIDTH), F32),
            pltpu.VMEM((BATCH, RG_WIDTH), F32),
        ],
        compiler_params=_cparams("arbitrary"),
        name="rglru_fwd" if final else "rglru_bwd",
    )(*args)


def _s5_kernel(*refs, direction, final):
    if final:
        (u_ref, yb_ref, h0_ref, a_ref, bd_ref, cd_ref, dsk_ref, gw_ref, gb_ref,
         out_ref, hN_ref, v_sc, h_sc) = refs
    else:
        u_ref, h0_ref, a_ref, bd_ref, cd_ref, out_ref, hN_ref, v_sc, h_sc = refs
    i = pl.program_id(0)
    n = S5_NSTATE

    @pl.when(i == 0)
    def _():
        h_sc[...] = h0_ref[...]

    u = u_ref[...]
    v_sc[...] = _dot(u.astype(BF16), bd_ref[...])
    ar = jnp.broadcast_to(a_ref[0:1, :], (BATCH, n))
    ai = jnp.broadcast_to(a_ref[1:2, :], (BATCH, n))

    def step(t, carry):
        hr, hi = carry
        tt = t if direction == 0 else T_CHUNK - 1 - t
        rows = pl.ds(pl.multiple_of(tt * BATCH, BATCH), BATCH)
        nr = ar * hr - ai * hi + v_sc[rows, :n]
        ni = ar * hi + ai * hr + v_sc[rows, n:]
        v_sc[rows, :n] = nr
        v_sc[rows, n:] = ni
        return nr, ni

    hr, hi = lax.fori_loop(0, T_CHUNK, step, (h_sc[:, :n], h_sc[:, n:]))
    h_sc[:, :n] = hr
    h_sc[:, n:] = hi
    hN_ref[...] = h_sc[...]
    y = _dot(v_sc[...].astype(BF16), cd_ref[...])
    if final:
        y = y + yb_ref[...] + dsk_ref[...] * u
        g = _gelu(y)
        o = g * _sigmoid(_dot(g.astype(BF16), gw_ref[...]) + gb_ref[...])
        ss = _dot_xr(o * o, _group_ones(S5_WIDTH, 4)) * (1.0 / S5_GROUP)
        y = o * lax.rsqrt(ss + EPS)
    out_ref[...] = y


def _s5_pass(u, yb, h0, a, bd, cd, dsk, gw, gb, direction):
    nt = u.shape[0]
    n_chunks = nt // CHUNK_ROWS
    final = yb is not None
    order = (lambda i: i) if direction == 0 else (lambda i: n_chunks - 1 - i)
    full = lambda s: pl.BlockSpec(s, lambda i: tuple(0 for _ in s))
    chunk = pl.BlockSpec((CHUNK_ROWS, S5_WIDTH), lambda i: (order(i), 0))
    in_specs = [chunk]
    args = [u]
    if final:
        in_specs.append(chunk)
        args.append(yb)
    in_specs += [full((BATCH, 2 * S5_NSTATE)), full((2, S5_NSTATE)), full((S5_WIDTH, 2 * S5_NSTATE)),
                 full((2 * S5_NSTATE, S5_WIDTH))]
    args += [h0, a, bd, cd]
    if final:
        in_specs += [full((1, S5_WIDTH)), full((S5_WIDTH, S5_WIDTH)), full((1, S5_WIDTH))]
        args += [dsk, gw, gb]
    return pl.pallas_call(
        functools.partial(_s5_kernel, direction=direction, final=final),
        out_shape=(jax.ShapeDtypeStruct((nt, S5_WIDTH), F32), jax.ShapeDtypeStruct((BATCH, 2 * S5_NSTATE), F32)),
        grid=(n_chunks,),
        in_specs=in_specs,
        out_specs=(chunk, full((BATCH, 2 * S5_NSTATE))),
        scratch_shapes=[
            pltpu.VMEM((CHUNK_ROWS, 2 * S5_NSTATE), F32),
            pltpu.VMEM((BATCH, 2 * S5_NSTATE), F32),
        ],
        compiler_params=_cparams("arbitrary"),
        name="s5_fwd" if final else "s5_bwd",
    )(*args)


def _outproj_kernel(x_ref, ya_ref, yb_ref, ycc_ref, ycl_ref, mod_ref, mw_ref, wo_ref, n2_ref, rw_ref, rb_ref,
                    xo_ref, h2_ref, idx_ref, gate_ref, rank_ref, base_ref, cnt_ref, cnt_sc, *, tile0, n_ctx_tiles):
    i = pl.program_id(0)
    tile = i + tile0
    tm = x_ref.shape[0]

    @pl.when(i == 0)
    def _():
        cnt_sc[...] = jnp.zeros(cnt_sc.shape, F32)

    yc = jnp.where(tile < n_ctx_tiles, ycc_ref[...], ycl_ref[...].reshape(tm, S5_WIDTH))
    merged = jnp.concatenate([ya_ref[h] for h in range(ML_HEADS)] + [yb_ref[...], yc], axis=1)
    merged = (merged * mw_ref[...]).astype(BF16)
    proj = _dot(merged, wo_ref[...])
    g1 = mod_ref[0, 0]
    x = x_ref[...].reshape(tm // BATCH, BATCH, D_MODEL) + g1[None] * proj.reshape(tm // BATCH, BATCH, D_MODEL)
    xo_ref[...] = x.reshape(tm, D_MODEL)
    ms = jnp.mean(x * x, axis=-1, keepdims=True)
    xn = (x * lax.rsqrt(ms + EPS)) * n2_ref[...][None]
    h2 = (xn * (1.0 + mod_ref[0, 2][None]) + mod_ref[0, 1][None]).reshape(tm, D_MODEL)
    _store_token_tiles(h2_ref, 0, _pack_bf16_pairs(h2))
    logits = _dot3(rw_ref[...], h2, nt=True) + rb_ref[...]
    eidx = lax.broadcasted_iota(I32, (N_EXPERTS, tm), 0)
    vals, ids, hots = [], [], []
    for _ in range(TOP_K):
        m = jnp.max(logits, axis=0, keepdims=True)
        sel = jnp.min(jnp.where(logits == m, eidx, N_EXPERTS), axis=0, keepdims=True)
        hot = eidx == sel
        logits = jnp.where(hot, -jnp.inf, logits)
        vals.append(m)
        ids.append(sel)
        hots.append(hot)
    ex = [jnp.exp(v - vals[0]) for v in vals]
    tot = ex[0] + ex[1] + ex[2] + ex[3]
    idx_ref[...] = jnp.concatenate(ids, axis=0)
    gate_ref[...] = jnp.concatenate([e / tot for e in ex], axis=0)
    selm = hots[0].astype(F32) + hots[1].astype(F32) + hots[2].astype(F32) + hots[3].astype(F32)
    r = lax.broadcasted_iota(I32, (tm, tm), 0)
    c = lax.broadcasted_iota(I32, (tm, tm), 1)
    before = _dot(selm.astype(BF16), (r < c).astype(BF16)) + cnt_sc[:, 0:1]
    ranks = [jnp.sum(jnp.where(hot, before, 0.0), axis=0, keepdims=True) for hot in hots]
    rank_ref[...] = jnp.concatenate(ranks, axis=0).astype(I32)
    base_ref[0] = cnt_sc[...]
    cnt_sc[...] = cnt_sc[...] + jnp.sum(selm, axis=1, keepdims=True)
    cnt_ref[...] = cnt_sc[...]


def _outproj(x, ya, yb, ycc, ycl, mod, mw, wo, n2, rw, rb, tile0, n_ctx_rows):
    nt = x.shape[0]
    tm = ROW_TILE
    nct = n_ctx_rows // tm
    n_tiles = nt // tm - tile0
    n_out = n_tiles * tm
    rows = ycl.shape[0] // tm
    ycl4 = ycl.reshape(GRID_W, rows, BATCH, S5_WIDTH)
    full = lambda s: pl.BlockSpec(s, lambda i: tuple(0 for _ in s))
    return pl.pallas_call(
        functools.partial(_outproj_kernel, tile0=tile0, n_ctx_tiles=nct),
        out_shape=(
            jax.ShapeDtypeStruct((n_out, D_MODEL), F32),
            jax.ShapeDtypeStruct((n_out * PACKED_ROWS, LANES), I32),
            jax.ShapeDtypeStruct((TOP_K, n_out), I32),
            jax.ShapeDtypeStruct((TOP_K, n_out), F32),
            jax.ShapeDtypeStruct((TOP_K, n_out), I32),
            jax.ShapeDtypeStruct((n_tiles, N_EXPERTS, LANES), F32),
            jax.ShapeDtypeStruct((N_EXPERTS, LANES), F32),
        ),
        grid=(n_tiles,),
        in_specs=[
            pl.BlockSpec((tm, D_MODEL), lambda i: (i + tile0, 0)),
            pl.BlockSpec((ML_HEADS, tm, LANES), lambda i: (0, i + tile0, 0)),
            pl.BlockSpec((tm, RG_WIDTH), lambda i: (i + tile0, 0)),
            pl.BlockSpec((tm, S5_WIDTH), lambda i: (jnp.minimum(i + tile0, nct - 1), 0)),
            pl.BlockSpec((GRID_W, 1, BATCH, S5_WIDTH), lambda i: (0, jnp.maximum(i + tile0 - nct, 0), 0, 0)),
            pl.BlockSpec((1, 3, BATCH, D_MODEL), lambda i: (jnp.where(i + tile0 < nct, 0, 1), 0, 0, 0)),
            full((1, MERGED)),
            full((MERGED, D_MODEL)),
            full((1, D_MODEL)),
            full((N_EXPERTS, D_MODEL)),
            full((N_EXPERTS, 1)),
        ],
        out_specs=(
            pl.BlockSpec((tm, D_MODEL), lambda i: (i, 0)),
            pl.BlockSpec((tm * PACKED_ROWS, LANES), lambda i: (i, 0)),
            pl.BlockSpec((TOP_K, tm), lambda i: (0, i)),
            pl.BlockSpec((TOP_K, tm), lambda i: (0, i)),
            pl.BlockSpec((TOP_K, tm), lambda i: (0, i)),
            pl.BlockSpec((1, N_EXPERTS, LANES), lambda i: (i, 0, 0)),
            full((N_EXPERTS, LANES)),
        ),
        scratch_shapes=[pltpu.VMEM((N_EXPERTS, LANES), F32)],
        compiler_params=_cparams("arbitrary"),
        name="outproj_router",
    )(x, ya, yb, ycc, ycl4, mod, mw, wo, n2, rw, rb)


STAGE_TOKENS = TOP_K * ROW_TILE
COMBINE_AHEAD = 2
DISPATCH_BUFFERS = 2
PACKED_ROWS = D_MODEL // (2 * LANES)


def _start_run_copies(hbm, stage, runs_ref, sem, to_hbm, tok_rows):
    def per_expert(e, carry):
        row = runs_ref[0, 0, e]
        length = runs_ref[0, 0, N_EXPERTS + e]
        off = runs_ref[0, 0, 2 * N_EXPERTS + e]
        for bit in range(RUN_BITS):
            size = (1 << bit) * tok_rows

            @pl.when((lax.shift_right_logical(length, bit) & 1) == 1)
            def _():
                done = lax.shift_left(lax.shift_right_logical(length, bit + 1), bit + 1)
                h = hbm.at[pl.ds(pl.multiple_of((row + done) * tok_rows, tok_rows), size), :]
                s = stage.at[pl.ds(pl.multiple_of((off + done) * tok_rows, tok_rows), size), :]
                cp = pltpu.make_async_copy(s, h, sem) if to_hbm else pltpu.make_async_copy(h, s, sem)
                cp.start(priority=bit % DMA_PRIORITIES)
        return carry

    lax.fori_loop(0, N_EXPERTS, per_expert, 0)


def _wait_run_copies(hbm, stage, sem, to_hbm, tok_rows):
    h = hbm.at[pl.ds(0, STAGE_TOKENS * tok_rows), :]
    cp = pltpu.make_async_copy(stage, h, sem) if to_hbm else pltpu.make_async_copy(h, stage, sem)
    cp.wait()


def _dispatch_kernel(pe_ref, pd_ref, nu_ref, runs_ref, loc_ref, h_ref, xs_hbm, stage, zero_sc, sem, zsem, *,
                     n_blocks, n_tiles):
    i = pl.program_id(0)
    tm = ROW_TILE
    tr = PACKED_ROWS

    def zero_block(blk):
        start = pl.multiple_of(blk * (MOE_BM * tr), SUBLANES)
        return pltpu.make_async_copy(zero_sc, xs_hbm.at[pl.ds(start, MOE_BM * tr), :], zsem)

    @pl.when(i == 0)
    def _():
        zero_sc[...] = jnp.zeros(zero_sc.shape, I32)
        for e in range(N_EXPERTS):
            @pl.when(pd_ref[e] > 0)
            def _():
                zero_block(pe_ref[e] // MOE_BM - 1).start()
        for e in range(N_EXPERTS):
            @pl.when(pd_ref[e] > 0)
            def _():
                zero_block(0).wait()

        def fill(blk, carry):
            cp = zero_block(blk)
            cp.start()
            cp.wait()
            return carry

        lax.fori_loop(nu_ref[0], n_blocks, fill, 0)

    buf = lax.rem(i, DISPATCH_BUFFERS)

    def place(r, carry):
        t = h_ref[pl.ds(pl.multiple_of(r * tr, tr), tr), :]
        for k in range(TOP_K):
            stage[pl.ds(pl.multiple_of(loc_ref[0, 0, TOP_K * r + k], tr), tr), :] = t
        return carry

    lax.fori_loop(0, tm, place, 0, unroll=4)
    rows = STAGE_TOKENS * tr
    view = lambda b: stage.at[pl.ds(pl.multiple_of(b * rows, SUBLANES), rows), :]
    _start_run_copies(xs_hbm, view(buf), runs_ref, sem.at[buf], to_hbm=True, tok_rows=tr)

    @pl.when(i > 0)
    def _():
        _wait_run_copies(xs_hbm, view(1 - buf), sem.at[1 - buf], to_hbm=True, tok_rows=tr)

    @pl.when(i == n_tiles - 1)
    def _():
        _wait_run_copies(xs_hbm, view(buf), sem.at[buf], to_hbm=True, tok_rows=tr)


def _moe_dispatch(pad_ends, padded, n_used, runs, loc, h2t, n_blocks):
    tm = ROW_TILE
    n_tiles = runs.shape[0]
    smem = lambda n: pl.BlockSpec((1, 1, n), lambda i, pe, pd, nu: (i, 0, 0), memory_space=pltpu.SMEM)
    grid_spec = pltpu.PrefetchScalarGridSpec(
        num_scalar_prefetch=3,
        grid=(n_tiles,),
        in_specs=[
            smem(3 * N_EXPERTS),
            smem(TOP_K * tm),
            pl.BlockSpec((tm * PACKED_ROWS, LANES), lambda i, pe, pd, nu: (i, 0)),
        ],
        out_specs=pl.BlockSpec(memory_space=pl.ANY),
        scratch_shapes=[
            pltpu.VMEM((DISPATCH_BUFFERS * STAGE_TOKENS * PACKED_ROWS, LANES), I32),
            pltpu.VMEM((MOE_BM * PACKED_ROWS, LANES), I32),
            pltpu.SemaphoreType.DMA((2,)),
            pltpu.SemaphoreType.DMA(()),
        ],
    )
    return pl.pallas_call(
        functools.partial(_dispatch_kernel, n_blocks=n_blocks, n_tiles=n_tiles),
        out_shape=jax.ShapeDtypeStruct((n_blocks * MOE_BM * PACKED_ROWS, LANES), I32),
        grid_spec=grid_spec,
        compiler_params=_cparams("arbitrary"),
        name="moe_dispatch",
    )(pad_ends, padded, n_used, runs, loc, h2t)


def _ffn_kernel(be_ref, nu_ref, x_ref, wgu_ref, bgu_ref, wd_ref, bd_ref, y_ref, wgu_sc, wd_sc):
    i = pl.program_id(0)
    changed = jnp.logical_or(i == 0, be_ref[i] != be_ref[jnp.maximum(i - 1, 0)])

    @pl.when(changed)
    def _():
        wgu_sc[...] = wgu_ref[0].astype(BF16)
        wd_sc[...] = wd_ref[0].astype(BF16)

    @pl.when(i < nu_ref[0])
    def _():
        x = _unpack_bf16_pairs(_load_token_tiles(x_ref, 0, MOE_BM, PACKED_ROWS))
        gu = _dot(x, wgu_sc[...]) + bgu_ref[0]
        gate = jnp.minimum(gu[:, :D_EXPERT], SWIGLU_LIMIT)
        up = jnp.clip(gu[:, D_EXPERT:], -SWIGLU_LIMIT, SWIGLU_LIMIT)
        glu = gate * _sigmoid(SWIGLU_ALPHA * gate)
        act = ((up + 1.0) * glu).astype(BF16)
        _store_token_tiles(y_ref, 0, _dot(act, wd_sc[...]) + bd_ref[0])

    @pl.when(i >= nu_ref[0])
    def _():
        y_ref[...] = jnp.zeros(y_ref.shape, F32)


def _moe_ffn(block_e, n_used, xs, wgu, bgu, wd, bd):
    rows = MOE_BM * SUBLANES
    in_rows = MOE_BM * PACKED_ROWS
    n_blocks = xs.shape[0] // in_rows
    ne = wgu.shape[0]
    grid_spec = pltpu.PrefetchScalarGridSpec(
        num_scalar_prefetch=2,
        grid=(n_blocks,),
        in_specs=[
            pl.BlockSpec((in_rows, LANES), lambda i, be, nu: (jnp.minimum(i, nu[0] - 1), 0)),
            pl.BlockSpec((1, D_MODEL, 2 * D_EXPERT), lambda i, be, nu: (be[i], 0, 0)),
            pl.BlockSpec((1, 1, 2 * D_EXPERT), lambda i, be, nu: (be[i], 0, 0)),
            pl.BlockSpec((1, D_EXPERT, D_MODEL), lambda i, be, nu: (be[i], 0, 0)),
            pl.BlockSpec((1, 1, D_MODEL), lambda i, be, nu: (be[i], 0, 0)),
        ],
        out_specs=pl.BlockSpec((rows, LANES), lambda i, be, nu: (i, 0)),
        scratch_shapes=[
            pltpu.VMEM((D_MODEL, 2 * D_EXPERT), BF16),
            pltpu.VMEM((D_EXPERT, D_MODEL), BF16),
        ],
    )
    return pl.pallas_call(
        _ffn_kernel,
        out_shape=jax.ShapeDtypeStruct((n_blocks * rows, LANES), F32),
        grid_spec=grid_spec,
        compiler_params=_cparams("arbitrary"),
        name="moe_ffn",
    )(block_e, n_used, xs, wgu, bgu.reshape(ne, 1, -1), wd, bd.reshape(ne, 1, -1))


def _combine_kernel(*refs, final_norm, n_tiles):
    runs_refs = refs[:COMBINE_AHEAD + 1]
    loc_ref, gate_ref, x_ref, mod_ref, fw_ref, ys_hbm, out_ref, stage, comb, sem = refs[COMBINE_AHEAD + 1:]
    i = pl.program_id(0)
    tm = x_ref.shape[0]
    n_buf = COMBINE_AHEAD + 1
    buf = lax.rem(i, n_buf)

    rows = STAGE_TOKENS * SUBLANES
    view = lambda b: stage.at[pl.ds(pl.multiple_of(b * rows, SUBLANES), rows), :]

    def fetch(slot, runs_ref):
        _start_run_copies(ys_hbm, view(slot), runs_ref, sem.at[slot], to_hbm=False, tok_rows=SUBLANES)

    @pl.when(i == 0)
    def _():
        for a in range(min(COMBINE_AHEAD, n_tiles)):
            fetch(a, runs_refs[a])

    @pl.when(i + COMBINE_AHEAD < n_tiles)
    def _():
        fetch(lax.rem(i + COMBINE_AHEAD, n_buf), runs_refs[COMBINE_AHEAD])

    _wait_run_copies(ys_hbm, view(buf), sem.at[buf], to_hbm=False, tok_rows=SUBLANES)

    def gather(r, carry):
        acc = None
        for k in range(TOP_K):
            row = pl.multiple_of(loc_ref[0, 0, TOP_K * r + k], SUBLANES)
            t = gate_ref[0, 0, TOP_K * r + k] * stage[pl.ds(row, SUBLANES), :]
            acc = t if acc is None else acc + t
        comb[pl.ds(pl.multiple_of(r * SUBLANES, SUBLANES), SUBLANES), :] = acc
        return carry

    lax.fori_loop(0, tm, gather, 0, unroll=4)
    f = _load_token_tiles(comb, 0, tm)
    x = x_ref[...].reshape(tm // BATCH, BATCH, D_MODEL) + mod_ref[0][None] * f.reshape(tm // BATCH, BATCH, D_MODEL)
    if final_norm:
        ms = jnp.mean(x * x, axis=-1, keepdims=True)
        x = (x * lax.rsqrt(ms + EPS)) * fw_ref[...][None]
    out_ref[...] = x.reshape(tm, D_MODEL)


def _moe_combine(runs, loc, gates_r, x, mod, fw, ys, n_ctx_rows, final_norm):
    nt = x.shape[0]
    tm = ROW_TILE
    n_tiles = nt // tm
    nct = n_ctx_rows // tm
    smem = lambda n: pl.BlockSpec((1, 1, n), lambda i: (i, 0, 0), memory_space=pltpu.SMEM)
    ahead = lambda a, i: (jnp.minimum(i + a, n_tiles - 1), 0, 0)
    return pl.pallas_call(
        functools.partial(_combine_kernel, final_norm=final_norm, n_tiles=n_tiles),
        out_shape=jax.ShapeDtypeStruct((nt, D_MODEL), F32),
        grid=(n_tiles,),
        in_specs=[
            pl.BlockSpec((1, 1, 3 * N_EXPERTS), functools.partial(ahead, a), memory_space=pltpu.SMEM)
            for a in range(COMBINE_AHEAD + 1)
        ] + [
            smem(TOP_K * tm),
            smem(TOP_K * tm),
            pl.BlockSpec((tm, D_MODEL), lambda i: (i, 0)),
            pl.BlockSpec((1, BATCH, D_MODEL), lambda i: (jnp.where(i < nct, 0, 1), 0, 0)),
            pl.BlockSpec((1, D_MODEL), lambda i: (0, 0)),
            pl.BlockSpec(memory_space=pl.ANY),
        ],
        out_specs=pl.BlockSpec((tm, D_MODEL), lambda i: (i, 0)),
        scratch_shapes=[
            pltpu.VMEM(((COMBINE_AHEAD + 1) * STAGE_TOKENS * SUBLANES, LANES), F32),
            pltpu.VMEM((tm * SUBLANES, LANES), F32),
            pltpu.SemaphoreType.DMA((COMBINE_AHEAD + 1,)),
        ],
        compiler_params=_cparams("arbitrary"),
        name="moe_combine",
    )(*([runs] * (COMBINE_AHEAD + 1)), loc, gates_r, x, mod, fw, ys)


def _moe(h2t, idx, gates, rank, tile_base, counts, x, g2, fw, layer, wgu, bgu, wd, bd, n_ctx_rows, final_norm):
    nt = idx.shape[1]
    bm = MOE_BM
    tm = ROW_TILE
    n_tiles = nt // tm
    n_blocks = (nt * TOP_K) // bm + N_EXPERTS
    cnt = counts[:, 0].astype(I32)
    padded = (cnt + bm - 1) // bm * bm
    pad_ends = jnp.cumsum(padded)
    pad_starts = pad_ends - padded
    experts = jnp.arange(N_EXPERTS, dtype=I32)
    base = tile_base[:, :, 0].astype(I32)
    run_len = jnp.concatenate([base[1:], cnt[None]], axis=0) - base
    run_off = jnp.cumsum(run_len, axis=1) - run_len
    run_row = pad_starts[None, :] + base
    runs = jnp.concatenate([run_row, run_len, run_off], axis=1).reshape(n_tiles, 1, 3 * N_EXPERTS)
    delta = (run_off - base).T
    idx3 = idx.reshape(TOP_K, n_tiles, tm)
    loc = rank.reshape(TOP_K, n_tiles, tm) + jnp.sum(
        jnp.where(idx3[None] == experts[:, None, None, None], delta[:, None, :, None], 0), axis=0)
    by_tile = lambda a: a.transpose(1, 2, 0).reshape(n_tiles, 1, tm * TOP_K)
    block_start = jnp.arange(n_blocks, dtype=I32) * bm
    block_e = jnp.minimum(jnp.sum((pad_ends[None, :] <= block_start[:, None]).astype(I32), axis=1), N_EXPERTS - 1)
    n_used = (pad_ends[-1] // bm).astype(I32).reshape(1)

    def stage_rows(tok_rows, n_buffers):
        buffer = (jnp.arange(n_tiles, dtype=I32) % n_buffers)[None, :, None]
        return by_tile((loc + buffer * STAGE_TOKENS) * tok_rows)

    xs = _moe_dispatch(pad_ends.astype(I32), padded, n_used, runs, stage_rows(PACKED_ROWS, DISPATCH_BUFFERS), h2t,
                       n_blocks)
    ys = _moe_ffn(block_e + layer * N_EXPERTS, n_used, xs, wgu, bgu, wd, bd)
    return _moe_combine(runs, stage_rows(SUBLANES, COMBINE_AHEAD + 1), by_tile(gates.reshape(TOP_K, n_tiles, tm)), x,
                        g2, fw, ys, n_ctx_rows, final_norm)


def _pad_heads(a):
    lead = a.shape[:-1]
    a = a.reshape(*lead, 4 * ML_HEADS, ML_HEAD_DIM)
    a = jnp.pad(a, [(0, 0)] * len(lead) + [(0, 0), (0, LANES - ML_HEAD_DIM)])
    return a.reshape(*lead, 4 * ML_HEADS * LANES)


def _kvqo(a):
    w = ML_HEADS * ML_HEAD_DIM
    return jnp.concatenate([a[..., w:2 * w], a[..., 2 * w:3 * w], a[..., :w], a[..., 3 * w:4 * w]], axis=-1)


def _block_diag(blocks):
    return jax.scipy.linalg.block_diag(*[blocks[g] for g in range(blocks.shape[0])])


def _s5_discretise(lam_re, lam_im, log_dt, b_re, b_im):
    dt = jnp.exp(log_dt)[:, None]
    mag = jnp.exp(lam_re * dt)
    ar, ai = mag * jnp.cos(lam_im * dt), mag * jnp.sin(lam_im * dt)
    den = lam_re * lam_re + lam_im * lam_im
    cr = ((ar - 1.0) * lam_re + ai * lam_im) / den
    ci = (ai * lam_re - (ar - 1.0) * lam_im) / den
    bbr = cr[..., None] * b_re - ci[..., None] * b_im
    bbi = cr[..., None] * b_im + ci[..., None] * b_re
    return ar, ai, bbr, bbi


def _layer_layouts(w_in, b_in, rg_wa, rg_ba, rg_wx, rg_bx, s5_lambda_re, s5_lambda_im, s5_log_dt, s5_b_re, s5_b_im,
                   s5_c_re, s5_c_im, mix_norm_w, w_out):
    dm = w_in.shape[0]
    g0 = 4 * ML_HEADS * ML_HEAD_DIM
    ng = 2 * ML_HEADS
    kscale = jnp.ones((4, ML_HEADS * LANES), F32).at[0].set(ML_HEAD_DIM ** -0.5).reshape(-1)
    ones_col = jnp.zeros((4, ML_HEADS, LANES), F32).at[1, :, ML_HEAD_DIM].set(1.0).reshape(-1)
    w_cat = jnp.concatenate([
        _pad_heads(_kvqo(w_in[:, :g0])) * kscale,
        jnp.pad(w_in[:, g0:g0 + ng], ((0, 0), (0, LANES - ng))),
        jnp.pad(w_in[:, g0 + ng:g0 + 2 * ng], ((0, 0), (0, LANES - ng))),
        w_in[:, g0 + 2 * ng:]], axis=1).astype(BF16)
    b_cat = jnp.concatenate([
        _pad_heads(_kvqo(b_in[:g0])) * kscale + ones_col,
        jnp.pad(b_in[g0:g0 + ng], (0, LANES - ng)),
        jnp.pad(b_in[g0 + ng:g0 + 2 * ng], (0, LANES - ng)),
        b_in[g0 + 2 * ng:]])[None]
    wg = [jnp.concatenate([_block_diag(rg_wa[d]), _block_diag(rg_wx[d])], axis=1).astype(BF16) for d in range(2)]
    bg = [jnp.concatenate([rg_ba[d], rg_bx[d]])[None] for d in range(2)]
    s5p = []
    for d in range(2):
        ar, ai, bbr, bbi = _s5_discretise(s5_lambda_re[d], s5_lambda_im[d], s5_log_dt[d], s5_b_re, s5_b_im)
        a = jnp.stack([ar.reshape(-1), ai.reshape(-1)])
        bd = jnp.concatenate([_block_diag(bbr.transpose(0, 2, 1)), _block_diag(bbi.transpose(0, 2, 1))],
                             axis=1).astype(BF16)
        cd = jnp.concatenate([_block_diag(s5_c_re.transpose(0, 2, 1)),
                              -_block_diag(s5_c_im.transpose(0, 2, 1))], axis=0).astype(BF16)
        s5p.append((a, bd, cd))
    wa = ML_HEADS * ML_HEAD_DIM
    mw_a = jnp.pad(mix_norm_w[:wa].reshape(ML_HEADS, ML_HEAD_DIM), ((0, 0), (0, LANES - ML_HEAD_DIM))).reshape(-1)
    mw = jnp.concatenate([mw_a, mix_norm_w[wa:]])[None]
    wo_a = jnp.pad(w_out[:wa].reshape(ML_HEADS, ML_HEAD_DIM, dm), ((0, 0), (0, LANES - ML_HEAD_DIM), (0, 0)))
    wo = jnp.concatenate([wo_a.reshape(ML_HEADS * LANES, dm), w_out[wa:]], axis=0).astype(BF16)
    return w_cat, b_cat, wg, bg, s5p, mw, wo


def kernel(x, c, ctx, c_ctx, ada_w, ada_b, norm1_w, w_in, b_in, rg_conv_w, rg_conv_b, rg_wa, rg_ba, rg_wx, rg_bx, rg_lambda, s5_lambda_re, s5_lambda_im, s5_log_dt, s5_b_re, s5_b_im, s5_c_re, s5_c_im, s5_d, s5_glu_w, s5_glu_b, mix_norm_w, w_out, norm2_w, router_w, router_b, moe_w_gate_up, moe_b_gate_up, moe_w_down, moe_b_down, final_norm_w):
    bsz, seq, dm = x.shape
    sc = ctx.shape[1]
    assert bsz == BATCH and dm == D_MODEL and seq % GRID_W == 0
    n_ctx_rows = sc * BATCH
    n_lat_rows = seq * BATCH
    assert n_ctx_rows % CHUNK_ROWS == 0 and n_lat_rows % CHUNK_ROWS == 0 and GRID_W * BATCH == ROW_TILE
    n_ctx_chunks = n_ctx_rows // CHUNK_ROWS
    depth = ada_w.shape[0]

    xs = jnp.concatenate([ctx.transpose(1, 0, 2).reshape(n_ctx_rows, dm),
                          x.transpose(1, 0, 2).reshape(n_lat_rows, dm)], axis=0)

    c_rows = jnp.concatenate([c, c_ctx[None], jnp.zeros((16 - bsz - 1, dm), F32)], axis=0)
    mods = _modulation(c_rows, ada_w, ada_b)
    mods = mods.reshape(depth, 16, 6, dm)
    mod_lat = mods[:, :bsz].transpose(0, 2, 1, 3)
    mod_ctx = jnp.broadcast_to(mods[:, bsz][:, :, None, :], mod_lat.shape)
    mod = jnp.stack([mod_ctx, mod_lat], axis=1)

    layouts = jax.vmap(_layer_layouts)(w_in, b_in, rg_wa, rg_ba, rg_wx, rg_bx, s5_lambda_re, s5_lambda_im, s5_log_dt,
                                       s5_b_re, s5_b_im, s5_c_re, s5_c_im, mix_norm_w, w_out)

    for l in range(depth):
        with_ctx = l < depth - 1
        w_cat, b_cat, wg, bg, s5p, mw, wo = jax.tree.map(lambda a: a[l], layouts)

        pa, pb, pcc, pcl = _inproj(xs, mod[l, :, 0:2], norm1_w[l][None], w_cat, b_cat, n_ctx_rows)
        pcl = pcl.reshape(n_lat_rows, S5_WIDTH)
        c0f, m0f, c0b, m0b = _mlstm_states(pa, n_ctx_chunks)
        ya = _mlstm_outputs(pa, c0f, m0f, c0b, m0b)
        rgb = _rglru_pass(pb, None, rg_conv_w[l], rg_conv_b[l][None], wg[1], bg[1], rg_lambda[l, 1][None], 1, n_ctx_chunks)
        yb = _rglru_pass(pb, rgb, rg_conv_w[l], rg_conv_b[l][None], wg[0], bg[0], rg_lambda[l, 0][None], 0, n_ctx_chunks)
        zero_state = jnp.zeros((BATCH, 2 * S5_NSTATE), F32)
        a1, bd1, cd1 = s5p[1]
        a0, bd0, cd0 = s5p[0]
        dsk, gw, gb = s5_d[l][None], s5_glu_w[l].astype(BF16), s5_glu_b[l][None]
        ycb_c, st = _s5_pass(pcc, None, zero_state, a1, bd1, cd1, None, None, None, 1)
        ycb_l, _ = _s5_pass(pcl, None, st, a1, bd1, cd1, None, None, None, 1)
        yc_c, st = _s5_pass(pcc, ycb_c, zero_state, a0, bd0, cd0, dsk, gw, gb, 0)
        yc_l, _ = _s5_pass(pcl, ycb_l, st, a0, bd0, cd0, dsk, gw, gb, 0)

        tile0 = 0 if with_ctx else n_ctx_rows // ROW_TILE
        xo, h2, idx, gates, rank, tile_base, counts = _outproj(
            xs, ya, yb, yc_c, yc_l, mod[l, :, 2:5], mw, wo, norm2_w[l][None], router_w[l].T,
            router_b[l][:, None], tile0, n_ctx_rows)

        xs_new = _moe(h2, idx, gates, rank, tile_base, counts, xo, mod[l, :, 5], final_norm_w[None], l,
                      moe_w_gate_up.reshape(-1, dm, 2 * D_EXPERT), moe_b_gate_up.reshape(-1, 2 * D_EXPERT),
                      moe_w_down.reshape(-1, D_EXPERT, dm), moe_b_down.reshape(-1, dm),
                      n_ctx_rows if with_ctx else 0, l == depth - 1)
        xs = xs_new

    out = xs.reshape(seq, bsz, dm).transpose(1, 0, 2)
    return out
```
